```python
import jax, jax.numpy as jnp
from jax import lax
import numpy as np

D_MODEL = 1024
BATCH = 32
SEQ = 2048
DEPTH = 4

F32 = jnp.float32
N_MIXERS = 4
EPS = 1e-6
HG_HEAD_DIM = 128
HG_HEADS = D_MODEL // HG_HEAD_DIM
HG_CHUNK = 32
SW_HEAD_DIM = 64
SW_HEADS = D_MODEL // SW_HEAD_DIM
SW_KV_HEADS = 4
SW_WINDOW = 128
ROPE_THETA = 10000.0
GM_WIDTH = D_MODEL
GM_GROUPS = 8
GM_CHUNK = 128
GD_HEAD_DIM = 128
GD_QK_HEADS = D_MODEL // GD_HEAD_DIM
GD_V_HEADS = 2 * GD_QK_HEADS
GD_CONV = 4
GD_CHUNK = 64
GD_QKV = 2 * GD_QK_HEADS * GD_HEAD_DIM + GD_V_HEADS * GD_HEAD_DIM
GD_IN = GD_QKV + GD_V_HEADS * GD_HEAD_DIM + 2 * GD_V_HEADS
D_FF = 2816
N_EXPERTS = 8
TOP_K = 2
D_FF_EXPERT = 3584
N_HGRN = (DEPTH + 3) // 4
N_SWA = (DEPTH + 2) // 4
N_GMLP = (DEPTH + 1) // 4
N_GDN = DEPTH // 4
N_DENSE = (DEPTH + 1) // 2
N_MOE = DEPTH // 2

kernel_name = 'hybrid_interleaved_hgrn2_swa_gmlp_gdn_moe'


def rms_norm(x, g):
    xf = x.astype(F32)
    xf = xf * lax.rsqrt(jnp.mean(xf * xf, axis=-1, keepdims=True) + EPS)
    return (xf * g.astype(F32)).astype(x.dtype)


def layer_norm(x, g, b):
    xf = x.astype(F32)
    mu = jnp.mean(xf, axis=-1, keepdims=True)
    var = jnp.mean(jnp.square(xf - mu), axis=-1, keepdims=True)
    return ((xf - mu) * lax.rsqrt(var + EPS) * g.astype(F32) + b.astype(F32)).astype(x.dtype)


def l2_norm(x):
    return x * lax.rsqrt(jnp.sum(x * x, axis=-1, keepdims=True) + EPS)


def rope(x, positions):
    B, T = positions.shape
    hd = x.shape[-1]
    inv_freq = ROPE_THETA ** (-jnp.arange(0, hd, 2, dtype=F32) / hd)
    ang = positions.astype(F32)[..., None] * inv_freq
    bshape = (B, T) + (1,) * (x.ndim - 3) + (hd // 2,)
    cos, sin = jnp.cos(ang).reshape(bshape), jnp.sin(ang).reshape(bshape)
    xf = x.astype(F32)
    x1, x2 = xf[..., : hd // 2], xf[..., hd // 2:]
    return jnp.concatenate([x1 * cos - x2 * sin, x2 * cos + x1 * sin], axis=-1).astype(x.dtype)


def chunk_gla(q, k, v, log_f, chunk):
    B, T, H, dk = q.shape
    dv = v.shape[-1]
    n = T // chunk
    rs = lambda a: a.reshape(B, n, chunk, H, a.shape[-1])
    q, k, v, log_f = rs(q), rs(k), rs(v), rs(log_f)
    b = jnp.cumsum(log_f, axis=2)
    b_ref = b[:, :, chunk // 2 - 1: chunk // 2]
    b_last = b[:, :, -1:]
    causal = jnp.tril(jnp.ones((chunk, chunk), bool))
    scores = jnp.einsum('bnchd,bnshd->bnhcs', q * jnp.exp(b - b_ref), k * jnp.exp(b_ref - b))
    scores = jnp.where(causal, scores, 0.0)
    o_intra = jnp.einsum('bnhcs,bnshv->bnchv', scores, v)
    q_in = q * jnp.exp(b)
    k_end = k * jnp.exp(b_last - b)
    d_end = jnp.exp(b_last[:, :, 0])

    def step(S, xs):
        q_c, k_c, v_c, d_c = xs
        o = jnp.einsum('bchd,bhdv->bchv', q_c, S)
        S = S * d_c[..., None] + jnp.einsum('bchd,bchv->bhdv', k_c, v_c)
        return S, o

    sw = lambda a: jnp.moveaxis(a, 1, 0)
    S0 = jnp.zeros((B, H, dk, dv), q.dtype)
    _, o_inter = lax.scan(step, S0, (sw(q_in), sw(k_end), sw(v), sw(d_end)))
    return (o_intra + sw(o_inter)).reshape(B, T, H, dv)


def hgrn2_mixer(h, w_in, o_norm, w_out, lower_bound):
    B, T, _ = h.shape
    H, dk = HG_HEADS, HG_HEAD_DIM
    q, f, i, g = jnp.split(h @ w_in, 4, axis=-1)
    forget = lower_bound + (1.0 - lower_bound) * jax.nn.sigmoid(f.astype(F32))
    heads = lambda a: a.reshape(B, T, H, dk)
    o = chunk_gla(heads(jax.nn.silu(q.astype(F32))), heads(1.0 - forget),
                  heads(i.astype(F32)), heads(jnp.log(forget)), HG_CHUNK)
    o = rms_norm(o, o_norm.reshape(H, dk)) * jax.nn.silu(heads(g.astype(F32)))
    return o.reshape(B, T, H * dk).astype(h.dtype) @ w_out


def swa_mixer(h, positions, w_in, q_norm, k_norm, sinks, w_out):
    B, T, _ = h.shape
    hd, KV, W = SW_HEAD_DIM, SW_KV_HEADS, SW_WINDOW
    G = SW_HEADS // KV
    nb = T // W
    proj = h @ w_in
    q = proj[..., : SW_HEADS * hd].reshape(B, T, KV, G, hd)
    k = proj[..., SW_HEADS * hd: (SW_HEADS + KV) * hd].reshape(B, T, KV, hd)
    v = proj[..., (SW_HEADS + KV) * hd:].reshape(B, T, KV, hd)
    q = rope(rms_norm(q, q_norm), positions)
    k = rope(rms_norm(k, k_norm), positions)
    qb = q.reshape(B, nb, W, KV, G, hd)
    kb = k.reshape(B, nb, W, KV, hd)
    vb = v.reshape(B, nb, W, KV, hd)
    pad = ((0, 0), (1, 0), (0, 0), (0, 0), (0, 0))
    kk = jnp.concatenate([jnp.pad(kb[:, :-1], pad), kb], axis=2)
    vv = jnp.concatenate([jnp.pad(vb[:, :-1], pad), vb], axis=2)
    c_idx = jnp.arange(W)[:, None]
    r_idx = jnp.arange(2 * W)[None, :]
    rel = c_idx + W - r_idx
    band = (rel >= 0) & (rel < W)
    sink_logit = sinks.astype(F32).reshape(KV, G)[None, :, :, None, None]
    scale = hd ** -0.5

    def block(args):
        q_j, k_j, v_j, j = args
        s = jnp.einsum('bckgd,brkd->bkgcr', q_j, k_j).astype(F32) * scale
        valid = band & ((j - 1) * W + r_idx >= 0)
        s = jnp.where(valid, s, -jnp.inf)
        sink = jnp.broadcast_to(sink_logit, s.shape[:-1] + (1,))
        p = jax.nn.softmax(jnp.concatenate([s, sink], axis=-1), axis=-1)[..., :-1]
        return jnp.einsum('bkgcr,brkd->bckgd', p.astype(v_j.dtype), v_j)

    sw = lambda a: jnp.moveaxis(a, 1, 0)
    o = lax.map(block, (sw(qb), sw(kk), sw(vv), jnp.arange(nb)))
    return jnp.moveaxis(o, 0, 1).reshape(B, T, SW_HEADS * hd) @ w_out


def gmlp_mixer(h, w_in, b_in, v_ln_g, v_ln_b, w_s, b_s, w_out):
    B, T, _ = h.shape
    n = T // GM_CHUNK
    C = GM_WIDTH // GM_GROUPS
    z = jax.nn.gelu(h @ w_in + b_in, approximate=False)
    u, v = jnp.split(z, 2, axis=-1)
    v = layer_norm(v, v_ln_g, v_ln_b)
    vb = v.reshape(B, n, GM_CHUNK, GM_GROUPS, C)
    w_causal = jnp.where(jnp.tril(jnp.ones((GM_CHUNK, GM_CHUNK), bool)), w_s, 0.0)
    mixed = jnp.einsum('gts,bnsgc->bntgc', w_causal, vb) + b_s.T[None, None, :, :, None]
    return (u * mixed.reshape(B, T, GM_WIDTH)) @ w_out


def causal_conv(x, w):
    K, C = w.shape
    return lax.conv_general_dilated(x, w[:, None, :], window_strides=(1,), padding=[(K - 1, 0)],
                                    dimension_numbers=('NWC', 'WIO', 'NWC'), feature_group_count=C)


def chunk_gated_delta(q, k, v, beta, g, chunk):
    B, T, H, dk = q.shape
    dv = v.shape[-1]
    n = T // chunk
    rs = lambda a: a.reshape(B, n, chunk, H, a.shape[-1]).transpose(0, 1, 3, 2, 4)
    q, k, v = rs(q), rs(k), rs(v)
    rs_s = lambda a: a.reshape(B, n, chunk, H).transpose(0, 1, 3, 2)
    beta, g = rs_s(beta), rs_s(g)
    gam = jnp.cumsum(g, axis=-1)
    tri = jnp.tril(jnp.ones((chunk, chunk), bool))
    tri_strict = jnp.tril(jnp.ones((chunk, chunk), bool), -1)
    decay = jnp.exp(jnp.where(tri, gam[..., :, None] - gam[..., None, :], -jnp.inf))
    kk = jnp.einsum('bnhcd,bnhsd->bnhcs', k, k)
    A = jnp.where(tri_strict, kk * decay * beta[..., :, None], 0.0)
    M = A + jnp.eye(chunk, dtype=A.dtype)
    rhs = jnp.concatenate([v * beta[..., None], k * (beta * jnp.exp(gam))[..., None]], axis=-1)
    sol = lax.linalg.triangular_solve(M, rhs, left_side=True, lower=True, unit_diagonal=True)
    u_val, w_k = sol[..., :dv], sol[..., dv:]
    qk = jnp.einsum('bnhcd,bnhsd->bnhcs', q, k) * decay
    q_dec = q * jnp.exp(gam)[..., None]
    k_end = k * jnp.exp(gam[..., -1:] - gam)[..., None]
    d_end = jnp.exp(gam[..., -1])

    def step(S, xs):
        qk_c, u_c, w_c, qd_c, ke_c, de_c = xs
        new_v = u_c - jnp.einsum('bhcd,bhdv->bhcv', w_c, S)
        o = jnp.einsum('bhcd,bhdv->bhcv', qd_c, S) + jnp.einsum('bhcs,bhsv->bhcv', qk_c, new_v)
        S = S * de_c[..., None, None] + jnp.einsum('bhcd,bhcv->bhdv', ke_c, new_v)
        return S, o

    sw = lambda a: jnp.moveaxis(a, 1, 0)
    S0 = jnp.zeros((B, H, dk, dv), q.dtype)
    _, o = lax.scan(step, S0, (sw(qk), sw(u_val), sw(w_k), sw(q_dec), sw(k_end), sw(d_end)))
    return o.transpose(1, 0, 3, 2, 4).reshape(B, T, H, dv)


def gdn_mixer(h, w_in, conv_w, a_log, dt_bias, o_norm, w_out):
    B, T, _ = h.shape
    Hk, Hv, dh = GD_QK_HEADS, GD_V_HEADS, GD_HEAD_DIM
    qkv, z, beta_raw, a_raw = jnp.split(h @ w_in, [GD_QKV, GD_QKV + Hv * dh, GD_QKV + Hv * dh + Hv], axis=-1)
    qkv = jax.nn.silu(causal_conv(qkv.astype(F32), conv_w.astype(F32)))
    q, k, v = jnp.split(qkv, [Hk * dh, 2 * Hk * dh], axis=-1)
    q = l2_norm(q.reshape(B, T, Hk, dh)) * (dh ** -0.5)
    k = l2_norm(k.reshape(B, T, Hk, dh))
    v = v.reshape(B, T, Hv, dh)
    rep = Hv // Hk
    q, k = jnp.repeat(q, rep, axis=2), jnp.repeat(k, rep, axis=2)
    beta = jax.nn.sigmoid(beta_raw.astype(F32))
    g = -jnp.exp(a_log.astype(F32)) * jax.nn.softplus(a_raw.astype(F32) + dt_bias.astype(F32))
    o = chunk_gated_delta(q, k, v, beta, g, GD_CHUNK)
    o = rms_norm(o, o_norm) * jax.nn.silu(z.astype(F32).reshape(B, T, Hv, dh))
    return o.reshape(B, T, Hv * dh).astype(h.dtype) @ w_out


def swiglu(h, w_gate, w_up, w_down):
    return (jax.nn.silu(h @ w_gate) * (h @ w_up)) @ w_down


def moe_ffn(h, router, w_gate, w_up, w_down):
    logits = (h @ router).astype(F32)
    top_val, top_idx = lax.top_k(logits, TOP_K)
    top_w = jax.nn.softmax(top_val, axis=-1)
    gates = jnp.sum(jax.nn.one_hot(top_idx, N_EXPERTS, dtype=F32) * top_w[..., None], axis=-2)
    out = jnp.zeros_like(h)
    for e in range(N_EXPERTS):
        out = out + gates[..., e:e + 1].astype(h.dtype) * swiglu(h, w_gate[e], w_up[e], w_down[e])
    return out


def setup_inputs(seed: int = 0) -> dict:
    key = jax.random.key(seed)
    keys = jax.random.split(key, 40)
    D = D_MODEL

    def nrm(i, shape, scale):
        return jax.random.normal(keys[i], shape, F32) * scale

    def gain(i, shape):
        return 1.0 + 0.02 * jax.random.normal(keys[i], shape, F32)

    x = nrm(0, (BATCH, SEQ, D), 1.0)
    positions = (jax.random.randint(keys[1], (BATCH, 1), 0, 4096, dtype=jnp.int32)
                 + jnp.arange(SEQ, dtype=jnp.int32)[None, :])
    dt = jnp.exp(jax.random.uniform(keys[26], (N_GDN, GD_V_HEADS), F32, np.log(1e-3), np.log(1e-1)))
    return {
        'x': x,
        'positions': positions,
        'mix_norm': gain(2, (DEPTH, D)),
        'ffn_norm': gain(3, (DEPTH, D)),
        'hgrn_lb_logits': nrm(4, (DEPTH, HG_HEADS * HG_HEAD_DIM), 0.1),
        'hgrn_w_in': nrm(5, (N_HGRN, D, 4 * HG_HEADS * HG_HEAD_DIM), D ** -0.5),
        'hgrn_o_norm': gain(6, (N_HGRN, HG_HEADS * HG_HEAD_DIM)),
        'hgrn_w_out': nrm(7, (N_HGRN, HG_HEADS * HG_HEAD_DIM, D), (HG_HEADS * HG_HEAD_DIM) ** -0.5),
        'swa_w_in': nrm(8, (N_SWA, D, (SW_HEADS + 2 * SW_KV_HEADS) * SW_HEAD_DIM), D ** -0.5),
        'swa_q_norm': gain(9, (N_SWA, SW_HEAD_DIM)),
        'swa_k_norm': gain(10, (N_SWA, SW_HEAD_DIM)),
        'swa_sinks': nrm(11, (N_SWA, SW_HEADS), 1.0),
        'swa_w_out': nrm(12, (N_SWA, SW_HEADS * SW_HEAD_DIM, D), (SW_HEADS * SW_HEAD_DIM) ** -0.5),
        'gmlp_w_in': nrm(13, (N_GMLP, D, 2 * GM_WIDTH), D ** -0.5),
        'gmlp_b_in': nrm(14, (N_GMLP, 2 * GM_WIDTH), 0.02),
        'gmlp_v_ln_g': gain(15, (N_GMLP, GM_WIDTH)),
        'gmlp_v_ln_b': nrm(16, (N_GMLP, GM_WIDTH), 0.02),
        'gmlp_w_s': nrm(17, (N_GMLP, GM_GROUPS, GM_CHUNK, GM_CHUNK), GM_CHUNK ** -0.5),
        'gmlp_b_s': gain(18, (N_GMLP, GM_GROUPS, GM_CHUNK)),
        'gmlp_w_out': nrm(19, (N_GMLP, GM_WIDTH, D), GM_WIDTH ** -0.5),
        'gdn_w_in': nrm(20, (N_GDN, D, GD_IN), D ** -0.5),
        'gdn_conv_w': nrm(21, (N_GDN, GD_CONV, GD_QKV), GD_CONV ** -0.5),
        'gdn_a_log': jnp.log(jax.random.uniform(keys[22], (N_GDN, GD_V_HEADS), F32, 1.0, 16.0)),
        'gdn_dt_bias': dt + jnp.log(-jnp.expm1(-dt)),
        'gdn_o_norm': gain(23, (N_GDN, GD_HEAD_DIM)),
        'gdn_w_out': nrm(24, (N_GDN, GD_V_HEADS * GD_HEAD_DIM, D), (GD_V_HEADS * GD_HEAD_DIM) ** -0.5),
        'dense_w_gate': nrm(27, (N_DENSE, D, D_FF), D ** -0.5),
        'dense_w_up': nrm(28, (N_DENSE, D, D_FF), D ** -0.5),
        'dense_w_down': nrm(29, (N_DENSE, D_FF, D), D_FF ** -0.5),
        'moe_router': nrm(30, (N_MOE, D, N_EXPERTS), D ** -0.5),
        'moe_w_gate': nrm(31, (N_MOE, N_EXPERTS, D, D_FF_EXPERT), D ** -0.5),
        'moe_w_up': nrm(32, (N_MOE, N_EXPERTS, D, D_FF_EXPERT), D ** -0.5),
        'moe_w_down': nrm(33, (N_MOE, N_EXPERTS, D_FF_EXPERT, D), D_FF_EXPERT ** -0.5),
    }


def reference(x, positions, mix_norm, ffn_norm, hgrn_lb_logits, hgrn_w_in, hgrn_o_norm, hgrn_w_out,
              swa_w_in, swa_q_norm, swa_k_norm, swa_sinks, swa_w_out,
              gmlp_w_in, gmlp_b_in, gmlp_v_ln_g, gmlp_v_ln_b, gmlp_w_s, gmlp_b_s, gmlp_w_out,
              gdn_w_in, gdn_conv_w, gdn_a_log, gdn_dt_bias, gdn_o_norm, gdn_w_out,
              dense_w_gate, dense_w_up, dense_w_down,
              moe_router, moe_w_gate, moe_w_up, moe_w_down):
    lower_bounds = jnp.cumsum(jax.nn.softmax(hgrn_lb_logits.astype(F32), axis=0), axis=0)
    for i in range(DEPTH):
        h = rms_norm(x, mix_norm[i])
        kind, j = i % N_MIXERS, i // N_MIXERS
        if kind == 0:
            m = hgrn2_mixer(h, hgrn_w_in[j], hgrn_o_norm[j], hgrn_w_out[j], lower_bounds[i])
        elif kind == 1:
            m = swa_mixer(h, positions, swa_w_in[j], swa_q_norm[j], swa_k_norm[j], swa_sinks[j], swa_w_out[j])
        elif kind == 2:
            m = gmlp_mixer(h, gmlp_w_in[j], gmlp_b_in[j], gmlp_v_ln_g[j], gmlp_v_ln_b[j],
                           gmlp_w_s[j], gmlp_b_s[j], gmlp_w_out[j])
        else:
            m = gdn_mixer(h, gdn_w_in[j], gdn_conv_w[j], gdn_a_log[j], gdn_dt_bias[j], gdn_o_norm[j], gdn_w_out[j])
        x = x + m
        h = rms_norm(x, ffn_norm[i])
        if i % 2 == 0:
            f = swiglu(h, dense_w_gate[i // 2], dense_w_up[i // 2], dense_w_down[i // 2])
        else:
            f = moe_ffn(h, moe_router[i // 2], moe_w_gate[i // 2], moe_w_up[i // 2], moe_w_down[i // 2])
        x = x + f
    return x
```

```python
import functools

import jax
import jax.numpy as jnp
from jax import lax
from jax.experimental import pallas as pl
from jax.experimental.pallas import tpu as pltpu

F32 = jnp.float32
BF16 = jnp.bfloat16
EPS = 1e-6

D_MODEL = 1024
LANES = 128
HG_HEAD_DIM = 128
HG_HEADS = D_MODEL // HG_HEAD_DIM
HG_CHUNK = 32
SW_HEAD_DIM = 64
SW_HEADS = D_MODEL // SW_HEAD_DIM
SW_KV_HEADS = 4
SW_GROUP = SW_HEADS // SW_KV_HEADS
SW_WINDOW = 128
ROPE_THETA = 10000.0
GM_WIDTH = D_MODEL
GM_GROUPS = 8
GM_CHUNK = 128
GD_HEAD_DIM = 128
GD_QK_HEADS = D_MODEL // GD_HEAD_DIM
GD_V_HEADS = 2 * GD_QK_HEADS
GD_CONV = 4
GD_CHUNK = 64
GD_QKV = 2 * GD_QK_HEADS * GD_HEAD_DIM + GD_V_HEADS * GD_HEAD_DIM
GD_Z = GD_V_HEADS * GD_HEAD_DIM
N_EXPERTS = 8
TOP_K = 2

V7X_VMEM_LIMIT_BYTES = 56 * 1024 * 1024

NT_DIMS = (((1,), (1,)), ((), ()))
TN_DIMS = (((0,), (0,)), ((), ()))


def _params(*sem):
    return pltpu.CompilerParams(dimension_semantics=sem, vmem_limit_bytes=V7X_VMEM_LIMIT_BYTES)


def _dot(a, b):
    return jnp.dot(a, b, preferred_element_type=F32)


def _dot_split(a_bf, x):
    hi = x.astype(BF16)
    lo = (x - hi.astype(F32)).astype(BF16)
    return _dot(a_bf, hi) + _dot(a_bf, lo)


def _rms(x, g):
    return x * lax.rsqrt(jnp.mean(x * x, axis=-1, keepdims=True) + EPS) * g


def _silu(x):
    return x * jax.nn.sigmoid(x)


def _row_tile(n, want):
    t = min(n, want)
    assert n % t == 0, (n, t)
    return t


def _norm_mm_body(x_ref, g_ref, w_ref, *rest, widths, chunk, use_bias, act):
    if use_bias:
        b_ref, out_refs = rest[0], rest[1:]
    else:
        b_ref, out_refs = None, rest
    h = _rms(x_ref[...], g_ref[...]).astype(BF16)
    off = 0
    for o_ref, width in zip(out_refs, widths):
        for c in range(0, width, chunk):
            cw = min(chunk, width - c)
            y = _dot(h, w_ref[:, off + c:off + c + cw])
            if use_bias:
                y = y + b_ref[:, off + c:off + c + cw]
            if act == "gelu":
                y = 0.5 * y * (1.0 + lax.erf(y * (0.5 ** 0.5)))
            o_ref[:, c:c + cw] = y.astype(o_ref.dtype)
        off += width


def norm_matmul(x, g, w, outs, bias=None, act=None, tm=512, chunk=512):
    n, d = x.shape
    m = w.shape[1]
    widths = tuple(o[0] for o in outs)
    assert sum(widths) == m
    tm = _row_tile(n, tm)
    in_specs = [pl.BlockSpec((tm, d), lambda i: (i, 0)),
                pl.BlockSpec((1, d), lambda i: (0, 0)),
                pl.BlockSpec((d, m), lambda i: (0, 0))]
    args = [x, g.reshape(1, d).astype(F32), w]
    if bias is not None:
        in_specs.append(pl.BlockSpec((1, m), lambda i: (0, 0)))
        args.append(bias.reshape(1, m).astype(F32))
    body = functools.partial(_norm_mm_body, widths=widths, chunk=chunk,
                             use_bias=bias is not None, act=act)
    return pl.pallas_call(
        body,
        out_shape=[jax.ShapeDtypeStruct((n, wd), dt) for wd, dt in outs],
        grid=(n // tm,),
        in_specs=in_specs,
        out_specs=[pl.BlockSpec((tm, wd), lambda i: (i, 0)) for wd, _ in outs],
        compiler_params=_params("parallel"),
    )(*args)


def _mm_res_body(a_ref, w_ref, x_ref, o_ref):
    o_ref[...] = x_ref[...] + _dot(a_ref[...], w_ref[...])


def matmul_residual(a, w, x, tm=512):
    n, k = a.shape
    d = w.shape[1]
    tm = _row_tile(n, tm)
    return pl.pallas_call(
        _mm_res_body,
        out_shape=jax.ShapeDtypeStruct((n, d), F32),
        grid=(n // tm,),
        in_specs=[pl.BlockSpec((tm, k), lambda i: (i, 0)),
                  pl.BlockSpec((k, d), lambda i: (0, 0)),
                  pl.BlockSpec((tm, d), lambda i: (i, 0))],
        out_specs=pl.BlockSpec((tm, d), lambda i: (i, 0)),
        compiler_params=_params("parallel"),
    )(a, w, x)


def _swiglu_up_body(x_ref, g_ref, wg_ref, wu_ref, o_ref, *, chunk):
    h = _rms(x_ref[...], g_ref[...]).astype(BF16)
    f = o_ref.shape[1]
    for c in range(0, f, chunk):
        a = _dot(h, wg_ref[:, c:c + chunk])
        b = _dot(h, wu_ref[:, c:c + chunk])
        o_ref[:, c:c + chunk] = (_silu(a) * b).astype(o_ref.dtype)


def swiglu_up(x, g, wg, wu, tm=512, chunk=256):
    n, d = x.shape
    f = wg.shape[1]
    assert f % chunk == 0
    tm = _row_tile(n, tm)
    return pl.pallas_call(
        functools.partial(_swiglu_up_body, chunk=chunk),
        out_shape=jax.ShapeDtypeStruct((n, f), BF16),
        grid=(n // tm,),
        in_specs=[pl.BlockSpec((tm, d), lambda i: (i, 0)),
                  pl.BlockSpec((1, d), lambda i: (0, 0)),
                  pl.BlockSpec((d, f), lambda i: (0, 0)),
                  pl.BlockSpec((d, f), lambda i: (0, 0))],
        out_specs=pl.BlockSpec((tm, f), lambda i: (i, 0)),
        compiler_params=_params("parallel"),
    )(x, g.reshape(1, d).astype(F32), wg, wu)


def _hgrn_body(q_ref, f_ref, i_ref, g_ref, lb_ref, on_ref, o_ref, s_ref, *, tb):
    c32 = HG_CHUNK
    hd = HG_HEAD_DIM

    @pl.when(pl.program_id(1) == 0)
    def _():
        s_ref[...] = jnp.zeros_like(s_ref)

    lb = lb_ref[...]
    on = on_ref[...]
    row = lax.broadcasted_iota(jnp.int32, (c32, c32), 0)
    col = lax.broadcasted_iota(jnp.int32, (c32, c32), 1)
    causal = row >= col
    tri_bf = causal.astype(BF16)

    def chunk(c, carry):
        r0 = pl.multiple_of(c * c32, c32)
        rows = pl.ds(r0, c32)
        forget = lb + (1.0 - lb) * jax.nn.sigmoid(f_ref[rows, :].astype(F32))
        b = _dot_split(tri_bf, jnp.log(forget))
        b_mid = b[c32 // 2 - 1:c32 // 2, :]
        b_last = b[c32 - 1:c32, :]
        qs = _silu(q_ref[rows, :].astype(F32))
        k = 1.0 - forget
        v_bf = i_ref[rows, :]
        qa = (qs * jnp.exp(b - b_mid)).astype(BF16)
        ka = (k * jnp.exp(b_mid - b)).astype(BF16)
        q_in = (qs * jnp.exp(b)).astype(BF16)
        k_end = (k * jnp.exp(b_last - b)).astype(BF16)
        d_end = jnp.exp(b_last)
        d_cols = jnp.concatenate([d_end[:, h * hd:(h + 1) * hd] for h in range(HG_HEADS)], axis=0).T
        gate = _silu(g_ref[rows, :].astype(F32))
        outs = []
        for h in range(HG_HEADS):
            sl = slice(h * hd, (h + 1) * hd)
            s = s_ref[h]
            sc = lax.dot_general(qa[:, sl], ka[:, sl], NT_DIMS, preferred_element_type=F32)
            sc = jnp.where(causal, sc, 0.0).astype(BF16)
            o = _dot(sc, v_bf[:, sl]) + _dot(q_in[:, sl], s.astype(BF16))
            s_ref[h] = s * d_cols[:, h:h + 1] + lax.dot_general(
                k_end[:, sl], v_bf[:, sl], TN_DIMS, preferred_element_type=F32)
            outs.append(_rms(o, on[:, sl]))
        o_ref[rows, :] = (jnp.concatenate(outs, axis=-1) * gate).astype(o_ref.dtype)
        return carry

    lax.fori_loop(0, tb // c32, chunk, 0)


def hgrn_recurrence(qfig, lb, o_norm, batch, seq, tb=256):
    n = batch * seq
    d = D_MODEL
    tb = _row_tile(seq, tb)
    nt = seq // tb
    col_spec = lambda j: pl.BlockSpec((tb, d), lambda b, t, j=j: (b * nt + t, j))
    return pl.pallas_call(
        functools.partial(_hgrn_body, tb=tb),
        out_shape=jax.ShapeDtypeStruct((n, d), BF16),
        grid=(batch, nt),
        in_specs=[col_spec(0), col_spec(1), col_spec(2), col_spec(3),
                  pl.BlockSpec((1, d), lambda b, t: (0, 0)),
                  pl.BlockSpec((1, d), lambda b, t: (0, 0))],
        out_specs=pl.BlockSpec((tb, d), lambda b, t: (b * nt + t, 0)),
        scratch_shapes=[pltpu.VMEM((HG_HEADS, HG_HEAD_DIM, HG_HEAD_DIM), F32)],
        compiler_params=_params("parallel", "arbitrary"),
    )(qfig, qfig, qfig, qfig, lb.reshape(1, d).astype(F32), o_norm.reshape(1, d).astype(F32))


def _swa_body(sink_ref, q_ref, kv_ref, pos_ref, qn_ref, kn_ref, invf_ref, sgn_ref, o_ref,
              kprev_ref, vprev_ref):
    w = SW_WINDOW
    hd = SW_HEAD_DIM
    j = pl.program_id(1)

    @pl.when(j == 0)
    def _():
        kprev_ref[...] = jnp.zeros_like(kprev_ref)
        vprev_ref[...] = jnp.zeros_like(vprev_ref)

    lane = lax.broadcasted_iota(jnp.int32, (w, LANES), 1)
    first_half = (lane % hd) < (hd // 2)
    low_head = lane < hd
    ang = pos_ref[...] * invf_ref[...]
    cos = jnp.cos(ang)
    sin_signed = jnp.sin(ang) * sgn_ref[...]
    gi = lax.broadcasted_iota(jnp.int32, (LANES, LANES), 0) // hd
    gj = lax.broadcasted_iota(jnp.int32, (LANES, LANES), 1) // hd
    head_mean = jnp.where(gi == gj, 1.0 / hd, 0.0).astype(BF16)

    def norm_rope(x, gain):
        xx = x * x
        hi = xx.astype(BF16)
        lo = (xx - hi.astype(F32)).astype(BF16)
        ms = _dot(hi, head_mean) + _dot(lo, head_mean)
        x = x * lax.rsqrt(ms + EPS) * gain
        partner = jnp.where(first_half, pltpu.roll(x, LANES - hd // 2, 1), pltpu.roll(x, hd // 2, 1))
        return x * cos + partner * sin_signed

    qn = qn_ref[...]
    kn = kn_ref[...]
    kv = kv_ref[...].astype(F32)
    n_kv_tiles = SW_KV_HEADS * hd // LANES
    k_cur = [norm_rope(kv[:, t * LANES:(t + 1) * LANES], kn) for t in range(n_kv_tiles)]
    v_cur = [kv[:, (n_kv_tiles + t) * LANES:(n_kv_tiles + t + 1) * LANES] for t in range(n_kv_tiles)]
    k_all = [jnp.concatenate([kprev_ref[t], k_cur[t]], axis=0) for t in range(n_kv_tiles)]
    v_all = [jnp.concatenate([vprev_ref[t], v_cur[t]], axis=0) for t in range(n_kv_tiles)]
    for t in range(n_kv_tiles):
        kprev_ref[t] = k_cur[t]
        vprev_ref[t] = v_cur[t]

    lane2 = lax.broadcasted_iota(jnp.int32, (2 * w, LANES), 1)
    low2 = lane2 < hd
    c_idx = lax.broadcasted_iota(jnp.int32, (w, 2 * w), 0)
    r_idx = lax.broadcasted_iota(jnp.int32, (w, 2 * w), 1)
    rel = c_idx + w - r_idx
    valid = (rel >= 0) & (rel < w) & ((j - 1) * w + r_idx >= 0)
    scale = hd ** -0.5

    for kvh in range(SW_KV_HEADS):
        t = kvh // 2
        kt, vt = k_all[t], v_all[t]
        kt_sw = pltpu.roll(kt, hd, 1)
        vt_sw = pltpu.roll(vt, hd, 1)
        if kvh % 2 == 0:
            k_lo_src, k_hi_src, v_lo_src, v_hi_src = kt, kt_sw, vt, vt_sw
        else:
            k_lo_src, k_hi_src, v_lo_src, v_hi_src = kt_sw, kt, vt_sw, vt
        k_lo = jnp.where(low2, k_lo_src, 0.0).astype(BF16)
        k_hi = jnp.where(low2, 0.0, k_hi_src).astype(BF16)
        v_lo = jnp.where(low2, v_lo_src, 0.0).astype(BF16)
        v_hi = jnp.where(low2, 0.0, v_hi_src).astype(BF16)
        for pair in range(SW_GROUP // 2):
            col0 = (kvh * SW_GROUP + 2 * pair) * hd
            qp = (norm_rope(q_ref[:, col0:col0 + LANES].astype(F32), qn) * scale).astype(BF16)
            acc = None
            for half, (k_m, v_m) in enumerate(((k_lo, v_lo), (k_hi, v_hi))):
                sink = sink_ref[kvh * SW_GROUP + 2 * pair + half]
                s = lax.dot_general(qp, k_m, NT_DIMS, preferred_element_type=F32)
                s = jnp.where(valid, s, -jnp.inf)
                m = jnp.maximum(jnp.max(s, axis=-1, keepdims=True), sink)
                p = jnp.exp(s - m)
                denom = jnp.sum(p, axis=-1, keepdims=True) + jnp.exp(sink - m)
                pv = _dot(p.astype(BF16), v_m) / denom
                acc = pv if acc is None else acc + pv
            o_ref[:, col0:col0 + LANES] = acc.astype(o_ref.dtype)


def swa_attention(q, kv, pos, q_norm, k_norm, sinks, batch, seq):
    n = batch * seq
    w = SW_WINDOW
    nb = seq // w
    hd = SW_HEAD_DIM
    lane = jnp.arange(LANES)
    inv_freq = ROPE_THETA ** (-jnp.arange(0, hd, 2, dtype=F32) / hd)
    invf = inv_freq[lane % (hd // 2)].reshape(1, LANES)
    sgn = jnp.where((lane % hd) < hd // 2, -1.0, 1.0).astype(F32).reshape(1, LANES)
    tile2 = lambda g: jnp.tile(g.astype(F32), LANES // hd).reshape(1, LANES)
    n_kv_tiles = SW_KV_HEADS * hd // LANES
    row = lambda b, jj, s: (b * nb + jj, 0)
    const = lambda b, jj, s: (0, 0)
    grid_spec = pltpu.PrefetchScalarGridSpec(
        num_scalar_prefetch=1,
        grid=(batch, nb),
        in_specs=[pl.BlockSpec((w, D_MODEL), row),
                  pl.BlockSpec((w, 2 * SW_KV_HEADS * hd), row),
                  pl.BlockSpec((w, 1), row),
                  pl.BlockSpec((1, LANES), const),
                  pl.BlockSpec((1, LANES), const),
                  pl.BlockSpec((1, LANES), const),
                  pl.BlockSpec((1, LANES), const)],
        out_specs=pl.BlockSpec((w, D_MODEL), row),
        scratch_shapes=[pltpu.VMEM((n_kv_tiles, w, LANES), F32),
                        pltpu.VMEM((n_kv_tiles, w, LANES), F32)],
    )
    return pl.pallas_call(
        _swa_body,
        out_shape=jax.ShapeDtypeStruct((n, D_MODEL), BF16),
        grid_spec=grid_spec,
        compiler_params=_params("parallel", "arbitrary"),
    )(sinks.astype(F32), q, kv, pos, tile2(q_norm), tile2(k_norm), invf, sgn)


def _gmlp_body(u_ref, v_ref, lg_ref, lbias_ref, ws_ref, bs_ref, o_ref, *, tb):
    c = GM_CHUNK
    gw = GM_WIDTH // GM_GROUPS
    row = lax.broadcasted_iota(jnp.int32, (c, c), 0)
    col = lax.broadcasted_iota(jnp.int32, (c, c), 1)
    causal = row >= col
    wcs = [jnp.where(causal, ws_ref[g], 0.0).astype(BF16) for g in range(GM_GROUPS)]
    bs = bs_ref[...]
    for r in range(0, tb, c):
        v = v_ref[r:r + c, :].astype(F32)
        mu = jnp.mean(v, axis=-1, keepdims=True)
        vc = v - mu
        var = jnp.mean(vc * vc, axis=-1, keepdims=True)
        vn = (vc * lax.rsqrt(var + EPS) * lg_ref[...] + lbias_ref[...]).astype(BF16)
        for g in range(GM_GROUPS):
            sl = slice(g * gw, (g + 1) * gw)
            mixed = _dot(wcs[g], vn[:, sl]) + bs[:, g:g + 1]
            o_ref[r:r + c, sl] = (u_ref[r:r + c, sl].astype(F32) * mixed).astype(o_ref.dtype)


def gmlp_spatial(z, ln_g, ln_b, w_s, b_s, tb=512):
    n = z.shape[0]
    d = GM_WIDTH
    tb = _row_tile(n, tb)
    const2 = lambda i: (0, 0)
    return pl.pallas_call(
        functools.partial(_gmlp_body, tb=tb),
        out_shape=jax.ShapeDtypeStruct((n, d), BF16),
        grid=(n // tb,),
        in_specs=[pl.BlockSpec((tb, d), lambda i: (i, 0)),
                  pl.BlockSpec((tb, d), lambda i: (i, 1)),
                  pl.BlockSpec((1, d), const2),
                  pl.BlockSpec((1, d), const2),
                  pl.BlockSpec((GM_GROUPS, GM_CHUNK, GM_CHUNK), lambda i: (0, 0, 0)),
                  pl.BlockSpec((GM_CHUNK, GM_GROUPS), const2)],
        out_specs=pl.BlockSpec((tb, d), lambda i: (i, 0)),
        compiler_params=_params("parallel"),
    )(z, z, ln_g.reshape(1, d).astype(F32), ln_b.reshape(1, d).astype(F32),
      w_s.astype(F32), b_s.T.astype(F32))


def _gdn_body(qkv_ref, z_ref, ba_ref, cw_ref, alog_ref, dtb_ref, on_ref, o_ref,
              s_ref, x_ref, y_ref, *, tb):
    c = GD_CHUNK
    hd = GD_HEAD_DIM
    halo = 8
    nq = GD_QK_HEADS * hd
    it = pl.program_id(1)

    @pl.when(it == 0)
    def _():
        s_ref[...] = jnp.zeros_like(s_ref)
        x_ref[0:halo, :] = jnp.zeros((halo, GD_QKV), F32)

    @pl.when(it > 0)
    def _():
        x_ref[0:halo, :] = x_ref[tb:tb + halo, :]

    x_ref[halo:halo + tb, :] = qkv_ref[...].astype(F32)
    acc = None
    for jj in range(GD_CONV):
        start = halo - (GD_CONV - 1) + jj
        term = cw_ref[jj:jj + 1, :] * x_ref[start:start + tb, :]
        acc = term if acc is None else acc + term
    y_ref[...] = _silu(acc)
    for h in range(2 * GD_QK_HEADS):
        sl = slice(h * hd, (h + 1) * hd)
        a = y_ref[:, sl]
        a = a * lax.rsqrt(jnp.sum(a * a, axis=-1, keepdims=True) + EPS)
        if h < GD_QK_HEADS:
            a = a * (hd ** -0.5)
        y_ref[:, sl] = a

    row = lax.broadcasted_iota(jnp.int32, (c, c), 0)
    col = lax.broadcasted_iota(jnp.int32, (c, c), 1)
    tri = row >= col
    tri_strict = row > col
    tri_bf = tri.astype(BF16)
    lane = lax.broadcasted_iota(jnp.int32, (1, LANES), 1)
    neg_a = -jnp.exp(alog_ref[...])
    on = on_ref[...]

    def chunk(ci, carry):
        r0 = pl.multiple_of(ci * c, c)
        rows = pl.ds(r0, c)
        ba = ba_ref[rows, :]
        beta_all = jax.nn.sigmoid(ba)
        g_all = neg_a * jax.nn.softplus(ba + dtb_ref[...])
        g_all = jnp.where((lane >= GD_V_HEADS) & (lane < 2 * GD_V_HEADS), g_all, 0.0)
        gam_all = _dot_split(tri_bf, g_all)
        gam_rows = gam_all.T
        outs = []
        for hv in range(GD_V_HEADS):
            hk = hv // (GD_V_HEADS // GD_QK_HEADS)
            q = y_ref[rows, hk * hd:(hk + 1) * hd]
            k = y_ref[rows, nq + hk * hd:nq + (hk + 1) * hd]
            v = y_ref[rows, 2 * nq + hv * hd:2 * nq + (hv + 1) * hd]
            k_bf = k.astype(BF16)
            beta = beta_all[:, hv:hv + 1]
            gam = gam_all[:, GD_V_HEADS + hv:GD_V_HEADS + hv + 1]
            gam_row = gam_rows[GD_V_HEADS + hv:GD_V_HEADS + hv + 1, :]
            gam_last = gam[c - 1:c, :]
            decay = jnp.exp(jnp.where(tri, gam - gam_row, -jnp.inf))
            kk = lax.dot_general(k_bf, k_bf, NT_DIMS, preferred_element_type=F32)
            a_mat = jnp.where(tri_strict, kk * decay * beta, 0.0)
            e_gam = jnp.exp(gam)
            rhs = jnp.concatenate([v * beta, k * (beta * e_gam)], axis=-1)
            p = -a_mat
            sol = rhs
            n_steps = c.bit_length() - 1
            for step in range(n_steps):
                p_bf = p.astype(BF16)
                sol = sol + _dot(p_bf, sol.astype(BF16))
                if step + 1 < n_steps:
                    p = _dot(p_bf, p_bf)
            u_val = sol[:, :hd]
            w_k = sol[:, hd:]
            qk = lax.dot_general(q.astype(BF16), k_bf, NT_DIMS, preferred_element_type=F32) * decay
            s = s_ref[hv]
            s_bf = s.astype(BF16)
            new_v = u_val - _dot(w_k.astype(BF16), s_bf)
            nv_bf = new_v.astype(BF16)
            o = _dot((q * e_gam).astype(BF16), s_bf) + _dot(qk.astype(BF16), nv_bf)
            k_end = (k * jnp.exp(gam_last - gam)).astype(BF16)
            s_ref[hv] = s * jnp.exp(gam_last) + lax.dot_general(
                k_end, nv_bf, TN_DIMS, preferred_element_type=F32)
            zg = _silu(z_ref[rows, hv * hd:(hv + 1) * hd].astype(F32))
            outs.append(_rms(o, on) * zg)
        o_ref[rows, :] = jnp.concatenate(outs, axis=-1).astype(o_ref.dtype)
        return carry

    lax.fori_loop(0, tb // c, chunk, 0)


def gdn_recurrence(qkv, z, ba, conv_w, a_log, dt_bias, o_norm, batch, seq, tb=256):
    n = batch * seq
    tb = _row_tile(seq, tb)
    nt = seq // tb
    pad_heads = lambda a: jnp.zeros((1, LANES), F32).at[0, GD_V_HEADS:2 * GD_V_HEADS].set(a.astype(F32))
    row = lambda b, t: (b * nt + t, 0)
    const = lambda b, t: (0, 0)
    return pl.pallas_call(
        functools.partial(_gdn_body, tb=tb),
        out_shape=jax.ShapeDtypeStruct((n, GD_Z), BF16),
        grid=(batch, nt),
        in_specs=[pl.BlockSpec((tb, GD_QKV), row),
                  pl.BlockSpec((tb, GD_Z), row),
                  pl.BlockSpec((tb, LANES), row),
                  pl.BlockSpec((GD_CONV, GD_QKV), const),
                  pl.BlockSpec((1, LANES), const),
                  pl.BlockSpec((1, LANES), const),
                  pl.BlockSpec((1, GD_HEAD_DIM), const)],
        out_specs=pl.BlockSpec((tb, GD_Z), row),
        scratch_shapes=[pltpu.VMEM((GD_V_HEADS, GD_HEAD_DIM, GD_HEAD_DIM), F32),
                        pltpu.VMEM((tb + 16, GD_QKV), F32),
                        pltpu.VMEM((tb, GD_QKV), F32)],
        compiler_params=_params("parallel", "arbitrary"),
    )(qkv, z, ba, conv_w.astype(F32), pad_heads(a_log), pad_heads(dt_bias),
      o_norm.reshape(1, GD_HEAD_DIM).astype(F32))


def _router_body(x_ref, g_ref, wr_ref, o_ref):
    h = _rms(x_ref[...], g_ref[...])
    logits = jnp.dot(h, wr_ref[...], preferred_element_type=F32, precision=lax.Precision.HIGHEST)
    lane = lax.broadcasted_iota(jnp.int32, logits.shape, 1)
    logits = jnp.where(lane < N_EXPERTS, logits, -jnp.inf)
    m1 = jnp.max(logits, axis=-1, keepdims=True)
    i1 = jnp.min(jnp.where(logits == m1, lane, LANES), axis=-1, keepdims=True)
    rest = jnp.where(lane == i1, -jnp.inf, logits)
    m2 = jnp.max(rest, axis=-1, keepdims=True)
    i2 = jnp.min(jnp.where(rest == m2, lane, LANES), axis=-1, keepdims=True)
    e2 = jnp.exp(m2 - m1)
    w1 = 1.0 / (1.0 + e2)
    w2 = e2 / (1.0 + e2)
    out = jnp.where(lane == 0, i1.astype(F32),
                    jnp.where(lane == 1, i2.astype(F32),
                              jnp.where(lane == 2, w1, jnp.where(lane == 3, w2, 0.0))))
    o_ref[...] = out


def moe_router(x, g, router, tm=512):
    n, d = x.shape
    tm = _row_tile(n, tm)
    wr = jnp.zeros((d, LANES), F32).at[:, :N_EXPERTS].set(router.astype(F32))
    return pl.pallas_call(
        _router_body,
        out_shape=jax.ShapeDtypeStruct((n, LANES), F32),
        grid=(n // tm,),
        in_specs=[pl.BlockSpec((tm, d), lambda i: (i, 0)),
                  pl.BlockSpec((1, d), lambda i: (0, 0)),
                  pl.BlockSpec((d, LANES), lambda i: (0, 0))],
        out_specs=pl.BlockSpec((tm, LANES), lambda i: (i, 0)),
        compiler_params=_params("parallel"),
    )(x, g.reshape(1, d).astype(F32), wr)


def _row_copy(src_ref, src_row, dst_ref, dst_row, sem):
    return pltpu.make_async_copy(src_ref.at[pl.ds(src_row, 1), :], dst_ref.at[pl.ds(dst_row, 1), :], sem)


def _dispatch_body(pos_ref, x_ref, xs_in_ref, xs_ref, sem, *, rt):
    del xs_in_ref

    def start(a, carry):
        _row_copy(x_ref, a // TOP_K, xs_ref, pos_ref[0, 0, a], sem).start()
        return carry

    lax.fori_loop(0, TOP_K * rt, start, 0)

    def wait(a, carry):
        _row_copy(x_ref, 0, xs_ref, 0, sem).wait()
        return carry

    lax.fori_loop(0, TOP_K * rt, wait, 0)


def moe_dispatch(x, pos, n_rows, rt=256):
    n, d = x.shape
    rt = _row_tile(n, rt)
    grid_spec = pltpu.PrefetchScalarGridSpec(
        num_scalar_prefetch=0,
        grid=(n // rt,),
        in_specs=[pl.BlockSpec((1, 1, TOP_K * rt), lambda i: (i, 0, 0), memory_space=pltpu.SMEM),
                  pl.BlockSpec((rt, d), lambda i: (i, 0)),
                  pl.BlockSpec(memory_space=pl.ANY)],
        out_specs=pl.BlockSpec(memory_space=pl.ANY),
        scratch_shapes=[pltpu.SemaphoreType.DMA(())],
    )
    return pl.pallas_call(
        functools.partial(_dispatch_body, rt=rt),
        out_shape=jax.ShapeDtypeStruct((n_rows, d), F32),
        grid_spec=grid_spec,
        input_output_aliases={2: 0},
        compiler_params=_params("arbitrary"),
    )(pos.reshape(n // rt, 1, TOP_K * rt), x, jnp.zeros((n_rows, d), F32))


def _moe_up_body(te_ref, nt_ref, xs_ref, g_ref, wg_ref, wu_ref, o_ref, *, chunk):
    del te_ref
    i = pl.program_id(1)

    @pl.when(i < nt_ref[0])
    def _():
        h = _rms(xs_ref[...], g_ref[...]).astype(BF16)
        f = o_ref.shape[1]
        for c in range(0, f, chunk):
            a = _dot(h, wg_ref[0, :, c:c + chunk])
            b = _dot(h, wu_ref[0, :, c:c + chunk])
            o_ref[:, c:c + chunk] = (_silu(a) * b).astype(o_ref.dtype)

    @pl.when(i >= nt_ref[0])
    def _():
        o_ref[...] = jnp.zeros_like(o_ref)


def moe_up(xs, g, wg, wu, tile_expert, n_tiles_used, tm, fsplit=2, chunk=256):
    p, d = xs.shape
    f = wg.shape[2]
    fb = f // fsplit
    assert fb % chunk == 0 and p % tm == 0
    grid_spec = pltpu.PrefetchScalarGridSpec(
        num_scalar_prefetch=2,
        grid=(fsplit, p // tm),
        in_specs=[pl.BlockSpec((tm, d), lambda j, i, te, nt: (i, 0)),
                  pl.BlockSpec((1, d), lambda j, i, te, nt: (0, 0)),
                  pl.BlockSpec((1, d, fb), lambda j, i, te, nt: (te[i], 0, j)),
                  pl.BlockSpec((1, d, fb), lambda j, i, te, nt: (te[i], 0, j))],
        out_specs=pl.BlockSpec((tm, fb), lambda j, i, te, nt: (i, j)),
    )
    return pl.pallas_call(
        functools.partial(_moe_up_body, chunk=chunk),
        out_shape=jax.ShapeDtypeStruct((p, f), BF16),
        grid_spec=grid_spec,
        compiler_params=_params("arbitrary", "arbitrary"),
    )(tile_expert, n_tiles_used, xs, g.reshape(1, d).astype(F32), wg, wu)


def _moe_down_body(te_ref, nt_ref, a_ref, wd_ref, o_ref):
    del te_ref
    i = pl.program_id(0)

    @pl.when(i < nt_ref[0])
    def _():
        o_ref[...] = _dot(a_ref[...], wd_ref[0])

    @pl.when(i >= nt_ref[0])
    def _():
        o_ref[...] = jnp.zeros_like(o_ref)


def moe_down(act, wd, tile_expert, n_tiles_used, tm):
    p, f = act.shape
    d = wd.shape[2]
    grid_spec = pltpu.PrefetchScalarGridSpec(
        num_scalar_prefetch=2,
        grid=(p // tm,),
        in_specs=[pl.BlockSpec((tm, f), lambda i, te, nt: (i, 0)),
                  pl.BlockSpec((1, f, d), lambda i, te, nt: (te[i], 0, 0))],
        out_specs=pl.BlockSpec((tm, d), lambda i, te, nt: (i, 0)),
    )
    return pl.pallas_call(
        _moe_down_body,
        out_shape=jax.ShapeDtypeStruct((p, d), F32),
        grid_spec=grid_spec,
        compiler_params=_params("arbitrary"),
    )(tile_expert, n_tiles_used, act, wd)


def _combine_body(pos_ref, x_ref, r_ref, ys_ref, o_ref, buf_ref, sem, *, rt):
    def start(a, carry):
        _row_copy(ys_ref, pos_ref[0, 0, a], buf_ref.at[a % TOP_K], a // TOP_K, sem).start()
        return carry

    lax.fori_loop(0, TOP_K * rt, start, 0)

    def wait(a, carry):
        _row_copy(ys_ref, 0, buf_ref.at[0], 0, sem).wait()
        return carry

    lax.fori_loop(0, TOP_K * rt, wait, 0)
    r = r_ref[...]
    o_ref[...] = x_ref[...] + r[:, 2:3] * buf_ref[0] + r[:, 3:4] * buf_ref[1]


def moe_combine(x, route, ys, pos, rt=256):
    n, d = x.shape
    rt = _row_tile(n, rt)
    grid_spec = pltpu.PrefetchScalarGridSpec(
        num_scalar_prefetch=0,
        grid=(n // rt,),
        in_specs=[pl.BlockSpec((1, 1, TOP_K * rt), lambda i: (i, 0, 0), memory_space=pltpu.SMEM),
                  pl.BlockSpec((rt, d), lambda i: (i, 0)),
                  pl.BlockSpec((rt, LANES), lambda i: (i, 0)),
                  pl.BlockSpec(memory_space=pl.ANY)],
        out_specs=pl.BlockSpec((rt, d), lambda i: (i, 0)),
        scratch_shapes=[pltpu.VMEM((TOP_K, rt, d), F32), pltpu.SemaphoreType.DMA(())],
    )
    return pl.pallas_call(
        functools.partial(_combine_body, rt=rt),
        out_shape=jax.ShapeDtypeStruct((n, d), F32),
        grid_spec=grid_spec,
        compiler_params=_params("arbitrary"),
    )(pos.reshape(n // rt, 1, TOP_K * rt), x, route, ys)


def moe_ffn(x, g, router, wg, wu, wd, tm=512):
    n, d = x.shape
    route = moe_router(x, g, router)
    expert = route[:, :TOP_K].astype(jnp.int32).reshape(n * TOP_K)
    onehot = (expert[:, None] == jnp.arange(N_EXPERTS, dtype=jnp.int32)[None, :]).astype(jnp.int32)
    csum = jnp.cumsum(onehot, axis=0)
    rank = jnp.sum((csum - onehot) * onehot, axis=1)
    counts = csum[-1]
    padded = ((counts + tm - 1) // tm) * tm
    ends = jnp.cumsum(padded)
    starts = ends - padded
    pos = (jnp.sum(starts[None, :] * onehot, axis=1) + rank).astype(jnp.int32)
    n_rows = n * TOP_K + N_EXPERTS * tm
    n_tiles = n_rows // tm
    tile_start = jnp.arange(n_tiles, dtype=jnp.int32) * tm
    tile_expert = jnp.minimum(jnp.sum((tile_start[:, None] >= ends[None, :]).astype(jnp.int32), axis=1),
                              N_EXPERTS - 1).astype(jnp.int32)
    n_tiles_used = (ends[-1:] // tm).astype(jnp.int32)

    xs = moe_dispatch(x, pos, n_rows)
    act = moe_up(xs, g, wg, wu, tile_expert, n_tiles_used, tm)
    ys = moe_down(act, wd, tile_expert, n_tiles_used, tm)
    return moe_combine(x, route, ys, pos)


def hgrn2_layer(x, g, w_in, o_norm, w_out, lower_bound, batch, seq):
    (qfig,) = norm_matmul(x, g, w_in.astype(BF16), [(4 * D_MODEL, BF16)])
    o = hgrn_recurrence(qfig, lower_bound, o_norm, batch, seq)
    return matmul_residual(o, w_out.astype(BF16), x)


def swa_layer(x, g, positions, w_in, q_norm, k_norm, sinks, w_out, batch, seq):
    q, kv = norm_matmul(x, g, w_in.astype(BF16),
                        [(D_MODEL, BF16), (2 * SW_KV_HEADS * SW_HEAD_DIM, BF16)])
    pos = positions.astype(F32).reshape(batch * seq, 1)
    o = swa_attention(q, kv, pos, q_norm, k_norm, sinks, batch, seq)
    return matmul_residual(o, w_out.astype(BF16), x)


def gmlp_layer(x, g, w_in, b_in, ln_g, ln_b, w_s, b_s, w_out):
    (z,) = norm_matmul(x, g, w_in.astype(BF16), [(2 * GM_WIDTH, BF16)], bias=b_in, act="gelu")
    o = gmlp_spatial(z, ln_g, ln_b, w_s, b_s)
    return matmul_residual(o, w_out.astype(BF16), x)


def gdn_layer(x, g, w_in, conv_w, a_log, dt_bias, o_norm, w_out, batch, seq):
    d = x.shape[1]
    n_small = w_in.shape[1] - GD_QKV - GD_Z
    w_main = w_in[:, :GD_QKV + GD_Z].astype(BF16)
    w_small = jnp.zeros((d, LANES), F32).at[:, :n_small].set(w_in[:, GD_QKV + GD_Z:].astype(F32))
    w_all = jnp.concatenate([w_main, w_small.astype(BF16)], axis=1)
    qkv, z, ba = norm_matmul(x, g, w_all, [(GD_QKV, BF16), (GD_Z, BF16), (LANES, F32)])
    o = gdn_recurrence(qkv, z, ba, conv_w, a_log, dt_bias, o_norm, batch, seq)
    return matmul_residual(o, w_out.astype(BF16), x)


def dense_ffn(x, g, w_gate, w_up, w_down):
    act = swiglu_up(x, g, w_gate.astype(BF16), w_up.astype(BF16))
    return matmul_residual(act, w_down.astype(BF16), x)


def kernel(x, positions, mix_norm, ffn_norm, hgrn_lb_logits, hgrn_w_in, hgrn_o_norm, hgrn_w_out,
           swa_w_in, swa_q_norm, swa_k_norm, swa_sinks, swa_w_out,
           gmlp_w_in, gmlp_b_in, gmlp_v_ln_g, gmlp_v_ln_b, gmlp_w_s, gmlp_b_s, gmlp_w_out,
           gdn_w_in, gdn_conv_w, gdn_a_log, gdn_dt_bias, gdn_o_norm, gdn_w_out,
           dense_w_gate, dense_w_up, dense_w_down,
           moe_router, moe_w_gate, moe_w_up, moe_w_down):
    batch, seq, d = x.shape
    depth = mix_norm.shape[0]
    lower_bounds = jnp.cumsum(jax.nn.softmax(hgrn_lb_logits.astype(F32), axis=0), axis=0)
    h = x.reshape(batch * seq, d)
    for i in range(depth):
        kind, j = i % 4, i // 4
        if kind == 0:
            h = hgrn2_layer(h, mix_norm[i], hgrn_w_in[j], hgrn_o_norm[j], hgrn_w_out[j],
                            lower_bounds[i], batch, seq)
        elif kind == 1:
            h = swa_layer(h, mix_norm[i], positions, swa_w_in[j], swa_q_norm[j], swa_k_norm[j],
                          swa_sinks[j], swa_w_out[j], batch, seq)
        elif kind == 2:
            h = gmlp_layer(h, mix_norm[i], gmlp_w_in[j], gmlp_b_in[j], gmlp_v_ln_g[j], gmlp_v_ln_b[j],
                           gmlp_w_s[j], gmlp_b_s[j], gmlp_w_out[j])
        else:
            h = gdn_layer(h, mix_norm[i], gdn_w_in[j], gdn_conv_w[j], gdn_a_log[j], gdn_dt_bias[j],
                          gdn_o_norm[j], gdn_w_out[j], batch, seq)
        if i % 2 == 0:
            h = dense_ffn(h, ffn_norm[i], dense_w_gate[i // 2], dense_w_up[i // 2], dense_w_down[i // 2])
        else:
            h = moe_ffn(h, ffn_norm[i], moe_router[i // 2], moe_w_gate[i // 2].astype(BF16),
                        moe_w_up[i // 2].astype(BF16), moe_w_down[i // 2].astype(BF16))
    return h.reshape(batch, seq, d)
```

```python
import functools

import jax
import jax.numpy as jnp
from jax import lax
from jax.experimental import pallas as pl
from jax.experimental.pallas import tpu as pltpu

F32 = jnp.float32
BF16 = jnp.bfloat16
EPS = 1e-6

D_MODEL = 1024
LANES = 128
HG_HEAD_DIM = 128
HG_HEADS = D_MODEL // HG_HEAD_DIM
HG_CHUNK = 32
SW_HEAD_DIM = 64
SW_HEADS = D_MODEL // SW_HEAD_DIM
SW_KV_HEADS = 4
SW_GROUP = SW_HEADS // SW_KV_HEADS
SW_WINDOW = 128
ROPE_THETA = 10000.0
GM_WIDTH = D_MODEL
GM_GROUPS = 8
GM_CHUNK = 128
GD_HEAD_DIM = 128
GD_QK_HEADS = D_MODEL // GD_HEAD_DIM
GD_V_HEADS = 2 * GD_QK_HEADS
GD_CONV = 4
GD_CHUNK = 64
GD_QKV = 2 * GD_QK_HEADS * GD_HEAD_DIM + GD_V_HEADS * GD_HEAD_DIM
GD_Z = GD_V_HEADS * GD_HEAD_DIM
N_EXPERTS = 8
TOP_K = 2

V7X_VMEM_LIMIT_BYTES = 56 * 1024 * 1024

NT_DIMS = (((1,), (1,)), ((), ()))
TN_DIMS = (((0,), (0,)), ((), ()))


def _params(*sem):
    return pltpu.CompilerParams(dimension_semantics=sem, vmem_limit_bytes=V7X_VMEM_LIMIT_BYTES)


def _dot(a, b):
    return jnp.dot(a, b, preferred_element_type=F32)


def _dot_split(a_bf, x):
    hi = x.astype(BF16)
    lo = (x - hi.astype(F32)).astype(BF16)
    return _dot(a_bf, hi) + _dot(a_bf, lo)


def _rms(x, g):
    return x * lax.rsqrt(jnp.mean(x * x, axis=-1, keepdims=True) + EPS) * g


def _silu(x):
    return x * jax.nn.sigmoid(x)


def _row_tile(n, want):
    t = min(n, want)
    assert n % t == 0, (n, t)
    return t


def _norm_mm_body(x_ref, g_ref, w_ref, *rest, widths, chunk, use_bias, act):
    if use_bias:
        b_ref, out_refs = rest[0], rest[1:]
    else:
        b_ref, out_refs = None, rest
    h = _rms(x_ref[...], g_ref[...]).astype(BF16)
    off = 0
    for o_ref, width in zip(out_refs, widths):
        for c in range(0, width, chunk):
            cw = min(chunk, width - c)
            y = _dot(h, w_ref[:, off + c:off + c + cw])
            if use_bias:
                y = y + b_ref[:, off + c:off + c + cw]
            if act == "gelu":
                y = 0.5 * y * (1.0 + lax.erf(y * (0.5 ** 0.5)))
            o_ref[:, c:c + cw] = y.astype(o_ref.dtype)
        off += width


def norm_matmul(x, g, w, outs, bias=None, act=None, tm=512, chunk=512):
    n, d = x.shape
    m = w.shape[1]
    widths = tuple(o[0] for o in outs)
    assert sum(widths) == m
    tm = _row_tile(n, tm)
    in_specs = [pl.BlockSpec((tm, d), lambda i: (i, 0)),
                pl.BlockSpec((1, d), lambda i: (0, 0)),
                pl.BlockSpec((d, m), lambda i: (0, 0))]
    args = [x, g.reshape(1, d).astype(F32), w]
    if bias is not None:
        in_specs.append(pl.BlockSpec((1, m), lambda i: (0, 0)))
        args.append(bias.reshape(1, m).astype(F32))
    body = functools.partial(_norm_mm_body, widths=widths, chunk=chunk,
                             use_bias=bias is not None, act=act)
    return pl.pallas_call(
        body,
        name="norm_matmul",
        out_shape=[jax.ShapeDtypeStruct((n, wd), dt) for wd, dt in outs],
        grid=(n // tm,),
        in_specs=in_specs,
        out_specs=[pl.BlockSpec((tm, wd), lambda i: (i, 0)) for wd, _ in outs],
        compiler_params=_params("parallel"),
    )(*args)


def _mm_res_body(a_ref, w_ref, x_ref, o_ref):
    o_ref[...] = x_ref[...] + _dot(a_ref[...], w_ref[...])


def matmul_residual(a, w, x, tm=512):
    n, k = a.shape
    d = w.shape[1]
    tm = _row_tile(n, tm)
    return pl.pallas_call(
        _mm_res_body,
        name="matmul_residual",
        out_shape=jax.ShapeDtypeStruct((n, d), F32),
        grid=(n // tm,),
        in_specs=[pl.BlockSpec((tm, k), lambda i: (i, 0)),
                  pl.BlockSpec((k, d), lambda i: (0, 0)),
                  pl.BlockSpec((tm, d), lambda i: (i, 0))],
        out_specs=pl.BlockSpec((tm, d), lambda i: (i, 0)),
        compiler_params=_params("parallel"),
    )(a, w, x)


def _swiglu_up_body(x_ref, g_ref, wg_ref, wu_ref, o_ref, *, chunk):
    h = _rms(x_ref[...], g_ref[...]).astype(BF16)
    f = o_ref.shape[1]
    for c in range(0, f, chunk):
        a = _dot(h, wg_ref[:, c:c + chunk])
        b = _dot(h, wu_ref[:, c:c + chunk])
        o_ref[:, c:c + chunk] = (_silu(a) * b).astype(o_ref.dtype)


def swiglu_up(x, g, wg, wu, tm=512, chunk=256):
    n, d = x.shape
    f = wg.shape[1]
    assert f % chunk == 0
    tm = _row_tile(n, tm)
    return pl.pallas_call(
        functools.partial(_swiglu_up_body, chunk=chunk),
        name="swiglu_up",
        out_shape=jax.ShapeDtypeStruct((n, f), BF16),
        grid=(n // tm,),
        in_specs=[pl.BlockSpec((tm, d), lambda i: (i, 0)),
                  pl.BlockSpec((1, d), lambda i: (0, 0)),
                  pl.BlockSpec((d, f), lambda i: (0, 0)),
                  pl.BlockSpec((d, f), lambda i: (0, 0))],
        out_specs=pl.BlockSpec((tm, f), lambda i: (i, 0)),
        compiler_params=_params("parallel"),
    )(x, g.reshape(1, d).astype(F32), wg, wu)


def _hgrn_body(q_ref, f_ref, i_ref, g_ref, lb_ref, on_ref, o_ref, s_ref, *, tb):
    c32 = HG_CHUNK
    hd = HG_HEAD_DIM

    @pl.when(pl.program_id(1) == 0)
    def _():
        s_ref[...] = jnp.zeros_like(s_ref)

    lb = lb_ref[...]
    on = on_ref[...]
    row = lax.broadcasted_iota(jnp.int32, (c32, c32), 0)
    col = lax.broadcasted_iota(jnp.int32, (c32, c32), 1)
    causal = row >= col
    tri_bf = causal.astype(BF16)

    def chunk(c, carry):
        r0 = pl.multiple_of(c * c32, c32)
        rows = pl.ds(r0, c32)
        forget = lb + (1.0 - lb) * jax.nn.sigmoid(f_ref[rows, :].astype(F32))
        b = _dot_split(tri_bf, jnp.log(forget))
        b_mid = b[c32 // 2 - 1:c32 // 2, :]
        b_last = b[c32 - 1:c32, :]
        qs = _silu(q_ref[rows, :].astype(F32))
        k = 1.0 - forget
        v_bf = i_ref[rows, :]
        qa = (qs * jnp.exp(b - b_mid)).astype(BF16)
        ka = (k * jnp.exp(b_mid - b)).astype(BF16)
        q_in = (qs * jnp.exp(b)).astype(BF16)
        k_end = (k * jnp.exp(b_last - b)).astype(BF16)
        d_end = jnp.exp(b_last)
        d_cols = jnp.concatenate([d_end[:, h * hd:(h + 1) * hd] for h in range(HG_HEADS)], axis=0).T
        gate = _silu(g_ref[rows, :].astype(F32))
        heads = range(HG_HEADS)
        sls = [slice(h * hd, (h + 1) * hd) for h in heads]
        s_old = [s_ref[h] for h in heads]
        sc = [lax.dot_general(qa[:, sl], ka[:, sl], NT_DIMS, preferred_element_type=F32) for sl in sls]
        inter = [_dot(q_in[:, sl], s.astype(BF16)) for sl, s in zip(sls, s_old)]
        upd = [lax.dot_general(k_end[:, sl], v_bf[:, sl], TN_DIMS, preferred_element_type=F32) for sl in sls]
        sc_bf = [jnp.where(causal, a, 0.0).astype(BF16) for a in sc]
        o = [_dot(a, v_bf[:, sl]) + b for a, sl, b in zip(sc_bf, sls, inter)]
        for h in heads:
            s_ref[h] = s_old[h] * d_cols[:, h:h + 1] + upd[h]
        outs = [_rms(a, on[:, sl]) for a, sl in zip(o, sls)]
        o_ref[rows, :] = (jnp.concatenate(outs, axis=-1) * gate).astype(o_ref.dtype)
        return carry

    lax.fori_loop(0, tb // c32, chunk, 0, unroll=4)


def hgrn_recurrence(qfig, lb, o_norm, batch, seq, tb=256):
    n = batch * seq
    d = D_MODEL
    tb = _row_tile(seq, tb)
    nt = seq // tb
    col_spec = lambda j: pl.BlockSpec((tb, d), lambda b, t, j=j: (b * nt + t, j))
    return pl.pallas_call(
        functools.partial(_hgrn_body, tb=tb),
        name="hgrn_recurrence",
        out_shape=jax.ShapeDtypeStruct((n, d), BF16),
        grid=(batch, nt),
        in_specs=[col_spec(0), col_spec(1), col_spec(2), col_spec(3),
                  pl.BlockSpec((1, d), lambda b, t: (0, 0)),
                  pl.BlockSpec((1, d), lambda b, t: (0, 0))],
        out_specs=pl.BlockSpec((tb, d), lambda b, t: (b * nt + t, 0)),
        scratch_shapes=[pltpu.VMEM((HG_HEADS, HG_HEAD_DIM, HG_HEAD_DIM), F32)],
        compiler_params=_params("parallel", "arbitrary"),
    )(qfig, qfig, qfig, qfig, lb.reshape(1, d).astype(F32), o_norm.reshape(1, d).astype(F32))


def _swa_body(sink_ref, q_ref, kv_ref, pos_ref, qn_ref, kn_ref, invf_ref, sgn_ref, o_ref,
              kprev_ref, vprev_ref):
    w = SW_WINDOW
    hd = SW_HEAD_DIM
    j = pl.program_id(1)

    @pl.when(j == 0)
    def _():
        kprev_ref[...] = jnp.zeros_like(kprev_ref)
        vprev_ref[...] = jnp.zeros_like(vprev_ref)

    lane = lax.broadcasted_iota(jnp.int32, (w, LANES), 1)
    first_half = (lane % hd) < (hd // 2)
    low_head = lane < hd
    ang = pos_ref[...] * invf_ref[...]
    cos = jnp.cos(ang)
    sin_signed = jnp.sin(ang) * sgn_ref[...]
    gi = lax.broadcasted_iota(jnp.int32, (LANES, LANES), 0) // hd
    gj = lax.broadcasted_iota(jnp.int32, (LANES, LANES), 1) // hd
    head_mean = jnp.where(gi == gj, 1.0 / hd, 0.0).astype(BF16)

    def norm_rope(x, gain):
        xx = x * x
        hi = xx.astype(BF16)
        lo = (xx - hi.astype(F32)).astype(BF16)
        ms = _dot(hi, head_mean) + _dot(lo, head_mean)
        x = x * lax.rsqrt(ms + EPS) * gain
        partner = jnp.where(first_half, pltpu.roll(x, LANES - hd // 2, 1), pltpu.roll(x, hd // 2, 1))
        return x * cos + partner * sin_signed

    qn = qn_ref[...]
    kn = kn_ref[...]
    kv = kv_ref[...].astype(F32)
    n_kv_tiles = SW_KV_HEADS * hd // LANES
    k_cur = [norm_rope(kv[:, t * LANES:(t + 1) * LANES], kn) for t in range(n_kv_tiles)]
    v_cur = [kv[:, (n_kv_tiles + t) * LANES:(n_kv_tiles + t + 1) * LANES] for t in range(n_kv_tiles)]
    k_all = [jnp.concatenate([kprev_ref[t], k_cur[t]], axis=0) for t in range(n_kv_tiles)]
    v_all = [jnp.concatenate([vprev_ref[t], v_cur[t]], axis=0) for t in range(n_kv_tiles)]
    for t in range(n_kv_tiles):
        kprev_ref[t] = k_cur[t]
        vprev_ref[t] = v_cur[t]

    lane2 = lax.broadcasted_iota(jnp.int32, (2 * w, LANES), 1)
    low2 = lane2 < hd
    c_idx = lax.broadcasted_iota(jnp.int32, (w, 2 * w), 0)
    r_idx = lax.broadcasted_iota(jnp.int32, (w, 2 * w), 1)
    rel = c_idx + w - r_idx
    valid = (rel >= 0) & (rel < w) & ((j - 1) * w + r_idx >= 0)
    scale = hd ** -0.5

    k_half, v_half = [], []
    for kvh in range(SW_KV_HEADS):
        kt, vt = k_all[kvh // 2], v_all[kvh // 2]
        kt_sw = pltpu.roll(kt, hd, 1)
        vt_sw = pltpu.roll(vt, hd, 1)
        lo_src, hi_src = ((kt, vt), (kt_sw, vt_sw)) if kvh % 2 == 0 else ((kt_sw, vt_sw), (kt, vt))
        k_half.append((jnp.where(low2, lo_src[0], 0.0).astype(BF16), jnp.where(low2, 0.0, hi_src[0]).astype(BF16)))
        v_half.append((jnp.where(low2, lo_src[1], 0.0).astype(BF16), jnp.where(low2, 0.0, hi_src[1]).astype(BF16)))
    n_pairs = SW_HEADS // 2
    qp = [(norm_rope(q_ref[:, t * LANES:(t + 1) * LANES].astype(F32), qn) * scale).astype(BF16)
          for t in range(n_pairs)]
    heads = range(SW_HEADS)
    s = [jnp.where(valid, lax.dot_general(qp[h // 2], k_half[h // SW_GROUP][h % 2], NT_DIMS,
                                          preferred_element_type=F32), -jnp.inf) for h in heads]
    m = [jnp.maximum(jnp.max(s[h], axis=-1, keepdims=True), sink_ref[h]) for h in heads]
    p = [jnp.exp(s[h] - m[h]) for h in heads]
    denom = [jnp.sum(p[h], axis=-1, keepdims=True) + jnp.exp(sink_ref[h] - m[h]) for h in heads]
    pv = [_dot(p[h].astype(BF16), v_half[h // SW_GROUP][h % 2]) / denom[h] for h in heads]
    for t in range(n_pairs):
        o_ref[:, t * LANES:(t + 1) * LANES] = (pv[2 * t] + pv[2 * t + 1]).astype(o_ref.dtype)


def swa_attention(q, kv, pos, q_norm, k_norm, sinks, batch, seq):
    n = batch * seq
    w = SW_WINDOW
    nb = seq // w
    hd = SW_HEAD_DIM
    lane = jnp.arange(LANES)
    inv_freq = ROPE_THETA ** (-jnp.arange(0, hd, 2, dtype=F32) / hd)
    invf = inv_freq[lane % (hd // 2)].reshape(1, LANES)
    sgn = jnp.where((lane % hd) < hd // 2, -1.0, 1.0).astype(F32).reshape(1, LANES)
    tile2 = lambda g: jnp.tile(g.astype(F32), LANES // hd).reshape(1, LANES)
    n_kv_tiles = SW_KV_HEADS * hd // LANES
    row = lambda b, jj, s: (b * nb + jj, 0)
    const = lambda b, jj, s: (0, 0)
    grid_spec = pltpu.PrefetchScalarGridSpec(
        num_scalar_prefetch=1,
        grid=(batch, nb),
        in_specs=[pl.BlockSpec((w, D_MODEL), row),
                  pl.BlockSpec((w, 2 * SW_KV_HEADS * hd), row),
                  pl.BlockSpec((w, 1), row),
                  pl.BlockSpec((1, LANES), const),
                  pl.BlockSpec((1, LANES), const),
                  pl.BlockSpec((1, LANES), const),
                  pl.BlockSpec((1, LANES), const)],
        out_specs=pl.BlockSpec((w, D_MODEL), row),
        scratch_shapes=[pltpu.VMEM((n_kv_tiles, w, LANES), F32),
                        pltpu.VMEM((n_kv_tiles, w, LANES), F32)],
    )
    return pl.pallas_call(
        _swa_body,
        name="swa_attention",
        out_shape=jax.ShapeDtypeStruct((n, D_MODEL), BF16),
        grid_spec=grid_spec,
        compiler_params=_params("parallel", "arbitrary"),
    )(sinks.astype(F32), q, kv, pos, tile2(q_norm), tile2(k_norm), invf, sgn)


def _gmlp_body(u_ref, v_ref, lg_ref, lbias_ref, ws_ref, bs_ref, o_ref, *, tb):
    c = GM_CHUNK
    gw = GM_WIDTH // GM_GROUPS
    row = lax.broadcasted_iota(jnp.int32, (c, c), 0)
    col = lax.broadcasted_iota(jnp.int32, (c, c), 1)
    causal = row >= col
    wcs = [jnp.where(causal, ws_ref[g], 0.0).astype(BF16) for g in range(GM_GROUPS)]
    bs = bs_ref[...]
    for r in range(0, tb, c):
        v = v_ref[r:r + c, :].astype(F32)
        mu = jnp.mean(v, axis=-1, keepdims=True)
        vc = v - mu
        var = jnp.mean(vc * vc, axis=-1, keepdims=True)
        vn = (vc * lax.rsqrt(var + EPS) * lg_ref[...] + lbias_ref[...]).astype(BF16)
        for g in range(GM_GROUPS):
            sl = slice(g * gw, (g + 1) * gw)
            mixed = _dot(wcs[g], vn[:, sl]) + bs[:, g:g + 1]
            o_ref[r:r + c, sl] = (u_ref[r:r + c, sl].astype(F32) * mixed).astype(o_ref.dtype)


def gmlp_spatial(z, ln_g, ln_b, w_s, b_s, tb=512):
    n = z.shape[0]
    d = GM_WIDTH
    tb = _row_tile(n, tb)
    const2 = lambda i: (0, 0)
    return pl.pallas_call(
        functools.partial(_gmlp_body, tb=tb),
        name="gmlp_spatial",
        out_shape=jax.ShapeDtypeStruct((n, d), BF16),
        grid=(n // tb,),
        in_specs=[pl.BlockSpec((tb, d), lambda i: (i, 0)),
                  pl.BlockSpec((tb, d), lambda i: (i, 1)),
                  pl.BlockSpec((1, d), const2),
                  pl.BlockSpec((1, d), const2),
                  pl.BlockSpec((GM_GROUPS, GM_CHUNK, GM_CHUNK), lambda i: (0, 0, 0)),
                  pl.BlockSpec((GM_CHUNK, GM_GROUPS), const2)],
        out_specs=pl.BlockSpec((tb, d), lambda i: (i, 0)),
        compiler_params=_params("parallel"),
    )(z, z, ln_g.reshape(1, d).astype(F32), ln_b.reshape(1, d).astype(F32),
      w_s.astype(F32), b_s.T.astype(F32))


def _gdn_body(qkv_ref, z_ref, ba_ref, cw_ref, alog_ref, dtb_ref, on_ref, o_ref,
              s_ref, x_ref, y_ref, *, tb):
    c = GD_CHUNK
    hd = GD_HEAD_DIM
    halo = 8
    nq = GD_QK_HEADS * hd
    it = pl.program_id(1)

    @pl.when(it == 0)
    def _():
        s_ref[...] = jnp.zeros_like(s_ref)
        x_ref[0:halo, :] = jnp.zeros((halo, GD_QKV), F32)

    @pl.when(it > 0)
    def _():
        x_ref[0:halo, :] = x_ref[tb:tb + halo, :]

    x_ref[halo:halo + tb, :] = qkv_ref[...].astype(F32)
    acc = None
    for jj in range(GD_CONV):
        start = halo - (GD_CONV - 1) + jj
        term = cw_ref[jj:jj + 1, :] * x_ref[start:start + tb, :]
        acc = term if acc is None else acc + term
    y_ref[...] = _silu(acc)
    for h in range(2 * GD_QK_HEADS):
        sl = slice(h * hd, (h + 1) * hd)
        a = y_ref[:, sl]
        a = a * lax.rsqrt(jnp.sum(a * a, axis=-1, keepdims=True) + EPS)
        if h < GD_QK_HEADS:
            a = a * (hd ** -0.5)
        y_ref[:, sl] = a

    row = lax.broadcasted_iota(jnp.int32, (c, c), 0)
    col = lax.broadcasted_iota(jnp.int32, (c, c), 1)
    tri = row >= col
    tri_strict = row > col
    tri_bf = tri.astype(BF16)
    lane = lax.broadcasted_iota(jnp.int32, (1, LANES), 1)
    neg_a = -jnp.exp(alog_ref[...])
    on = on_ref[...]

    def chunk(ci, carry):
        r0 = pl.multiple_of(ci * c, c)
        rows = pl.ds(r0, c)
        ba = ba_ref[rows, :]
        beta_all = jax.nn.sigmoid(ba)
        g_all = neg_a * jax.nn.softplus(ba + dtb_ref[...])
        g_all = jnp.where((lane >= GD_V_HEADS) & (lane < 2 * GD_V_HEADS), g_all, 0.0)
        gam_all = _dot_split(tri_bf, g_all)
        gam_rows = gam_all.T
        e_gam_all = jnp.exp(gam_all)
        gam_last_all = gam_all[c - 1:c, :]
        e_end_all = jnp.exp(gam_last_all - gam_all)
        d_end_all = jnp.exp(gam_last_all)
        heads = range(GD_V_HEADS)
        rep = GD_V_HEADS // GD_QK_HEADS
        q = [y_ref[rows, hk * hd:(hk + 1) * hd] for hk in range(GD_QK_HEADS)]
        k = [y_ref[rows, nq + hk * hd:nq + (hk + 1) * hd] for hk in range(GD_QK_HEADS)]
        k_bf = [a.astype(BF16) for a in k]
        kk = [lax.dot_general(a, a, NT_DIMS, preferred_element_type=F32) for a in k_bf]
        qk = [lax.dot_general(a.astype(BF16), b, NT_DIMS, preferred_element_type=F32)
              for a, b in zip(q, k_bf)]
        col = lambda a, hv: a[:, hv:hv + 1]
        gcol = lambda a, hv: a[:, GD_V_HEADS + hv:GD_V_HEADS + hv + 1]
        decay = [jnp.exp(jnp.where(tri, gcol(gam_all, hv) - gam_rows[GD_V_HEADS + hv:GD_V_HEADS + hv + 1, :],
                                   -jnp.inf)) for hv in heads]
        p = [-jnp.where(tri_strict, kk[hv // rep] * decay[hv] * col(beta_all, hv), 0.0) for hv in heads]
        sol = [jnp.concatenate(
            [y_ref[rows, 2 * nq + hv * hd:2 * nq + (hv + 1) * hd] * col(beta_all, hv),
             k[hv // rep] * (col(beta_all, hv) * gcol(e_gam_all, hv))], axis=-1) for hv in heads]
        n_steps = c.bit_length() - 1
        for step in range(n_steps):
            p_bf = [a.astype(BF16) for a in p]
            sol = [x + _dot(pb, x.astype(BF16)) for x, pb in zip(sol, p_bf)]
            if step + 1 < n_steps:
                p = [_dot(pb, pb) for pb in p_bf]
        s_old = [s_ref[hv] for hv in heads]
        s_bf = [a.astype(BF16) for a in s_old]
        nv_bf = [(sol[hv][:, :hd] - _dot(sol[hv][:, hd:].astype(BF16), s_bf[hv])).astype(BF16) for hv in heads]
        o = [_dot((q[hv // rep] * gcol(e_gam_all, hv)).astype(BF16), s_bf[hv])
             + _dot((qk[hv // rep] * decay[hv]).astype(BF16), nv_bf[hv]) for hv in heads]
        s_new = [s_old[hv] * gcol(d_end_all, hv) + lax.dot_general(
            (k[hv // rep] * gcol(e_end_all, hv)).astype(BF16), nv_bf[hv], TN_DIMS,
            preferred_element_type=F32) for hv in heads]
        for hv in heads:
            s_ref[hv] = s_new[hv]
        outs = [_rms(o[hv], on) * _silu(z_ref[rows, hv * hd:(hv + 1) * hd].astype(F32)) for hv in heads]
        o_ref[rows, :] = jnp.concatenate(outs, axis=-1).astype(o_ref.dtype)
        return carry

    lax.fori_loop(0, tb // c, chunk, 0)


def gdn_recurrence(qkv, z, ba, conv_w, a_log, dt_bias, o_norm, batch, seq, tb=256):
    n = batch * seq
    tb = _row_tile(seq, tb)
    nt = seq // tb
    pad_heads = lambda a: jnp.zeros((1, LANES), F32).at[0, GD_V_HEADS:2 * GD_V_HEADS].set(a.astype(F32))
    row = lambda b, t: (b * nt + t, 0)
    const = lambda b, t: (0, 0)
    return pl.pallas_call(
        functools.partial(_gdn_body, tb=tb),
        name="gdn_recurrence",
        out_shape=jax.ShapeDtypeStruct((n, GD_Z), BF16),
        grid=(batch, nt),
        in_specs=[pl.BlockSpec((tb, GD_QKV), row),
                  pl.BlockSpec((tb, GD_Z), row),
                  pl.BlockSpec((tb, LANES), row),
                  pl.BlockSpec((GD_CONV, GD_QKV), const),
                  pl.BlockSpec((1, LANES), const),
                  pl.BlockSpec((1, LANES), const),
                  pl.BlockSpec((1, GD_HEAD_DIM), const)],
        out_specs=pl.BlockSpec((tb, GD_Z), row),
        scratch_shapes=[pltpu.VMEM((GD_V_HEADS, GD_HEAD_DIM, GD_HEAD_DIM), F32),
                        pltpu.VMEM((tb + 16, GD_QKV), F32),
                        pltpu.VMEM((tb, GD_QKV), F32)],
        compiler_params=_params("parallel", "arbitrary"),
    )(qkv, z, ba, conv_w.astype(F32), pad_heads(a_log), pad_heads(dt_bias),
      o_norm.reshape(1, GD_HEAD_DIM).astype(F32))


def _router_body(x_ref, g_ref, wr_ref, o_ref):
    h = _rms(x_ref[...], g_ref[...])
    logits = jnp.dot(h, wr_ref[...], preferred_element_type=F32, precision=lax.Precision.HIGHEST)
    lane = lax.broadcasted_iota(jnp.int32, logits.shape, 1)
    logits = jnp.where(lane < N_EXPERTS, logits, -jnp.inf)
    m1 = jnp.max(logits, axis=-1, keepdims=True)
    i1 = jnp.min(jnp.where(logits == m1, lane, LANES), axis=-1, keepdims=True)
    rest = jnp.where(lane == i1, -jnp.inf, logits)
    m2 = jnp.max(rest, axis=-1, keepdims=True)
    i2 = jnp.min(jnp.where(rest == m2, lane, LANES), axis=-1, keepdims=True)
    e2 = jnp.exp(m2 - m1)
    w1 = 1.0 / (1.0 + e2)
    w2 = e2 / (1.0 + e2)
    out = jnp.where(lane == 0, i1.astype(F32),
                    jnp.where(lane == 1, i2.astype(F32),
                              jnp.where(lane == 2, w1, jnp.where(lane == 3, w2, 0.0))))
    o_ref[...] = out


def moe_router(x, g, router, tm=512):
    n, d = x.shape
    tm = _row_tile(n, tm)
    wr = jnp.zeros((d, LANES), F32).at[:, :N_EXPERTS].set(router.astype(F32))
    return pl.pallas_call(
        _router_body,
        name="moe_router",
        out_shape=jax.ShapeDtypeStruct((n, LANES), F32),
        grid=(n // tm,),
        in_specs=[pl.BlockSpec((tm, d), lambda i: (i, 0)),
                  pl.BlockSpec((1, d), lambda i: (0, 0)),
                  pl.BlockSpec((d, LANES), lambda i: (0, 0))],
        out_specs=pl.BlockSpec((tm, LANES), lambda i: (i, 0)),
        compiler_params=_params("parallel"),
    )(x, g.reshape(1, d).astype(F32), wr)


def _row_copy(src_ref, src_row, dst_ref, dst_row, sem):
    return pltpu.make_async_copy(src_ref.at[pl.ds(src_row, 1), :], dst_ref.at[pl.ds(dst_row, 1), :], sem)


def _dispatch_body(pos_ref, x_ref, xs_in_ref, xs_ref, sem, *, rt):
    del xs_in_ref

    def start(t, carry):
        for k in range(TOP_K):
            _row_copy(x_ref, t, xs_ref, pos_ref[0, 0, TOP_K * t + k], sem).start()
        return carry

    lax.fori_loop(0, rt, start, 0, unroll=8)
    all_rows = xs_ref.at[pl.ds(0, TOP_K * rt), :]
    pltpu.make_async_copy(all_rows, all_rows, sem).wait()


def moe_dispatch(x, pos, n_rows, rt=256):
    n, d = x.shape
    rt = _row_tile(n, rt)
    grid_spec = pltpu.PrefetchScalarGridSpec(
        num_scalar_prefetch=0,
        grid=(n // rt,),
        in_specs=[pl.BlockSpec((1, 1, TOP_K * rt), lambda i: (i, 0, 0), memory_space=pltpu.SMEM),
                  pl.BlockSpec((rt, d), lambda i: (i, 0)),
                  pl.BlockSpec(memory_space=pl.ANY)],
        out_specs=pl.BlockSpec(memory_space=pl.ANY),
        scratch_shapes=[pltpu.SemaphoreType.DMA(())],
    )
    return pl.pallas_call(
        functools.partial(_dispatch_body, rt=rt),
        name="moe_dispatch",
        out_shape=jax.ShapeDtypeStruct((n_rows, d), F32),
        grid_spec=grid_spec,
        input_output_aliases={2: 0},
        compiler_params=_params("arbitrary"),
    )(pos.reshape(n // rt, 1, TOP_K * rt), x, jnp.zeros((n_rows, d), F32))


def _moe_up_body(te_ref, nt_ref, xs_ref, g_ref, wg_ref, wu_ref, o_ref, *, chunk):
    del te_ref
    i = pl.program_id(1)

    @pl.when(i < nt_ref[0])
    def _():
        h = _rms(xs_ref[...], g_ref[...]).astype(BF16)
        f = o_ref.shape[1]
        for c in range(0, f, chunk):
            a = _dot(h, wg_ref[0, :, c:c + chunk])
            b = _dot(h, wu_ref[0, :, c:c + chunk])
            o_ref[:, c:c + chunk] = (_silu(a) * b).astype(o_ref.dtype)

    @pl.when(i >= nt_ref[0])
    def _():
        o_ref[...] = jnp.zeros_like(o_ref)


def moe_up(xs, g, wg, wu, tile_expert, n_tiles_used, tm, fsplit=2, chunk=256):
    p, d = xs.shape
    f = wg.shape[2]
    fb = f // fsplit
    assert fb % chunk == 0 and p % tm == 0
    grid_spec = pltpu.PrefetchScalarGridSpec(
        num_scalar_prefetch=2,
        grid=(fsplit, p // tm),
        in_specs=[pl.BlockSpec((tm, d), lambda j, i, te, nt: (i, 0)),
                  pl.BlockSpec((1, d), lambda j, i, te, nt: (0, 0)),
                  pl.BlockSpec((1, d, fb), lambda j, i, te, nt: (te[i], 0, j)),
                  pl.BlockSpec((1, d, fb), lambda j, i, te, nt: (te[i], 0, j))],
        out_specs=pl.BlockSpec((tm, fb), lambda j, i, te, nt: (i, j)),
    )
    return pl.pallas_call(
        functools.partial(_moe_up_body, chunk=chunk),
        name="moe_up",
        out_shape=jax.ShapeDtypeStruct((p, f), BF16),
        grid_spec=grid_spec,
        compiler_params=_params("arbitrary", "arbitrary"),
    )(tile_expert, n_tiles_used, xs, g.reshape(1, d).astype(F32), wg, wu)


def _moe_down_body(te_ref, nt_ref, a_ref, wd_ref, o_ref):
    del te_ref
    i = pl.program_id(0)

    @pl.when(i < nt_ref[0])
    def _():
        o_ref[...] = _dot(a_ref[...], wd_ref[0])

    @pl.when(i >= nt_ref[0])
    def _():
        o_ref[...] = jnp.zeros_like(o_ref)


def moe_down(act, wd, tile_expert, n_tiles_used, tm):
    p, f = act.shape
    d = wd.shape[2]
    grid_spec = pltpu.PrefetchScalarGridSpec(
        num_scalar_prefetch=2,
        grid=(p // tm,),
        in_specs=[pl.BlockSpec((tm, f), lambda i, te, nt: (i, 0)),
                  pl.BlockSpec((1, f, d), lambda i, te, nt: (te[i], 0, 0))],
        out_specs=pl.BlockSpec((tm, d), lambda i, te, nt: (i, 0)),
    )
    return pl.pallas_call(
        _moe_down_body,
        name="moe_down",
        out_shape=jax.ShapeDtypeStruct((p, d), F32),
        grid_spec=grid_spec,
        compiler_params=_params("arbitrary"),
    )(tile_expert, n_tiles_used, act, wd)


def _combine_body(pos_ref, x_ref, r_ref, ys_ref, o_ref, buf_ref, sem, *, rt):
    def start(t, carry):
        for k in range(TOP_K):
            _row_copy(ys_ref, pos_ref[0, 0, TOP_K * t + k], buf_ref.at[k], t, sem).start()
        return carry

    lax.fori_loop(0, rt, start, 0, unroll=8)
    for k in range(TOP_K):
        pltpu.make_async_copy(ys_ref.at[pl.ds(0, rt), :], buf_ref.at[k], sem).wait()
    r = r_ref[...]
    o_ref[...] = x_ref[...] + r[:, 2:3] * buf_ref[0] + r[:, 3:4] * buf_ref[1]


def moe_combine(x, route, ys, pos, rt=256):
    n, d = x.shape
    rt = _row_tile(n, rt)
    grid_spec = pltpu.PrefetchScalarGridSpec(
        num_scalar_prefetch=0,
        grid=(n // rt,),
        in_specs=[pl.BlockSpec((1, 1, TOP_K * rt), lambda i: (i, 0, 0), memory_space=pltpu.SMEM),
                  pl.BlockSpec((rt, d), lambda i: (i, 0)),
                  pl.BlockSpec((rt, LANES), lambda i: (i, 0)),
                  pl.BlockSpec(memory_space=pl.ANY)],
        out_specs=pl.BlockSpec((rt, d), lambda i: (i, 0)),
        scratch_shapes=[pltpu.VMEM((TOP_K, rt, d), F32), pltpu.SemaphoreType.DMA(())],
    )
    return pl.pallas_call(
        functools.partial(_combine_body, rt=rt),
        name="moe_combine",
        out_shape=jax.ShapeDtypeStruct((n, d), F32),
        grid_spec=grid_spec,
        compiler_params=_params("arbitrary"),
    )(pos.reshape(n // rt, 1, TOP_K * rt), x, route, ys)


def moe_ffn(x, g, router, wg, wu, wd, tm=512):
    n, d = x.shape
    route = moe_router(x, g, router)
    expert = route[:, :TOP_K].astype(jnp.int32).reshape(n * TOP_K)
    onehot = (expert[:, None] == jnp.arange(N_EXPERTS, dtype=jnp.int32)[None, :]).astype(jnp.int32)
    csum = jnp.cumsum(onehot, axis=0)
    rank = jnp.sum((csum - onehot) * onehot, axis=1)
    counts = csum[-1]
    padded = ((counts + tm - 1) // tm) * tm
    ends = jnp.cumsum(padded)
    starts = ends - padded
    pos = (jnp.sum(starts[None, :] * onehot, axis=1) + rank).astype(jnp.int32)
    n_rows = n * TOP_K + N_EXPERTS * tm
    n_tiles = n_rows // tm
    tile_start = jnp.arange(n_tiles, dtype=jnp.int32) * tm
    tile_expert = jnp.minimum(jnp.sum((tile_start[:, None] >= ends[None, :]).astype(jnp.int32), axis=1),
                              N_EXPERTS - 1).astype(jnp.int32)
    n_tiles_used = (ends[-1:] // tm).astype(jnp.int32)

    xs = moe_dispatch(x, pos, n_rows)
    act = moe_up(xs, g, wg, wu, tile_expert, n_tiles_used, tm)
    ys = moe_down(act, wd, tile_expert, n_tiles_used, tm)
    return moe_combine(x, route, ys, pos)


def hgrn2_layer(x, g, w_in, o_norm, w_out, lower_bound, batch, seq):
    (qfig,) = norm_matmul(x, g, w_in.astype(BF16), [(4 * D_MODEL, BF16)])
    o = hgrn_recurrence(qfig, lower_bound, o_norm, batch, seq)
    return matmul_residual(o, w_out.astype(BF16), x)


def swa_layer(x, g, positions, w_in, q_norm, k_norm, sinks, w_out, batch, seq):
    q, kv = norm_matmul(x, g, w_in.astype(BF16),
                        [(D_MODEL, BF16), (2 * SW_KV_HEADS * SW_HEAD_DIM, BF16)])
    pos = positions.astype(F32).reshape(batch * seq, 1)
    o = swa_attention(q, kv, pos, q_norm, k_norm, sinks, batch, seq)
    return matmul_residual(o, w_out.astype(BF16), x)


def gmlp_layer(x, g, w_in, b_in, ln_g, ln_b, w_s, b_s, w_out):
    (z,) = norm_matmul(x, g, w_in.astype(BF16), [(2 * GM_WIDTH, BF16)], bias=b_in, act="gelu")
    o = gmlp_spatial(z, ln_g, ln_b, w_s, b_s)
    return matmul_residual(o, w_out.astype(BF16), x)


def gdn_layer(x, g, w_in, conv_w, a_log, dt_bias, o_norm, w_out, batch, seq):
    d = x.shape[1]
    n_small = w_in.shape[1] - GD_QKV - GD_Z
    w_main = w_in[:, :GD_QKV + GD_Z].astype(BF16)
    w_small = jnp.zeros((d, LANES), F32).at[:, :n_small].set(w_in[:, GD_QKV + GD_Z:].astype(F32))
    w_all = jnp.concatenate([w_main, w_small.astype(BF16)], axis=1)
    qkv, z, ba = norm_matmul(x, g, w_all, [(GD_QKV, BF16), (GD_Z, BF16), (LANES, F32)])
    o = gdn_recurrence(qkv, z, ba, conv_w, a_log, dt_bias, o_norm, batch, seq)
    return matmul_residual(o, w_out.astype(BF16), x)


def dense_ffn(x, g, w_gate, w_up, w_down):
    act = swiglu_up(x, g, w_gate.astype(BF16), w_up.astype(BF16))
    return matmul_residual(act, w_down.astype(BF16), x)


def kernel(x, positions, mix_norm, ffn_norm, hgrn_lb_logits, hgrn_w_in, hgrn_o_norm, hgrn_w_out,
           swa_w_in, swa_q_norm, swa_k_norm, swa_sinks, swa_w_out,
           gmlp_w_in, gmlp_b_in, gmlp_v_ln_g, gmlp_v_ln_b, gmlp_w_s, gmlp_b_s, gmlp_w_out,
           gdn_w_in, gdn_conv_w, gdn_a_log, gdn_dt_bias, gdn_o_norm, gdn_w_out,
           dense_w_gate, dense_w_up, dense_w_down,
           moe_router, moe_w_gate, moe_w_up, moe_w_down):
    batch, seq, d = x.shape
    depth = mix_norm.shape[0]
    lower_bounds = jnp.cumsum(jax.nn.softmax(hgrn_lb_logits.astype(F32), axis=0), axis=0)
    h = x.reshape(batch * seq, d)
    for i in range(depth):
        kind, j = i % 4, i // 4
        if kind == 0:
            h = hgrn2_layer(h, mix_norm[i], hgrn_w_in[j], hgrn_o_norm[j], hgrn_w_out[j],
                            lower_bounds[i], batch, seq)
        elif kind == 1:
            h = swa_layer(h, mix_norm[i], positions, swa_w_in[j], swa_q_norm[j], swa_k_norm[j],
                          swa_sinks[j], swa_w_out[j], batch, seq)
        elif kind == 2:
            h = gmlp_layer(h, mix_norm[i], gmlp_w_in[j], gmlp_b_in[j], gmlp_v_ln_g[j], gmlp_v_ln_b[j],
                           gmlp_w_s[j], gmlp_b_s[j], gmlp_w_out[j])
        else:
            h = gdn_layer(h, mix_norm[i], gdn_w_in[j], gdn_conv_w[j], gdn_a_log[j], gdn_dt_bias[j],
                          gdn_o_norm[j], gdn_w_out[j], batch, seq)
        if i % 2 == 0:
            h = dense_ffn(h, ffn_norm[i], dense_w_gate[i // 2], dense_w_up[i // 2], dense_w_down[i // 2])
        else:
            h = moe_ffn(h, ffn_norm[i], moe_router[i // 2], moe_w_gate[i // 2].astype(BF16),
                        moe_w_up[i // 2].astype(BF16), moe_w_down[i // 2].astype(BF16))
    return h.reshape(batch, seq, d)
```

```python
import functools

import jax
import jax.numpy as jnp
from jax import lax
from jax.experimental import pallas as pl
from jax.experimental.pallas import tpu as pltpu

F32 = jnp.float32
BF16 = jnp.bfloat16
EPS = 1e-6

D_MODEL = 1024
LANES = 128
SUBLANES = 8
HG_HEAD_DIM = 128
HG_HEADS = D_MODEL // HG_HEAD_DIM
HG_CHUNK = 32
SW_HEAD_DIM = 64
SW_HEADS = D_MODEL // SW_HEAD_DIM
SW_KV_HEADS = 4
SW_GROUP = SW_HEADS // SW_KV_HEADS
SW_WINDOW = 128
ROPE_THETA = 10000.0
GM_WIDTH = D_MODEL
GM_GROUPS = 8
GM_CHUNK = 128
GD_HEAD_DIM = 128
GD_QK_HEADS = D_MODEL // GD_HEAD_DIM
GD_V_HEADS = 2 * GD_QK_HEADS
GD_CONV = 4
GD_CHUNK = 64
GD_QKV = 2 * GD_QK_HEADS * GD_HEAD_DIM + GD_V_HEADS * GD_HEAD_DIM
GD_Z = GD_V_HEADS * GD_HEAD_DIM
N_EXPERTS = 8
TOP_K = 2

V7X_VMEM_LIMIT_BYTES = 56 * 1024 * 1024

NT_DIMS = (((1,), (1,)), ((), ()))
TN_DIMS = (((0,), (0,)), ((), ()))


def _params(*sem):
    return pltpu.CompilerParams(dimension_semantics=sem, vmem_limit_bytes=V7X_VMEM_LIMIT_BYTES)


def _dot(a, b):
    return jnp.dot(a, b, preferred_element_type=F32)


def _dot_split(a_bf, x):
    hi = x.astype(BF16)
    lo = (x - hi.astype(F32)).astype(BF16)
    return _dot(a_bf, hi) + _dot(a_bf, lo)


def _rms(x, g):
    return x * lax.rsqrt(jnp.mean(x * x, axis=-1, keepdims=True) + EPS) * g


def _silu(x):
    return x * jax.nn.sigmoid(x)


def _row_tile(n, want):
    t = min(n, want)
    assert n % t == 0, (n, t)
    return t


def _norm_mm_body(x_ref, g_ref, w_ref, *rest, widths, chunk, use_bias, act):
    if use_bias:
        b_ref, out_refs = rest[0], rest[1:]
    else:
        b_ref, out_refs = None, rest
    h = _rms(x_ref[...], g_ref[...]).astype(BF16)
    off = 0
    for o_ref, width in zip(out_refs, widths):
        for c in range(0, width, chunk):
            cw = min(chunk, width - c)
            y = _dot(h, w_ref[:, off + c:off + c + cw])
            if use_bias:
                y = y + b_ref[:, off + c:off + c + cw]
            if act == "gelu":
                y = 0.5 * y * (1.0 + lax.erf(y * (0.5 ** 0.5)))
            o_ref[:, c:c + cw] = y.astype(o_ref.dtype)
        off += width


def norm_matmul(x, g, w, outs, bias=None, act=None, tm=512, chunk=512):
    n, d = x.shape
    m = w.shape[1]
    widths = tuple(o[0] for o in outs)
    assert sum(widths) == m
    tm = _row_tile(n, tm)
    in_specs = [pl.BlockSpec((tm, d), lambda i: (i, 0)),
                pl.BlockSpec((1, d), lambda i: (0, 0)),
                pl.BlockSpec((d, m), lambda i: (0, 0))]
    args = [x, g.reshape(1, d).astype(F32), w]
    if bias is not None:
        in_specs.append(pl.BlockSpec((1, m), lambda i: (0, 0)))
        args.append(bias.reshape(1, m).astype(F32))
    body = functools.partial(_norm_mm_body, widths=widths, chunk=chunk,
                             use_bias=bias is not None, act=act)
    return pl.pallas_call(
        body,
        name="norm_matmul",
        out_shape=[jax.ShapeDtypeStruct((n, wd), dt) for wd, dt in outs],
        grid=(n // tm,),
        in_specs=in_specs,
        out_specs=[pl.BlockSpec((tm, wd), lambda i: (i, 0)) for wd, _ in outs],
        compiler_params=_params("parallel"),
    )(*args)


def _mm_res_body(a_ref, w_ref, x_ref, o_ref):
    o_ref[...] = x_ref[...] + _dot(a_ref[...], w_ref[...])


def matmul_residual(a, w, x, tm=512):
    n, k = a.shape
    d = w.shape[1]
    tm = _row_tile(n, tm)
    return pl.pallas_call(
        _mm_res_body,
        name="matmul_residual",
        out_shape=jax.ShapeDtypeStruct((n, d), F32),
        grid=(n // tm,),
        in_specs=[pl.BlockSpec((tm, k), lambda i: (i, 0)),
                  pl.BlockSpec((k, d), lambda i: (0, 0)),
                  pl.BlockSpec((tm, d), lambda i: (i, 0))],
        out_specs=pl.BlockSpec((tm, d), lambda i: (i, 0)),
        compiler_params=_params("parallel"),
    )(a, w, x)


def _swiglu_up_body(x_ref, g_ref, wg_ref, wu_ref, o_ref, *, chunk):
    h = _rms(x_ref[...], g_ref[...]).astype(BF16)
    f = o_ref.shape[1]
    for c in range(0, f, chunk):
        a = _dot(h, wg_ref[:, c:c + chunk])
        b = _dot(h, wu_ref[:, c:c + chunk])
        o_ref[:, c:c + chunk] = (_silu(a) * b).astype(o_ref.dtype)


def swiglu_up(x, g, wg, wu, tm=512, chunk=256):
    n, d = x.shape
    f = wg.shape[1]
    assert f % chunk == 0
    tm = _row_tile(n, tm)
    return pl.pallas_call(
        functools.partial(_swiglu_up_body, chunk=chunk),
        name="swiglu_up",
        out_shape=jax.ShapeDtypeStruct((n, f), BF16),
        grid=(n // tm,),
        in_specs=[pl.BlockSpec((tm, d), lambda i: (i, 0)),
                  pl.BlockSpec((1, d), lambda i: (0, 0)),
                  pl.BlockSpec((d, f), lambda i: (0, 0)),
                  pl.BlockSpec((d, f), lambda i: (0, 0))],
        out_specs=pl.BlockSpec((tm, f), lambda i: (i, 0)),
        compiler_params=_params("parallel"),
    )(x, g.reshape(1, d).astype(F32), wg, wu)


def _hgrn_body(q_ref, f_ref, i_ref, g_ref, lb_ref, on_ref, o_ref, s_ref, *, tb):
    c32 = HG_CHUNK
    hd = HG_HEAD_DIM

    @pl.when(pl.program_id(1) == 0)
    def _():
        s_ref[...] = jnp.zeros_like(s_ref)

    lb = lb_ref[...]
    on = on_ref[...]
    row = lax.broadcasted_iota(jnp.int32, (c32, c32), 0)
    col = lax.broadcasted_iota(jnp.int32, (c32, c32), 1)
    causal = row >= col
    tri_bf = causal.astype(BF16)

    def chunk(c, carry):
        r0 = pl.multiple_of(c * c32, c32)
        rows = pl.ds(r0, c32)
        forget = lb + (1.0 - lb) * jax.nn.sigmoid(f_ref[rows, :].astype(F32))
        b = _dot_split(tri_bf, jnp.log(forget))
        b_mid = b[c32 // 2 - 1:c32 // 2, :]
        b_last = b[c32 - 1:c32, :]
        qs = _silu(q_ref[rows, :].astype(F32))
        k = 1.0 - forget
        v_bf = i_ref[rows, :]
        qa = (qs * jnp.exp(b - b_mid)).astype(BF16)
        ka = (k * jnp.exp(b_mid - b)).astype(BF16)
        q_in = (qs * jnp.exp(b)).astype(BF16)
        k_end = (k * jnp.exp(b_last - b)).astype(BF16)
        d_end = jnp.exp(b_last)
        d_cols = jnp.concatenate([d_end[:, h * hd:(h + 1) * hd] for h in range(HG_HEADS)], axis=0).T
        gate = _silu(g_ref[rows, :].astype(F32))
        heads = range(HG_HEADS)
        sls = [slice(h * hd, (h + 1) * hd) for h in heads]
        s_old = [s_ref[h] for h in heads]
        sc = [lax.dot_general(qa[:, sl], ka[:, sl], NT_DIMS, preferred_element_type=F32) for sl in sls]
        inter = [_dot(q_in[:, sl], s.astype(BF16)) for sl, s in zip(sls, s_old)]
        upd = [lax.dot_general(k_end[:, sl], v_bf[:, sl], TN_DIMS, preferred_element_type=F32) for sl in sls]
        sc_bf = [jnp.where(causal, a, 0.0).astype(BF16) for a in sc]
        o = [_dot(a, v_bf[:, sl]) + b for a, sl, b in zip(sc_bf, sls, inter)]
        for h in heads:
            s_ref[h] = s_old[h] * d_cols[:, h:h + 1] + upd[h]
        outs = [_rms(a, on[:, sl]) for a, sl in zip(o, sls)]
        o_ref[rows, :] = (jnp.concatenate(outs, axis=-1) * gate).astype(o_ref.dtype)
        return carry

    lax.fori_loop(0, tb // c32, chunk, 0, unroll=8)


def hgrn_recurrence(qfig, lb, o_norm, batch, seq, tb=256):
    n = batch * seq
    d = D_MODEL
    tb = _row_tile(seq, tb)
    nt = seq // tb
    col_spec = lambda j: pl.BlockSpec((tb, d), lambda b, t, j=j: (b * nt + t, j))
    return pl.pallas_call(
        functools.partial(_hgrn_body, tb=tb),
        name="hgrn_recurrence",
        out_shape=jax.ShapeDtypeStruct((n, d), BF16),
        grid=(batch, nt),
        in_specs=[col_spec(0), col_spec(1), col_spec(2), col_spec(3),
                  pl.BlockSpec((1, d), lambda b, t: (0, 0)),
                  pl.BlockSpec((1, d), lambda b, t: (0, 0))],
        out_specs=pl.BlockSpec((tb, d), lambda b, t: (b * nt + t, 0)),
        scratch_shapes=[pltpu.VMEM((HG_HEADS, HG_HEAD_DIM, HG_HEAD_DIM), F32)],
        compiler_params=_params("parallel", "arbitrary"),
    )(qfig, qfig, qfig, qfig, lb.reshape(1, d).astype(F32), o_norm.reshape(1, d).astype(F32))


def _swa_body(sink_ref, q_ref, kv_ref, pos_ref, qn_ref, kn_ref, invf_ref, sgn_ref, o_ref,
              kprev_ref, vprev_ref):
    w = SW_WINDOW
    hd = SW_HEAD_DIM
    j = pl.program_id(1)

    @pl.when(j == 0)
    def _():
        kprev_ref[...] = jnp.zeros_like(kprev_ref)
        vprev_ref[...] = jnp.zeros_like(vprev_ref)

    lane = lax.broadcasted_iota(jnp.int32, (w, LANES), 1)
    first_half = (lane % hd) < (hd // 2)
    low_head = lane < hd
    ang = pos_ref[...] * invf_ref[...]
    cos = jnp.cos(ang)
    sin_signed = jnp.sin(ang) * sgn_ref[...]
    gi = lax.broadcasted_iota(jnp.int32, (LANES, LANES), 0) // hd
    gj = lax.broadcasted_iota(jnp.int32, (LANES, LANES), 1) // hd
    head_mean = jnp.where(gi == gj, 1.0 / hd, 0.0).astype(BF16)

    def norm_rope(x, gain):
        xx = x * x
        hi = xx.astype(BF16)
        lo = (xx - hi.astype(F32)).astype(BF16)
        ms = _dot(hi, head_mean) + _dot(lo, head_mean)
        x = x * lax.rsqrt(ms + EPS) * gain
        partner = jnp.where(first_half, pltpu.roll(x, LANES - hd // 2, 1), pltpu.roll(x, hd // 2, 1))
        return x * cos + partner * sin_signed

    qn = qn_ref[...]
    kn = kn_ref[...]
    kv = kv_ref[...].astype(F32)
    n_kv_tiles = SW_KV_HEADS * hd // LANES
    k_cur = [norm_rope(kv[:, t * LANES:(t + 1) * LANES], kn) for t in range(n_kv_tiles)]
    v_cur = [kv[:, (n_kv_tiles + t) * LANES:(n_kv_tiles + t + 1) * LANES] for t in range(n_kv_tiles)]
    k_all = [jnp.concatenate([kprev_ref[t], k_cur[t]], axis=0) for t in range(n_kv_tiles)]
    v_all = [jnp.concatenate([vprev_ref[t], v_cur[t]], axis=0) for t in range(n_kv_tiles)]
    for t in range(n_kv_tiles):
        kprev_ref[t] = k_cur[t]
        vprev_ref[t] = v_cur[t]

    lane2 = lax.broadcasted_iota(jnp.int32, (2 * w, LANES), 1)
    low2 = lane2 < hd
    c_idx = lax.broadcasted_iota(jnp.int32, (w, 2 * w), 0)
    r_idx = lax.broadcasted_iota(jnp.int32, (w, 2 * w), 1)
    rel = c_idx + w - r_idx
    valid = (rel >= 0) & (rel < w) & ((j - 1) * w + r_idx >= 0)
    scale = hd ** -0.5

    k_half, v_half = [], []
    for kvh in range(SW_KV_HEADS):
        kt, vt = k_all[kvh // 2], v_all[kvh // 2]
        kt_sw = pltpu.roll(kt, hd, 1)
        vt_sw = pltpu.roll(vt, hd, 1)
        lo_src, hi_src = ((kt, vt), (kt_sw, vt_sw)) if kvh % 2 == 0 else ((kt_sw, vt_sw), (kt, vt))
        k_half.append((jnp.where(low2, lo_src[0], 0.0).astype(BF16), jnp.where(low2, 0.0, hi_src[0]).astype(BF16)))
        v_half.append((jnp.where(low2, lo_src[1], 0.0).astype(BF16), jnp.where(low2, 0.0, hi_src[1]).astype(BF16)))
    n_pairs = SW_HEADS // 2
    qp = [(norm_rope(q_ref[:, t * LANES:(t + 1) * LANES].astype(F32), qn) * scale).astype(BF16)
          for t in range(n_pairs)]
    heads = range(SW_HEADS)
    s = [jnp.where(valid, lax.dot_general(qp[h // 2], k_half[h // SW_GROUP][h % 2], NT_DIMS,
                                          preferred_element_type=F32), -jnp.inf) for h in heads]
    m = [jnp.maximum(jnp.max(s[h], axis=-1, keepdims=True), sink_ref[h]) for h in heads]
    p = [jnp.exp(s[h] - m[h]) for h in heads]
    denom = [jnp.sum(p[h], axis=-1, keepdims=True) + jnp.exp(sink_ref[h] - m[h]) for h in heads]
    pv = [_dot(p[h].astype(BF16), v_half[h // SW_GROUP][h % 2]) / denom[h] for h in heads]
    for t in range(n_pairs):
        o_ref[:, t * LANES:(t + 1) * LANES] = (pv[2 * t] + pv[2 * t + 1]).astype(o_ref.dtype)


def swa_attention(q, kv, pos, q_norm, k_norm, sinks, batch, seq):
    n = batch * seq
    w = SW_WINDOW
    nb = seq // w
    hd = SW_HEAD_DIM
    lane = jnp.arange(LANES)
    inv_freq = ROPE_THETA ** (-jnp.arange(0, hd, 2, dtype=F32) / hd)
    invf = inv_freq[lane % (hd // 2)].reshape(1, LANES)
    sgn = jnp.where((lane % hd) < hd // 2, -1.0, 1.0).astype(F32).reshape(1, LANES)
    tile2 = lambda g: jnp.tile(g.astype(F32), LANES // hd).reshape(1, LANES)
    n_kv_tiles = SW_KV_HEADS * hd // LANES
    row = lambda b, jj, s: (b * nb + jj, 0)
    const = lambda b, jj, s: (0, 0)
    grid_spec = pltpu.PrefetchScalarGridSpec(
        num_scalar_prefetch=1,
        grid=(batch, nb),
        in_specs=[pl.BlockSpec((w, D_MODEL), row),
                  pl.BlockSpec((w, 2 * SW_KV_HEADS * hd), row),
                  pl.BlockSpec((w, 1), row),
                  pl.BlockSpec((1, LANES), const),
                  pl.BlockSpec((1, LANES), const),
                  pl.BlockSpec((1, LANES), const),
                  pl.BlockSpec((1, LANES), const)],
        out_specs=pl.BlockSpec((w, D_MODEL), row),
        scratch_shapes=[pltpu.VMEM((n_kv_tiles, w, LANES), F32),
                        pltpu.VMEM((n_kv_tiles, w, LANES), F32)],
    )
    return pl.pallas_call(
        _swa_body,
        name="swa_attention",
        out_shape=jax.ShapeDtypeStruct((n, D_MODEL), BF16),
        grid_spec=grid_spec,
        compiler_params=_params("parallel", "arbitrary"),
    )(sinks.astype(F32), q, kv, pos, tile2(q_norm), tile2(k_norm), invf, sgn)


def _gmlp_body(u_ref, v_ref, lg_ref, lbias_ref, ws_ref, bs_ref, o_ref, *, tb):
    c = GM_CHUNK
    gw = GM_WIDTH // GM_GROUPS
    row = lax.broadcasted_iota(jnp.int32, (c, c), 0)
    col = lax.broadcasted_iota(jnp.int32, (c, c), 1)
    causal = row >= col
    wcs = [jnp.where(causal, ws_ref[g], 0.0).astype(BF16) for g in range(GM_GROUPS)]
    bs = bs_ref[...]
    for r in range(0, tb, c):
        v = v_ref[r:r + c, :].astype(F32)
        mu = jnp.mean(v, axis=-1, keepdims=True)
        vc = v - mu
        var = jnp.mean(vc * vc, axis=-1, keepdims=True)
        vn = (vc * lax.rsqrt(var + EPS) * lg_ref[...] + lbias_ref[...]).astype(BF16)
        for g in range(GM_GROUPS):
            sl = slice(g * gw, (g + 1) * gw)
            mixed = _dot(wcs[g], vn[:, sl]) + bs[:, g:g + 1]
            o_ref[r:r + c, sl] = (u_ref[r:r + c, sl].astype(F32) * mixed).astype(o_ref.dtype)


def gmlp_spatial(z, ln_g, ln_b, w_s, b_s, tb=512):
    n = z.shape[0]
    d = GM_WIDTH
    tb = _row_tile(n, tb)
    const2 = lambda i: (0, 0)
    return pl.pallas_call(
        functools.partial(_gmlp_body, tb=tb),
        name="gmlp_spatial",
        out_shape=jax.ShapeDtypeStruct((n, d), BF16),
        grid=(n // tb,),
        in_specs=[pl.BlockSpec((tb, d), lambda i: (i, 0)),
                  pl.BlockSpec((tb, d), lambda i: (i, 1)),
                  pl.BlockSpec((1, d), const2),
                  pl.BlockSpec((1, d), const2),
                  pl.BlockSpec((GM_GROUPS, GM_CHUNK, GM_CHUNK), lambda i: (0, 0, 0)),
                  pl.BlockSpec((GM_CHUNK, GM_GROUPS), const2)],
        out_specs=pl.BlockSpec((tb, d), lambda i: (i, 0)),
        compiler_params=_params("parallel"),
    )(z, z, ln_g.reshape(1, d).astype(F32), ln_b.reshape(1, d).astype(F32),
      w_s.astype(F32), b_s.T.astype(F32))


def _gdn_body(qkv_ref, z_ref, ba_ref, cw_ref, alog_ref, dtb_ref, on_ref, o_ref,
              s_ref, x_ref, y_ref, *, tb):
    c = GD_CHUNK
    hd = GD_HEAD_DIM
    halo = 8
    nq = GD_QK_HEADS * hd
    it = pl.program_id(1)

    @pl.when(it == 0)
    def _():
        s_ref[...] = jnp.zeros_like(s_ref)
        x_ref[0:halo, :] = jnp.zeros((halo, GD_QKV), F32)

    @pl.when(it > 0)
    def _():
        x_ref[0:halo, :] = x_ref[tb:tb + halo, :]

    x_ref[halo:halo + tb, :] = qkv_ref[...].astype(F32)
    acc = None
    for jj in range(GD_CONV):
        start = halo - (GD_CONV - 1) + jj
        term = cw_ref[jj:jj + 1, :] * x_ref[start:start + tb, :]
        acc = term if acc is None else acc + term
    y_ref[...] = _silu(acc)
    for h in range(2 * GD_QK_HEADS):
        sl = slice(h * hd, (h + 1) * hd)
        a = y_ref[:, sl]
        a = a * lax.rsqrt(jnp.sum(a * a, axis=-1, keepdims=True) + EPS)
        if h < GD_QK_HEADS:
            a = a * (hd ** -0.5)
        y_ref[:, sl] = a

    row = lax.broadcasted_iota(jnp.int32, (c, c), 0)
    col = lax.broadcasted_iota(jnp.int32, (c, c), 1)
    tri = row >= col
    tri_strict = row > col
    tri_bf = tri.astype(BF16)
    eye = (row == col).astype(F32)
    lane = lax.broadcasted_iota(jnp.int32, (1, LANES), 1)
    neg_a = -jnp.exp(alog_ref[...])
    on = on_ref[...]

    def chunk(ci, carry):
        r0 = pl.multiple_of(ci * c, c)
        rows = pl.ds(r0, c)
        ba = ba_ref[rows, :]
        beta_all = jax.nn.sigmoid(ba)
        g_all = neg_a * jax.nn.softplus(ba + dtb_ref[...])
        g_all = jnp.where((lane >= GD_V_HEADS) & (lane < 2 * GD_V_HEADS), g_all, 0.0)
        gam_all = _dot_split(tri_bf, g_all)
        gam_rows = gam_all.T
        e_gam_all = jnp.exp(gam_all)
        gam_last_all = gam_all[c - 1:c, :]
        e_end_all = jnp.exp(gam_last_all - gam_all)
        d_end_all = jnp.exp(gam_last_all)
        heads = range(GD_V_HEADS)
        rep = GD_V_HEADS // GD_QK_HEADS
        q = [y_ref[rows, hk * hd:(hk + 1) * hd] for hk in range(GD_QK_HEADS)]
        k = [y_ref[rows, nq + hk * hd:nq + (hk + 1) * hd] for hk in range(GD_QK_HEADS)]
        k_bf = [a.astype(BF16) for a in k]
        kk = [lax.dot_general(a, a, NT_DIMS, preferred_element_type=F32) for a in k_bf]
        qk = [lax.dot_general(a.astype(BF16), b, NT_DIMS, preferred_element_type=F32)
              for a, b in zip(q, k_bf)]
        col = lambda a, hv: a[:, hv:hv + 1]
        gcol = lambda a, hv: a[:, GD_V_HEADS + hv:GD_V_HEADS + hv + 1]
        decay = [jnp.exp(jnp.where(tri, gcol(gam_all, hv) - gam_rows[GD_V_HEADS + hv:GD_V_HEADS + hv + 1, :],
                                   -jnp.inf)) for hv in heads]
        p = [-jnp.where(tri_strict, kk[hv // rep] * decay[hv] * col(beta_all, hv), 0.0) for hv in heads]
        rhs = [jnp.concatenate(
            [y_ref[rows, 2 * nq + hv * hd:2 * nq + (hv + 1) * hd] * col(beta_all, hv),
             k[hv // rep] * (col(beta_all, hv) * gcol(e_gam_all, hv))], axis=-1).astype(BF16) for hv in heads]
        u = [eye + a for a in p]
        p_bf = [a.astype(BF16) for a in p]
        n_steps = c.bit_length() - 1
        for step in range(1, n_steps):
            p_bf = [_dot(pb, pb).astype(BF16) for pb in p_bf]
            u = [a + _dot(a.astype(BF16), pb) for a, pb in zip(u, p_bf)]
        sol = [_dot(a.astype(BF16), b) for a, b in zip(u, rhs)]
        s_old = [s_ref[hv] for hv in heads]
        s_bf = [a.astype(BF16) for a in s_old]
        nv_bf = [(sol[hv][:, :hd] - _dot(sol[hv][:, hd:].astype(BF16), s_bf[hv])).astype(BF16) for hv in heads]
        o = [_dot((q[hv // rep] * gcol(e_gam_all, hv)).astype(BF16), s_bf[hv])
             + _dot((qk[hv // rep] * decay[hv]).astype(BF16), nv_bf[hv]) for hv in heads]
        s_new = [s_old[hv] * gcol(d_end_all, hv) + lax.dot_general(
            (k[hv // rep] * gcol(e_end_all, hv)).astype(BF16), nv_bf[hv], TN_DIMS,
            preferred_element_type=F32) for hv in heads]
        for hv in heads:
            s_ref[hv] = s_new[hv]
        outs = [_rms(o[hv], on) * _silu(z_ref[rows, hv * hd:(hv + 1) * hd].astype(F32)) for hv in heads]
        o_ref[rows, :] = jnp.concatenate(outs, axis=-1).astype(o_ref.dtype)
        return carry

    lax.fori_loop(0, tb // c, chunk, 0)


def gdn_recurrence(qkv, z, ba, conv_w, a_log, dt_bias, o_norm, batch, seq, tb=256):
    n = batch * seq
    tb = _row_tile(seq, tb)
    nt = seq // tb
    pad_heads = lambda a: jnp.zeros((1, LANES), F32).at[0, GD_V_HEADS:2 * GD_V_HEADS].set(a.astype(F32))
    row = lambda b, t: (b * nt + t, 0)
    const = lambda b, t: (0, 0)
    return pl.pallas_call(
        functools.partial(_gdn_body, tb=tb),
        name="gdn_recurrence",
        out_shape=jax.ShapeDtypeStruct((n, GD_Z), BF16),
        grid=(batch, nt),
        in_specs=[pl.BlockSpec((tb, GD_QKV), row),
                  pl.BlockSpec((tb, GD_Z), row),
                  pl.BlockSpec((tb, LANES), row),
                  pl.BlockSpec((GD_CONV, GD_QKV), const),
                  pl.BlockSpec((1, LANES), const),
                  pl.BlockSpec((1, LANES), const),
                  pl.BlockSpec((1, GD_HEAD_DIM), const)],
        out_specs=pl.BlockSpec((tb, GD_Z), row),
        scratch_shapes=[pltpu.VMEM((GD_V_HEADS, GD_HEAD_DIM, GD_HEAD_DIM), F32),
                        pltpu.VMEM((tb + 16, GD_QKV), F32),
                        pltpu.VMEM((tb, GD_QKV), F32)],
        compiler_params=_params("parallel", "arbitrary"),
    )(qkv, z, ba, conv_w.astype(F32), pad_heads(a_log), pad_heads(dt_bias),
      o_norm.reshape(1, GD_HEAD_DIM).astype(F32))


def _router_body(x_ref, g_ref, wr_ref, o_ref):
    h = _rms(x_ref[...], g_ref[...])
    w = wr_ref[...]
    h_hi = h.astype(BF16)
    h_lo = (h - h_hi.astype(F32)).astype(BF16)
    w_hi = w.astype(BF16)
    w_lo = (w - w_hi.astype(F32)).astype(BF16)
    logits = _dot(h_hi, w_hi) + (_dot(h_lo, w_hi) + _dot(h_hi, w_lo))
    lane = lax.broadcasted_iota(jnp.int32, logits.shape, 1)
    logits = jnp.where(lane < N_EXPERTS, logits, -jnp.inf)
    m1 = jnp.max(logits, axis=-1, keepdims=True)
    i1 = jnp.min(jnp.where(logits == m1, lane, LANES), axis=-1, keepdims=True)
    rest = jnp.where(lane == i1, -jnp.inf, logits)
    m2 = jnp.max(rest, axis=-1, keepdims=True)
    i2 = jnp.min(jnp.where(rest == m2, lane, LANES), axis=-1, keepdims=True)
    e2 = jnp.exp(m2 - m1)
    w1 = 1.0 / (1.0 + e2)
    w2 = e2 / (1.0 + e2)
    out = jnp.where(lane == 0, i1.astype(F32),
                    jnp.where(lane == 1, i2.astype(F32),
                              jnp.where(lane == 2, w1, jnp.where(lane == 3, w2, 0.0))))
    o_ref[...] = out


def moe_router(x, g, router, tm=512):
    n, d = x.shape
    tm = _row_tile(n, tm)
    wr = jnp.zeros((d, LANES), F32).at[:, :N_EXPERTS].set(router.astype(F32))
    return pl.pallas_call(
        _router_body,
        name="moe_router",
        out_shape=jax.ShapeDtypeStruct((n, LANES), F32),
        grid=(n // tm,),
        in_specs=[pl.BlockSpec((tm, d), lambda i: (i, 0)),
                  pl.BlockSpec((1, d), lambda i: (0, 0)),
                  pl.BlockSpec((d, LANES), lambda i: (0, 0))],
        out_specs=pl.BlockSpec((tm, LANES), lambda i: (i, 0)),
        compiler_params=_params("parallel"),
    )(x, g.reshape(1, d).astype(F32), wr)


def _row_copy(src_ref, src_row, dst_ref, dst_row, sem):
    return pltpu.make_async_copy(src_ref.at[pl.ds(src_row, 1), :], dst_ref.at[pl.ds(dst_row, 1), :], sem)


def _dispatch_body(ends_ref, padded_ref, pos_ref, x_ref, xs_ref, zero_ref, sem, zero_sem, *, rt, tm):
    @pl.when(pl.program_id(0) == 0)
    def _():
        zero_ref[...] = jnp.zeros_like(zero_ref)

        def zero_tile(row0):
            row0 = row0 if isinstance(row0, int) else pl.multiple_of(row0, tm)
            cp = pltpu.make_async_copy(zero_ref, xs_ref.at[pl.ds(row0, tm), :], zero_sem)
            cp.start()
            cp.wait()

        for e in range(N_EXPERTS):
            pl.when(padded_ref[e] > 0)(functools.partial(zero_tile, ends_ref[e] - tm))
        n_rows = xs_ref.shape[0]
        for back in range(1, N_EXPERTS + 1):
            row0 = n_rows - back * tm
            pl.when(row0 >= ends_ref[N_EXPERTS - 1])(functools.partial(zero_tile, row0))

    def start(g, carry):
        r0 = pl.multiple_of(g * SUBLANES, SUBLANES)
        for u in range(SUBLANES):
            for k in range(TOP_K):
                _row_copy(x_ref, r0 + u, xs_ref, pos_ref[0, 0, TOP_K * (r0 + u) + k], sem).start()
        return carry

    lax.fori_loop(0, rt // SUBLANES, start, 0)
    all_rows = xs_ref.at[pl.ds(0, TOP_K * rt), :]
    pltpu.make_async_copy(all_rows, all_rows, sem).wait()


def moe_dispatch(x, pos, ends, padded, n_rows, tm, rt=256):
    n, d = x.shape
    rt = _row_tile(n, rt)
    grid_spec = pltpu.PrefetchScalarGridSpec(
        num_scalar_prefetch=2,
        grid=(n // rt,),
        in_specs=[pl.BlockSpec((1, 1, TOP_K * rt), lambda i, en, pa: (i, 0, 0), memory_space=pltpu.SMEM),
                  pl.BlockSpec((rt, d), lambda i, en, pa: (i, 0))],
        out_specs=pl.BlockSpec(memory_space=pl.ANY),
        scratch_shapes=[pltpu.VMEM((tm, d), F32), pltpu.SemaphoreType.DMA(()),
                        pltpu.SemaphoreType.DMA(())],
    )
    return pl.pallas_call(
        functools.partial(_dispatch_body, rt=rt, tm=tm),
        name="moe_dispatch",
        out_shape=jax.ShapeDtypeStruct((n_rows, d), F32),
        grid_spec=grid_spec,
        compiler_params=_params("arbitrary"),
    )(ends, padded, pos.reshape(n // rt, 1, TOP_K * rt), x)


def _moe_up_body(te_ref, nt_ref, xs_ref, g_ref, wg_ref, wu_ref, o_ref, *, chunk):
    del te_ref
    i = pl.program_id(1)

    @pl.when(i < nt_ref[0])
    def _():
        h = _rms(xs_ref[...], g_ref[...]).astype(BF16)
        f = o_ref.shape[1]
        for c in range(0, f, chunk):
            a = _dot(h, wg_ref[0, :, c:c + chunk])
            b = _dot(h, wu_ref[0, :, c:c + chunk])
            o_ref[:, c:c + chunk] = (_silu(a) * b).astype(o_ref.dtype)

    @pl.when(i >= nt_ref[0])
    def _():
        o_ref[...] = jnp.zeros_like(o_ref)


def moe_up(xs, g, wg, wu, tile_expert, n_tiles_used, tm, fsplit=2, chunk=256):
    p, d = xs.shape
    f = wg.shape[2]
    fb = f // fsplit
    assert fb % chunk == 0 and p % tm == 0
    grid_spec = pltpu.PrefetchScalarGridSpec(
        num_scalar_prefetch=2,
        grid=(fsplit, p // tm),
        in_specs=[pl.BlockSpec((tm, d), lambda j, i, te, nt: (jnp.minimum(i, nt[0] - 1), 0)),
                  pl.BlockSpec((1, d), lambda j, i, te, nt: (0, 0)),
                  pl.BlockSpec((1, d, fb), lambda j, i, te, nt: (te[i], 0, j)),
                  pl.BlockSpec((1, d, fb), lambda j, i, te, nt: (te[i], 0, j))],
        out_specs=pl.BlockSpec((tm, fb), lambda j, i, te, nt: (i, j)),
    )
    return pl.pallas_call(
        functools.partial(_moe_up_body, chunk=chunk),
        name="moe_up",
        out_shape=jax.ShapeDtypeStruct((p, f), BF16),
        grid_spec=grid_spec,
        compiler_params=_params("arbitrary", "arbitrary"),
    )(tile_expert, n_tiles_used, xs, g.reshape(1, d).astype(F32), wg, wu)


def _moe_down_body(te_ref, nt_ref, a_ref, wd_ref, o_ref):
    del te_ref
    i = pl.program_id(0)

    @pl.when(i < nt_ref[0])
    def _():
        o_ref[...] = _dot(a_ref[...], wd_ref[0])

    @pl.when(i >= nt_ref[0])
    def _():
        o_ref[...] = jnp.zeros_like(o_ref)


def moe_down(act, wd, tile_expert, n_tiles_used, tm):
    p, f = act.shape
    d = wd.shape[2]
    grid_spec = pltpu.PrefetchScalarGridSpec(
        num_scalar_prefetch=2,
        grid=(p // tm,),
        in_specs=[pl.BlockSpec((tm, f), lambda i, te, nt: (i, 0)),
                  pl.BlockSpec((1, f, d), lambda i, te, nt: (te[i], 0, 0))],
        out_specs=pl.BlockSpec((tm, d), lambda i, te, nt: (i, 0)),
    )
    return pl.pallas_call(
        _moe_down_body,
        name="moe_down",
        out_shape=jax.ShapeDtypeStruct((p, d), F32),
        grid_spec=grid_spec,
        compiler_params=_params("arbitrary"),
    )(tile_expert, n_tiles_used, act, wd)


def _combine_body(pos_ref, pos_next_ref, x_ref, r_ref, ys_ref, o_ref, buf_ref, sem, *, rt):
    i = pl.program_id(0)
    slot = i % 2

    def gather(p_ref, s):
        def start(g, carry):
            r0 = pl.multiple_of(g * SUBLANES, SUBLANES)
            for u in range(SUBLANES):
                for k in range(TOP_K):
                    _row_copy(ys_ref, p_ref[0, 0, TOP_K * (r0 + u) + k], buf_ref.at[s, k], r0 + u,
                              sem.at[s]).start()
            return carry

        lax.fori_loop(0, rt // SUBLANES, start, 0)

    @pl.when(i == 0)
    def _():
        gather(pos_ref, 0)

    @pl.when(i + 1 < pl.num_programs(0))
    def _():
        gather(pos_next_ref, 1 - slot)

    for k in range(TOP_K):
        pltpu.make_async_copy(ys_ref.at[pl.ds(0, rt), :], buf_ref.at[slot, k], sem.at[slot]).wait()
    r = r_ref[...]
    o_ref[...] = x_ref[...] + r[:, 2:3] * buf_ref[slot, 0] + r[:, 3:4] * buf_ref[slot, 1]


def moe_combine(x, route, ys, pos, rt=256):
    n, d = x.shape
    rt = _row_tile(n, rt)
    steps = n // rt
    pos3 = pos.reshape(steps, 1, TOP_K * rt)
    grid_spec = pltpu.PrefetchScalarGridSpec(
        num_scalar_prefetch=0,
        grid=(steps,),
        in_specs=[pl.BlockSpec((1, 1, TOP_K * rt), lambda i: (i, 0, 0), memory_space=pltpu.SMEM),
                  pl.BlockSpec((1, 1, TOP_K * rt), lambda i: (jnp.minimum(i + 1, steps - 1), 0, 0),
                               memory_space=pltpu.SMEM),
                  pl.BlockSpec((rt, d), lambda i: (i, 0)),
                  pl.BlockSpec((rt, LANES), lambda i: (i, 0)),
                  pl.BlockSpec(memory_space=pl.ANY)],
        out_specs=pl.BlockSpec((rt, d), lambda i: (i, 0)),
        scratch_shapes=[pltpu.VMEM((2, TOP_K, rt, d), F32), pltpu.SemaphoreType.DMA((2,))],
    )
    return pl.pallas_call(
        functools.partial(_combine_body, rt=rt),
        name="moe_combine",
        out_shape=jax.ShapeDtypeStruct((n, d), F32),
        grid_spec=grid_spec,
        compiler_params=_params("arbitrary"),
    )(pos3, pos3, x, route, ys)


def moe_ffn(x, g, router, wg, wu, wd, tm=512):
    n, d = x.shape
    route = moe_router(x, g, router)
    expert = route[:, :TOP_K].astype(jnp.int32).reshape(n * TOP_K)
    onehot = (expert[:, None] == jnp.arange(N_EXPERTS, dtype=jnp.int32)[None, :]).astype(jnp.int32)
    csum = jnp.cumsum(onehot, axis=0)
    rank = jnp.sum((csum - onehot) * onehot, axis=1)
    counts = csum[-1]
    padded = ((counts + tm - 1) // tm) * tm
    ends = jnp.cumsum(padded)
    starts = ends - padded
    pos = (jnp.sum(starts[None, :] * onehot, axis=1) + rank).astype(jnp.int32)
    n_rows = n * TOP_K + N_EXPERTS * tm
    n_tiles = n_rows // tm
    tile_start = jnp.arange(n_tiles, dtype=jnp.int32) * tm
    tile_expert = jnp.minimum(jnp.sum((tile_start[:, None] >= ends[None, :]).astype(jnp.int32), axis=1),
                              N_EXPERTS - 1).astype(jnp.int32)
    n_tiles_used = (ends[-1:] // tm).astype(jnp.int32)

    xs = moe_dispatch(x, pos, ends.astype(jnp.int32), padded.astype(jnp.int32), n_rows, tm)
    act = moe_up(xs, g, wg, wu, tile_expert, n_tiles_used, tm)
    ys = moe_down(act, wd, tile_expert, n_tiles_used, tm)
    return moe_combine(x, route, ys, pos)


def hgrn2_layer(x, g, w_in, o_norm, w_out, lower_bound, batch, seq):
    (qfig,) = norm_matmul(x, g, w_in.astype(BF16), [(4 * D_MODEL, BF16)])
    o = hgrn_recurrence(qfig, lower_bound, o_norm, batch, seq)
    return matmul_residual(o, w_out.astype(BF16), x)


def swa_layer(x, g, positions, w_in, q_norm, k_norm, sinks, w_out, batch, seq):
    q, kv = norm_matmul(x, g, w_in.astype(BF16),
                        [(D_MODEL, BF16), (2 * SW_KV_HEADS * SW_HEAD_DIM, BF16)])
    pos = positions.astype(F32).reshape(batch * seq, 1)
    o = swa_attention(q, kv, pos, q_norm, k_norm, sinks, batch, seq)
    return matmul_residual(o, w_out.astype(BF16), x)


def gmlp_layer(x, g, w_in, b_in, ln_g, ln_b, w_s, b_s, w_out):
    (z,) = norm_matmul(x, g, w_in.astype(BF16), [(2 * GM_WIDTH, BF16)], bias=b_in, act="gelu")
    o = gmlp_spatial(z, ln_g, ln_b, w_s, b_s)
    return matmul_residual(o, w_out.astype(BF16), x)


def gdn_layer(x, g, w_in, conv_w, a_log, dt_bias, o_norm, w_out, batch, seq):
    d = x.shape[1]
    n_small = w_in.shape[1] - GD_QKV - GD_Z
    w_main = w_in[:, :GD_QKV + GD_Z].astype(BF16)
    w_small = jnp.zeros((d, LANES), F32).at[:, :n_small].set(w_in[:, GD_QKV + GD_Z:].astype(F32))
    w_all = jnp.concatenate([w_main, w_small.astype(BF16)], axis=1)
    qkv, z, ba = norm_matmul(x, g, w_all, [(GD_QKV, BF16), (GD_Z, BF16), (LANES, F32)])
    o = gdn_recurrence(qkv, z, ba, conv_w, a_log, dt_bias, o_norm, batch, seq)
    return matmul_residual(o, w_out.astype(BF16), x)


def dense_ffn(x, g, w_gate, w_up, w_down):
    act = swiglu_up(x, g, w_gate.astype(BF16), w_up.astype(BF16))
    return matmul_residual(act, w_down.astype(BF16), x)


def kernel(x, positions, mix_norm, ffn_norm, hgrn_lb_logits, hgrn_w_in, hgrn_o_norm, hgrn_w_out,
           swa_w_in, swa_q_norm, swa_k_norm, swa_sinks, swa_w_out,
           gmlp_w_in, gmlp_b_in, gmlp_v_ln_g, gmlp_v_ln_b, gmlp_w_s, gmlp_b_s, gmlp_w_out,
           gdn_w_in, gdn_conv_w, gdn_a_log, gdn_dt_bias, gdn_o_norm, gdn_w_out,
           dense_w_gate, dense_w_up, dense_w_down,
           moe_router, moe_w_gate, moe_w_up, moe_w_down):
    batch, seq, d = x.shape
    depth = mix_norm.shape[0]
    lower_bounds = jnp.cumsum(jax.nn.softmax(hgrn_lb_logits.astype(F32), axis=0), axis=0)
    h = x.reshape(batch * seq, d)
    for i in range(depth):
        kind, j = i % 4, i // 4
        if kind == 0:
            h = hgrn2_layer(h, mix_norm[i], hgrn_w_in[j], hgrn_o_norm[j], hgrn_w_out[j],
                            lower_bounds[i], batch, seq)
        elif kind == 1:
            h = swa_layer(h, mix_norm[i], positions, swa_w_in[j], swa_q_norm[j], swa_k_norm[j],
                          swa_sinks[j], swa_w_out[j], batch, seq)
        elif kind == 2:
            h = gmlp_layer(h, mix_norm[i], gmlp_w_in[j], gmlp_b_in[j], gmlp_v_ln_g[j], gmlp_v_ln_b[j],
                           gmlp_w_s[j], gmlp_b_s[j], gmlp_w_out[j])
        else:
            h = gdn_layer(h, mix_norm[i], gdn_w_in[j], gdn_conv_w[j], gdn_a_log[j], gdn_dt_bias[j],
                          gdn_o_norm[j], gdn_w_out[j], batch, seq)
        if i % 2 == 0:
            h = dense_ffn(h, ffn_norm[i], dense_w_gate[i // 2], dense_w_up[i // 2], dense_w_down[i // 2])
        else:
            h = moe_ffn(h, ffn_norm[i], moe_router[i // 2], moe_w_gate[i // 2].astype(BF16),
                        moe_w_up[i // 2].astype(BF16), moe_w_down[i // 2].astype(BF16))
    return h.reshape(batch, seq, d)
```

```python
import functools

import jax
import jax.numpy as jnp
from jax import lax
from jax.experimental import pallas as pl
from jax.experimental.pallas import tpu as pltpu

F32 = jnp.float32
BF16 = jnp.bfloat16
EPS = 1e-6

D_MODEL = 1024
LANES = 128
SUBLANES = 8
HG_HEAD_DIM = 128
HG_HEADS = D_MODEL // HG_HEAD_DIM
HG_CHUNK = 32
SW_HEAD_DIM = 64
SW_HEADS = D_MODEL // SW_HEAD_DIM
SW_KV_HEADS = 4
SW_GROUP = SW_HEADS // SW_KV_HEADS
SW_WINDOW = 128
ROPE_THETA = 10000.0
GM_WIDTH = D_MODEL
GM_GROUPS = 8
GM_CHUNK = 128
GD_HEAD_DIM = 128
GD_QK_HEADS = D_MODEL // GD_HEAD_DIM
GD_V_HEADS = 2 * GD_QK_HEADS
GD_CONV = 4
GD_CHUNK = 64
GD_QKV = 2 * GD_QK_HEADS * GD_HEAD_DIM + GD_V_HEADS * GD_HEAD_DIM
GD_Z = GD_V_HEADS * GD_HEAD_DIM
GD_PAR_CHUNKS = 4
N_EXPERTS = 8
TOP_K = 2

V7X_VMEM_LIMIT_BYTES = 56 * 1024 * 1024

NT_DIMS = (((1,), (1,)), ((), ()))
TN_DIMS = (((0,), (0,)), ((), ()))


def _params(*sem):
    return pltpu.CompilerParams(dimension_semantics=sem, vmem_limit_bytes=V7X_VMEM_LIMIT_BYTES)


def _dot(a, b):
    return jnp.dot(a, b, preferred_element_type=F32)


def _dot_split(a_bf, x):
    hi = x.astype(BF16)
    lo = (x - hi.astype(F32)).astype(BF16)
    return _dot(a_bf, hi) + _dot(a_bf, lo)


def _rms(x, g):
    return x * lax.rsqrt(jnp.mean(x * x, axis=-1, keepdims=True) + EPS) * g


def _silu(x):
    return x * jax.nn.sigmoid(x)


def _row_tile(n, want):
    t = min(n, want)
    assert n % t == 0, (n, t)
    return t


def _norm_mm_body(x_ref, g_ref, w_ref, *rest, widths, chunk, use_bias, act):
    if use_bias:
        b_ref, out_refs = rest[0], rest[1:]
    else:
        b_ref, out_refs = None, rest
    h = _rms(x_ref[...], g_ref[...]).astype(BF16)
    off = 0
    for o_ref, width in zip(out_refs, widths):
        for c in range(0, width, chunk):
            cw = min(chunk, width - c)
            y = _dot(h, w_ref[:, off + c:off + c + cw])
            if use_bias:
                y = y + b_ref[:, off + c:off + c + cw]
            if act == "gelu":
                y = 0.5 * y * (1.0 + lax.erf(y * (0.5 ** 0.5)))
            o_ref[:, c:c + cw] = y.astype(o_ref.dtype)
        off += width


def norm_matmul(x, g, w, outs, bias=None, act=None, tm=512, chunk=512):
    n, d = x.shape
    m = w.shape[1]
    widths = tuple(o[0] for o in outs)
    assert sum(widths) == m
    tm = _row_tile(n, tm)
    in_specs = [pl.BlockSpec((tm, d), lambda i: (i, 0)),
                pl.BlockSpec((1, d), lambda i: (0, 0)),
                pl.BlockSpec((d, m), lambda i: (0, 0))]
    args = [x, g.reshape(1, d).astype(F32), w]
    if bias is not None:
        in_specs.append(pl.BlockSpec((1, m), lambda i: (0, 0)))
        args.append(bias.reshape(1, m).astype(F32))
    body = functools.partial(_norm_mm_body, widths=widths, chunk=chunk,
                             use_bias=bias is not None, act=act)
    return pl.pallas_call(
        body,
        name="norm_matmul",
        out_shape=[jax.ShapeDtypeStruct((n, wd), dt) for wd, dt in outs],
        grid=(n // tm,),
        in_specs=in_specs,
        out_specs=[pl.BlockSpec((tm, wd), lambda i: (i, 0)) for wd, _ in outs],
        compiler_params=_params("parallel"),
    )(*args)


def _mm_res_body(a_ref, w_ref, x_ref, o_ref):
    o_ref[...] = x_ref[...] + _dot(a_ref[...], w_ref[...])


def matmul_residual(a, w, x, tm=512):
    n, k = a.shape
    d = w.shape[1]
    tm = _row_tile(n, tm)
    return pl.pallas_call(
        _mm_res_body,
        name="matmul_residual",
        out_shape=jax.ShapeDtypeStruct((n, d), F32),
        grid=(n // tm,),
        in_specs=[pl.BlockSpec((tm, k), lambda i: (i, 0)),
                  pl.BlockSpec((k, d), lambda i: (0, 0)),
                  pl.BlockSpec((tm, d), lambda i: (i, 0))],
        out_specs=pl.BlockSpec((tm, d), lambda i: (i, 0)),
        compiler_params=_params("parallel"),
    )(a, w, x)


def _swiglu_up_body(x_ref, g_ref, wg_ref, wu_ref, o_ref, *, chunk):
    h = _rms(x_ref[...], g_ref[...]).astype(BF16)
    f = o_ref.shape[1]
    for c in range(0, f, chunk):
        a = _dot(h, wg_ref[:, c:c + chunk])
        b = _dot(h, wu_ref[:, c:c + chunk])
        o_ref[:, c:c + chunk] = (_silu(a) * b).astype(o_ref.dtype)


def swiglu_up(x, g, wg, wu, tm=512, chunk=256):
    n, d = x.shape
    f = wg.shape[1]
    assert f % chunk == 0
    tm = _row_tile(n, tm)
    return pl.pallas_call(
        functools.partial(_swiglu_up_body, chunk=chunk),
        name="swiglu_up",
        out_shape=jax.ShapeDtypeStruct((n, f), BF16),
        grid=(n // tm,),
        in_specs=[pl.BlockSpec((tm, d), lambda i: (i, 0)),
                  pl.BlockSpec((1, d), lambda i: (0, 0)),
                  pl.BlockSpec((d, f), lambda i: (0, 0)),
                  pl.BlockSpec((d, f), lambda i: (0, 0))],
        out_specs=pl.BlockSpec((tm, f), lambda i: (i, 0)),
        compiler_params=_params("parallel"),
    )(x, g.reshape(1, d).astype(F32), wg, wu)


def _hgrn_body(q_ref, f_ref, i_ref, g_ref, lb_ref, on_ref, o_ref, s_ref, *, tb):
    c32 = HG_CHUNK
    hd = HG_HEAD_DIM

    @pl.when(pl.program_id(1) == 0)
    def _():
        s_ref[...] = jnp.zeros_like(s_ref)

    lb = lb_ref[...]
    on = on_ref[...]
    row = lax.broadcasted_iota(jnp.int32, (c32, c32), 0)
    col = lax.broadcasted_iota(jnp.int32, (c32, c32), 1)
    causal = row >= col
    tri_bf = causal.astype(BF16)

    def chunk(c, carry):
        r0 = pl.multiple_of(c * c32, c32)
        rows = pl.ds(r0, c32)
        forget = lb + (1.0 - lb) * jax.nn.sigmoid(f_ref[rows, :].astype(F32))
        b = _dot_split(tri_bf, jnp.log(forget))
        b_mid = b[c32 // 2 - 1:c32 // 2, :]
        b_last = b[c32 - 1:c32, :]
        qs = _silu(q_ref[rows, :].astype(F32))
        k = 1.0 - forget
        v_bf = i_ref[rows, :]
        qa = (qs * jnp.exp(b - b_mid)).astype(BF16)
        ka = (k * jnp.exp(b_mid - b)).astype(BF16)
        q_in = (qs * jnp.exp(b)).astype(BF16)
        k_end = (k * jnp.exp(b_last - b)).astype(BF16)
        d_end = jnp.exp(b_last)
        d_cols = jnp.concatenate([d_end[:, h * hd:(h + 1) * hd] for h in range(HG_HEADS)], axis=0).T
        gate = _silu(g_ref[rows, :].astype(F32))
        heads = range(HG_HEADS)
        sls = [slice(h * hd, (h + 1) * hd) for h in heads]
        s_old = [s_ref[h] for h in heads]
        sc = [lax.dot_general(qa[:, sl], ka[:, sl], NT_DIMS, preferred_element_type=F32) for sl in sls]
        inter = [_dot(q_in[:, sl], s.astype(BF16)) for sl, s in zip(sls, s_old)]
        upd = [lax.dot_general(k_end[:, sl], v_bf[:, sl], TN_DIMS, preferred_element_type=F32) for sl in sls]
        sc_bf = [jnp.where(causal, a, 0.0).astype(BF16) for a in sc]
        o = [_dot(a, v_bf[:, sl]) + b for a, sl, b in zip(sc_bf, sls, inter)]
        for h in heads:
            s_ref[h] = s_old[h] * d_cols[:, h:h + 1] + upd[h]
        outs = [_rms(a, on[:, sl]) for a, sl in zip(o, sls)]
        o_ref[rows, :] = (jnp.concatenate(outs, axis=-1) * gate).astype(o_ref.dtype)
        return carry

    lax.fori_loop(0, tb // c32, chunk, 0, unroll=8)


def hgrn_recurrence(qfig, lb, o_norm, batch, seq, tb=256):
    n = batch * seq
    d = D_MODEL
    tb = _row_tile(seq, tb)
    nt = seq // tb
    col_spec = lambda j: pl.BlockSpec((tb, d), lambda b, t, j=j: (b * nt + t, j))
    return pl.pallas_call(
        functools.partial(_hgrn_body, tb=tb),
        name="hgrn_recurrence",
        out_shape=jax.ShapeDtypeStruct((n, d), BF16),
        grid=(batch, nt),
        in_specs=[col_spec(0), col_spec(1), col_spec(2), col_spec(3),
                  pl.BlockSpec((1, d), lambda b, t: (0, 0)),
                  pl.BlockSpec((1, d), lambda b, t: (0, 0))],
        out_specs=pl.BlockSpec((tb, d), lambda b, t: (b * nt + t, 0)),
        scratch_shapes=[pltpu.VMEM((HG_HEADS, HG_HEAD_DIM, HG_HEAD_DIM), F32)],
        compiler_params=_params("parallel", "arbitrary"),
    )(qfig, qfig, qfig, qfig, lb.reshape(1, d).astype(F32), o_norm.reshape(1, d).astype(F32))


def _swa_body(sink_ref, q_ref, kv_ref, pos_ref, qn_ref, kn_ref, invf_ref, sgn_ref, o_ref,
              kprev_ref, vprev_ref):
    w = SW_WINDOW
    hd = SW_HEAD_DIM
    j = pl.program_id(1)

    @pl.when(j == 0)
    def _():
        kprev_ref[...] = jnp.zeros_like(kprev_ref)
        vprev_ref[...] = jnp.zeros_like(vprev_ref)

    lane = lax.broadcasted_iota(jnp.int32, (w, LANES), 1)
    first_half = (lane % hd) < (hd // 2)
    low_head = lane < hd
    ang = pos_ref[...] * invf_ref[...]
    cos = jnp.cos(ang)
    sin_signed = jnp.sin(ang) * sgn_ref[...]
    gi = lax.broadcasted_iota(jnp.int32, (LANES, LANES), 0) // hd
    gj = lax.broadcasted_iota(jnp.int32, (LANES, LANES), 1) // hd
    head_mean = jnp.where(gi == gj, 1.0 / hd, 0.0).astype(BF16)

    def norm_rope(x, gain):
        xx = x * x
        hi = xx.astype(BF16)
        lo = (xx - hi.astype(F32)).astype(BF16)
        ms = _dot(hi, head_mean) + _dot(lo, head_mean)
        x = x * lax.rsqrt(ms + EPS) * gain
        partner = jnp.where(first_half, pltpu.roll(x, LANES - hd // 2, 1), pltpu.roll(x, hd // 2, 1))
        return x * cos + partner * sin_signed

    qn = qn_ref[...]
    kn = kn_ref[...]
    kv = kv_ref[...].astype(F32)
    n_kv_tiles = SW_KV_HEADS * hd // LANES
    k_cur = [norm_rope(kv[:, t * LANES:(t + 1) * LANES], kn) for t in range(n_kv_tiles)]
    v_cur = [kv[:, (n_kv_tiles + t) * LANES:(n_kv_tiles + t + 1) * LANES] for t in range(n_kv_tiles)]
    k_all = [jnp.concatenate([kprev_ref[t], k_cur[t]], axis=0) for t in range(n_kv_tiles)]
    v_all = [jnp.concatenate([vprev_ref[t], v_cur[t]], axis=0) for t in range(n_kv_tiles)]
    for t in range(n_kv_tiles):
        kprev_ref[t] = k_cur[t]
        vprev_ref[t] = v_cur[t]

    lane2 = lax.broadcasted_iota(jnp.int32, (2 * w, LANES), 1)
    low2 = lane2 < hd
    c_idx = lax.broadcasted_iota(jnp.int32, (w, 2 * w), 0)
    r_idx = lax.broadcasted_iota(jnp.int32, (w, 2 * w), 1)
    rel = c_idx + w - r_idx
    valid = (rel >= 0) & (rel < w) & ((j - 1) * w + r_idx >= 0)
    scale = hd ** -0.5

    k_half, v_half = [], []
    for kvh in range(SW_KV_HEADS):
        kt, vt = k_all[kvh // 2], v_all[kvh // 2]
        kt_sw = pltpu.roll(kt, hd, 1)
        vt_sw = pltpu.roll(vt, hd, 1)
        lo_src, hi_src = ((kt, vt), (kt_sw, vt_sw)) if kvh % 2 == 0 else ((kt_sw, vt_sw), (kt, vt))
        k_half.append((jnp.where(low2, lo_src[0], 0.0).astype(BF16), jnp.where(low2, 0.0, hi_src[0]).astype(BF16)))
        v_half.append((jnp.where(low2, lo_src[1], 0.0).astype(BF16), jnp.where(low2, 0.0, hi_src[1]).astype(BF16)))
    n_pairs = SW_HEADS // 2
    qp = [(norm_rope(q_ref[:, t * LANES:(t + 1) * LANES].astype(F32), qn) * scale).astype(BF16)
          for t in range(n_pairs)]
    heads = range(SW_HEADS)
    s = [jnp.where(valid, lax.dot_general(qp[h // 2], k_half[h // SW_GROUP][h % 2], NT_DIMS,
                                          preferred_element_type=F32), -jnp.inf) for h in heads]
    m = [jnp.maximum(jnp.max(s[h], axis=-1, keepdims=True), sink_ref[h]) for h in heads]
    p = [jnp.exp(s[h] - m[h]) for h in heads]
    denom = [jnp.sum(p[h], axis=-1, keepdims=True) + jnp.exp(sink_ref[h] - m[h]) for h in heads]
    pv = [_dot(p[h].astype(BF16), v_half[h // SW_GROUP][h % 2]) / denom[h] for h in heads]
    for t in range(n_pairs):
        o_ref[:, t * LANES:(t + 1) * LANES] = (pv[2 * t] + pv[2 * t + 1]).astype(o_ref.dtype)


def swa_attention(q, kv, pos, q_norm, k_norm, sinks, batch, seq):
    n = batch * seq
    w = SW_WINDOW
    nb = seq // w
    hd = SW_HEAD_DIM
    lane = jnp.arange(LANES)
    inv_freq = ROPE_THETA ** (-jnp.arange(0, hd, 2, dtype=F32) / hd)
    invf = inv_freq[lane % (hd // 2)].reshape(1, LANES)
    sgn = jnp.where((lane % hd) < hd // 2, -1.0, 1.0).astype(F32).reshape(1, LANES)
    tile2 = lambda g: jnp.tile(g.astype(F32), LANES // hd).reshape(1, LANES)
    n_kv_tiles = SW_KV_HEADS * hd // LANES
    row = lambda b, jj, s: (b * nb + jj, 0)
    const = lambda b, jj, s: (0, 0)
    grid_spec = pltpu.PrefetchScalarGridSpec(
        num_scalar_prefetch=1,
        grid=(batch, nb),
        in_specs=[pl.BlockSpec((w, D_MODEL), row),
                  pl.BlockSpec((w, 2 * SW_KV_HEADS * hd), row),
                  pl.BlockSpec((w, 1), row),
                  pl.BlockSpec((1, LANES), const),
                  pl.BlockSpec((1, LANES), const),
                  pl.BlockSpec((1, LANES), const),
                  pl.BlockSpec((1, LANES), const)],
        out_specs=pl.BlockSpec((w, D_MODEL), row),
        scratch_shapes=[pltpu.VMEM((n_kv_tiles, w, LANES), F32),
                        pltpu.VMEM((n_kv_tiles, w, LANES), F32)],
    )
    return pl.pallas_call(
        _swa_body,
        name="swa_attention",
        out_shape=jax.ShapeDtypeStruct((n, D_MODEL), BF16),
        grid_spec=grid_spec,
        compiler_params=_params("parallel", "arbitrary"),
    )(sinks.astype(F32), q, kv, pos, tile2(q_norm), tile2(k_norm), invf, sgn)


def _gmlp_body(u_ref, v_ref, lg_ref, lbias_ref, ws_ref, bs_ref, o_ref, *, tb):
    c = GM_CHUNK
    gw = GM_WIDTH // GM_GROUPS
    row = lax.broadcasted_iota(jnp.int32, (c, c), 0)
    col = lax.broadcasted_iota(jnp.int32, (c, c), 1)
    causal = row >= col
    wcs = [jnp.where(causal, ws_ref[g], 0.0).astype(BF16) for g in range(GM_GROUPS)]
    bs = bs_ref[...]
    for r in range(0, tb, c):
        v = v_ref[r:r + c, :].astype(F32)
        mu = jnp.mean(v, axis=-1, keepdims=True)
        vc = v - mu
        var = jnp.mean(vc * vc, axis=-1, keepdims=True)
        vn = (vc * lax.rsqrt(var + EPS) * lg_ref[...] + lbias_ref[...]).astype(BF16)
        for g in range(GM_GROUPS):
            sl = slice(g * gw, (g + 1) * gw)
            mixed = _dot(wcs[g], vn[:, sl]) + bs[:, g:g + 1]
            o_ref[r:r + c, sl] = (u_ref[r:r + c, sl].astype(F32) * mixed).astype(o_ref.dtype)


def gmlp_spatial(z, ln_g, ln_b, w_s, b_s, tb=512):
    n = z.shape[0]
    d = GM_WIDTH
    tb = _row_tile(n, tb)
    const2 = lambda i: (0, 0)
    return pl.pallas_call(
        functools.partial(_gmlp_body, tb=tb),
        name="gmlp_spatial",
        out_shape=jax.ShapeDtypeStruct((n, d), BF16),
        grid=(n // tb,),
        in_specs=[pl.BlockSpec((tb, d), lambda i: (i, 0)),
                  pl.BlockSpec((tb, d), lambda i: (i, 1)),
                  pl.BlockSpec((1, d), const2),
                  pl.BlockSpec((1, d), const2),
                  pl.BlockSpec((GM_GROUPS, GM_CHUNK, GM_CHUNK), lambda i: (0, 0, 0)),
                  pl.BlockSpec((GM_CHUNK, GM_GROUPS), const2)],
        out_specs=pl.BlockSpec((tb, d), lambda i: (i, 0)),
        compiler_params=_params("parallel"),
    )(z, z, ln_g.reshape(1, d).astype(F32), ln_b.reshape(1, d).astype(F32),
      w_s.astype(F32), b_s.T.astype(F32))


def _gdn_body(qkv_ref, z_ref, ba_ref, cw_ref, alog_ref, dtb_ref, on_ref, o_ref,
              s_ref, x_ref, y_ref, *, tb):
    c = GD_CHUNK
    hd = GD_HEAD_DIM
    halo = 8
    nq = GD_QK_HEADS * hd
    it = pl.program_id(1)

    @pl.when(it == 0)
    def _():
        s_ref[...] = jnp.zeros_like(s_ref)
        x_ref[0:halo, :] = jnp.zeros((halo, GD_QKV), F32)

    @pl.when(it > 0)
    def _():
        x_ref[0:halo, :] = x_ref[tb:tb + halo, :]

    x_ref[halo:halo + tb, :] = qkv_ref[...].astype(F32)
    acc = None
    for jj in range(GD_CONV):
        start = halo - (GD_CONV - 1) + jj
        term = cw_ref[jj:jj + 1, :] * x_ref[start:start + tb, :]
        acc = term if acc is None else acc + term
    y_ref[...] = _silu(acc)
    for h in range(2 * GD_QK_HEADS):
        sl = slice(h * hd, (h + 1) * hd)
        a = y_ref[:, sl]
        a = a * lax.rsqrt(jnp.sum(a * a, axis=-1, keepdims=True) + EPS)
        if h < GD_QK_HEADS:
            a = a * (hd ** -0.5)
        y_ref[:, sl] = a

    row = lax.broadcasted_iota(jnp.int32, (c, c), 0)
    col = lax.broadcasted_iota(jnp.int32, (c, c), 1)
    tri = row >= col
    tri_strict = row > col
    tri_bf = tri.astype(BF16)
    eye = (row == col).astype(F32)
    lane = lax.broadcasted_iota(jnp.int32, (1, LANES), 1)
    neg_a = -jnp.exp(alog_ref[...])
    on = on_ref[...]

    heads = range(GD_V_HEADS)
    rep = GD_V_HEADS // GD_QK_HEADS
    col = lambda a, hv: a[:, hv:hv + 1]
    gcol = lambda a, hv: a[:, GD_V_HEADS + hv:GD_V_HEADS + hv + 1]
    n_par = GD_PAR_CHUNKS

    def group(gi, carry):
        rows = [pl.ds(pl.multiple_of((gi * n_par + j) * c, c), c) for j in range(n_par)]
        par = range(n_par)
        jobs = [(j, hv) for j in par for hv in heads]
        ba = [ba_ref[r, :] for r in rows]
        beta_all = [jax.nn.sigmoid(a) for a in ba]
        in_rate_lanes = (lane >= GD_V_HEADS) & (lane < 2 * GD_V_HEADS)
        g_all = [jnp.where(in_rate_lanes, neg_a * jax.nn.softplus(a + dtb_ref[...]), 0.0) for a in ba]
        gam_all = [_dot_split(tri_bf, a) for a in g_all]
        gam_rows = [a.T for a in gam_all]
        e_gam_all = [jnp.exp(a) for a in gam_all]
        gam_last_all = [a[c - 1:c, :] for a in gam_all]
        e_end_all = [jnp.exp(b - a) for a, b in zip(gam_all, gam_last_all)]
        d_end_all = [jnp.exp(a) for a in gam_last_all]
        q = [[y_ref[r, hk * hd:(hk + 1) * hd] for hk in range(GD_QK_HEADS)] for r in rows]
        k = [[y_ref[r, nq + hk * hd:nq + (hk + 1) * hd] for hk in range(GD_QK_HEADS)] for r in rows]
        k_bf = [[a.astype(BF16) for a in kj] for kj in k]
        kk = [[lax.dot_general(a, a, NT_DIMS, preferred_element_type=F32) for a in kj] for kj in k_bf]
        qk = [[lax.dot_general(a.astype(BF16), b, NT_DIMS, preferred_element_type=F32)
               for a, b in zip(q[j], k_bf[j])] for j in par]
        decay = [jnp.exp(jnp.where(
            tri, gcol(gam_all[j], hv) - gam_rows[j][GD_V_HEADS + hv:GD_V_HEADS + hv + 1, :], -jnp.inf))
            for j, hv in jobs]
        p = [-jnp.where(tri_strict, kk[j][hv // rep] * decay[n] * col(beta_all[j], hv), 0.0)
             for n, (j, hv) in enumerate(jobs)]
        rhs = [jnp.concatenate(
            [y_ref[rows[j], 2 * nq + hv * hd:2 * nq + (hv + 1) * hd] * col(beta_all[j], hv),
             k[j][hv // rep] * (col(beta_all[j], hv) * gcol(e_gam_all[j], hv))], axis=-1).astype(BF16)
            for j, hv in jobs]
        u = [eye + a for a in p]
        p_bf = [a.astype(BF16) for a in p]
        n_steps = c.bit_length() - 1
        for step in range(1, n_steps):
            p_bf = [_dot(pb, pb).astype(BF16) for pb in p_bf]
            u = [a + _dot(a.astype(BF16), pb) for a, pb in zip(u, p_bf)]
        sol = [_dot(a.astype(BF16), b) for a, b in zip(u, rhs)]
        q_dec = [(q[j][hv // rep] * gcol(e_gam_all[j], hv)).astype(BF16) for j, hv in jobs]
        qk_dec = [(qk[j][hv // rep] * decay[n]).astype(BF16) for n, (j, hv) in enumerate(jobs)]
        k_end = [(k[j][hv // rep] * gcol(e_end_all[j], hv)).astype(BF16) for j, hv in jobs]
        gate = [_silu(z_ref[rows[j], hv * hd:(hv + 1) * hd].astype(F32)) for j, hv in jobs]
        for j in par:
            at = lambda hv: j * GD_V_HEADS + hv
            s_old = [s_ref[hv] for hv in heads]
            s_bf = [a.astype(BF16) for a in s_old]
            nv_bf = [(sol[at(hv)][:, :hd] - _dot(sol[at(hv)][:, hd:].astype(BF16), s_bf[hv])).astype(BF16)
                     for hv in heads]
            o = [_dot(q_dec[at(hv)], s_bf[hv]) + _dot(qk_dec[at(hv)], nv_bf[hv]) for hv in heads]
            s_new = [s_old[hv] * gcol(d_end_all[j], hv) + lax.dot_general(
                k_end[at(hv)], nv_bf[hv], TN_DIMS, preferred_element_type=F32) for hv in heads]
            for hv in heads:
                s_ref[hv] = s_new[hv]
            outs = [_rms(o[hv], on) * gate[at(hv)] for hv in heads]
            o_ref[rows[j], :] = jnp.concatenate(outs, axis=-1).astype(o_ref.dtype)
        return carry

    lax.fori_loop(0, tb // (c * n_par), group, 0)


def gdn_recurrence(qkv, z, ba, conv_w, a_log, dt_bias, o_norm, batch, seq, tb=256):
    n = batch * seq
    tb = _row_tile(seq, tb)
    nt = seq // tb
    pad_heads = lambda a: jnp.zeros((1, LANES), F32).at[0, GD_V_HEADS:2 * GD_V_HEADS].set(a.astype(F32))
    row = lambda b, t: (b * nt + t, 0)
    const = lambda b, t: (0, 0)
    return pl.pallas_call(
        functools.partial(_gdn_body, tb=tb),
        name="gdn_recurrence",
        out_shape=jax.ShapeDtypeStruct((n, GD_Z), BF16),
        grid=(batch, nt),
        in_specs=[pl.BlockSpec((tb, GD_QKV), row),
                  pl.BlockSpec((tb, GD_Z), row),
                  pl.BlockSpec((tb, LANES), row),
                  pl.BlockSpec((GD_CONV, GD_QKV), const),
                  pl.BlockSpec((1, LANES), const),
                  pl.BlockSpec((1, LANES), const),
                  pl.BlockSpec((1, GD_HEAD_DIM), const)],
        out_specs=pl.BlockSpec((tb, GD_Z), row),
        scratch_shapes=[pltpu.VMEM((GD_V_HEADS, GD_HEAD_DIM, GD_HEAD_DIM), F32),
                        pltpu.VMEM((tb + 16, GD_QKV), F32),
                        pltpu.VMEM((tb, GD_QKV), F32)],
        compiler_params=_params("parallel", "arbitrary"),
    )(qkv, z, ba, conv_w.astype(F32), pad_heads(a_log), pad_heads(dt_bias),
      o_norm.reshape(1, GD_HEAD_DIM).astype(F32))


def _router_body(x_ref, g_ref, wr_ref, o_ref):
    h = _rms(x_ref[...], g_ref[...])
    w = wr_ref[...]
    h_hi = h.astype(BF16)
    h_lo = (h - h_hi.astype(F32)).astype(BF16)
    w_hi = w.astype(BF16)
    w_lo = (w - w_hi.astype(F32)).astype(BF16)
    logits = _dot(h_hi, w_hi) + (_dot(h_lo, w_hi) + _dot(h_hi, w_lo))
    lane = lax.broadcasted_iota(jnp.int32, logits.shape, 1)
    logits = jnp.where(lane < N_EXPERTS, logits, -jnp.inf)
    m1 = jnp.max(logits, axis=-1, keepdims=True)
    i1 = jnp.min(jnp.where(logits == m1, lane, LANES), axis=-1, keepdims=True)
    rest = jnp.where(lane == i1, -jnp.inf, logits)
    m2 = jnp.max(rest, axis=-1, keepdims=True)
    i2 = jnp.min(jnp.where(rest == m2, lane, LANES), axis=-1, keepdims=True)
    e2 = jnp.exp(m2 - m1)
    w1 = 1.0 / (1.0 + e2)
    w2 = e2 / (1.0 + e2)
    out = jnp.where(lane == 0, i1.astype(F32),
                    jnp.where(lane == 1, i2.astype(F32),
                              jnp.where(lane == 2, w1, jnp.where(lane == 3, w2, 0.0))))
    o_ref[...] = out


def moe_router(x, g, router, tm=512):
    n, d = x.shape
    tm = _row_tile(n, tm)
    wr = jnp.zeros((d, LANES), F32).at[:, :N_EXPERTS].set(router.astype(F32))
    return pl.pallas_call(
        _router_body,
        name="moe_router",
        out_shape=jax.ShapeDtypeStruct((n, LANES), F32),
        grid=(n // tm,),
        in_specs=[pl.BlockSpec((tm, d), lambda i: (i, 0)),
                  pl.BlockSpec((1, d), lambda i: (0, 0)),
                  pl.BlockSpec((d, LANES), lambda i: (0, 0))],
        out_specs=pl.BlockSpec((tm, LANES), lambda i: (i, 0)),
        compiler_params=_params("parallel"),
    )(x, g.reshape(1, d).astype(F32), wr)


def _row_copy(src_ref, src_row, dst_ref, dst_row, sem):
    return pltpu.make_async_copy(src_ref.at[pl.ds(src_row, 1), :], dst_ref.at[pl.ds(dst_row, 1), :], sem)


def _dispatch_body(ends_ref, padded_ref, pos_ref, x_ref, xs_ref, zero_ref, sem, zero_sem, *, rt, tm):
    @pl.when(pl.program_id(0) == 0)
    def _():
        zero_ref[...] = jnp.zeros_like(zero_ref)

        def zero_tile(row0):
            row0 = row0 if isinstance(row0, int) else pl.multiple_of(row0, tm)
            cp = pltpu.make_async_copy(zero_ref, xs_ref.at[pl.ds(row0, tm), :], zero_sem)
            cp.start()
            cp.wait()

        for e in range(N_EXPERTS):
            pl.when(padded_ref[e] > 0)(functools.partial(zero_tile, ends_ref[e] - tm))
        n_rows = xs_ref.shape[0]
        for back in range(1, N_EXPERTS + 1):
            row0 = n_rows - back * tm
            pl.when(row0 >= ends_ref[N_EXPERTS - 1])(functools.partial(zero_tile, row0))

    def start(g, carry):
        r0 = pl.multiple_of(g * SUBLANES, SUBLANES)
        for u in range(SUBLANES):
            for k in range(TOP_K):
                _row_copy(x_ref, r0 + u, xs_ref, pos_ref[0, 0, TOP_K * (r0 + u) + k], sem).start()
        return carry

    lax.fori_loop(0, rt // SUBLANES, start, 0)
    all_rows = xs_ref.at[pl.ds(0, TOP_K * rt), :]
    pltpu.make_async_copy(all_rows, all_rows, sem).wait()


def moe_dispatch(x, pos, ends, padded, n_rows, tm, rt=256):
    n, d = x.shape
    rt = _row_tile(n, rt)
    grid_spec = pltpu.PrefetchScalarGridSpec(
        num_scalar_prefetch=2,
        grid=(n // rt,),
        in_specs=[pl.BlockSpec((1, 1, TOP_K * rt), lambda i, en, pa: (i, 0, 0), memory_space=pltpu.SMEM),
                  pl.BlockSpec((rt, d), lambda i, en, pa: (i, 0))],
        out_specs=pl.BlockSpec(memory_space=pl.ANY),
        scratch_shapes=[pltpu.VMEM((tm, d), F32), pltpu.SemaphoreType.DMA(()),
                        pltpu.SemaphoreType.DMA(())],
    )
    return pl.pallas_call(
        functools.partial(_dispatch_body, rt=rt, tm=tm),
        name="moe_dispatch",
        out_shape=jax.ShapeDtypeStruct((n_rows, d), F32),
        grid_spec=grid_spec,
        compiler_params=_params("arbitrary"),
    )(ends, padded, pos.reshape(n // rt, 1, TOP_K * rt), x)


def _expert_changed(te_ref, i):
    return (i == 0) | (te_ref[i] != te_ref[jnp.maximum(i - 1, 0)])


def _moe_up_body(te_ref, nt_ref, xs_ref, g_ref, wg_ref, wu_ref, o_ref, wg_bf, wu_bf, *, chunk):
    i = pl.program_id(1)
    f = o_ref.shape[1]

    @pl.when(_expert_changed(te_ref, i))
    def _():
        for c in range(0, f, chunk):
            wg_bf[:, c:c + chunk] = wg_ref[:, c:c + chunk].astype(BF16)
            wu_bf[:, c:c + chunk] = wu_ref[:, c:c + chunk].astype(BF16)

    @pl.when(i < nt_ref[0])
    def _():
        h = _rms(xs_ref[...], g_ref[...]).astype(BF16)
        for c in range(0, f, chunk):
            a = _dot(h, wg_bf[:, c:c + chunk])
            b = _dot(h, wu_bf[:, c:c + chunk])
            o_ref[:, c:c + chunk] = (_silu(a) * b).astype(o_ref.dtype)

    @pl.when(i >= nt_ref[0])
    def _():
        o_ref[...] = jnp.zeros_like(o_ref)


def moe_up(xs, g, wg, wu, layer, tile_expert, n_tiles_used, tm, fsplit=2, chunk=256):
    p, d = xs.shape
    f = wg.shape[3]
    fb = f // fsplit
    assert fb % chunk == 0 and p % tm == 0
    w_spec = pl.BlockSpec((None, None, d, fb), lambda j, i, te, nt: (layer, te[i], 0, j))
    grid_spec = pltpu.PrefetchScalarGridSpec(
        num_scalar_prefetch=2,
        grid=(fsplit, p // tm),
        in_specs=[pl.BlockSpec((tm, d), lambda j, i, te, nt: (jnp.minimum(i, nt[0] - 1), 0)),
                  pl.BlockSpec((1, d), lambda j, i, te, nt: (0, 0)),
                  w_spec, w_spec],
        out_specs=pl.BlockSpec((tm, fb), lambda j, i, te, nt: (i, j)),
        scratch_shapes=[pltpu.VMEM((d, fb), BF16), pltpu.VMEM((d, fb), BF16)],
    )
    return pl.pallas_call(
        functools.partial(_moe_up_body, chunk=chunk),
        name="moe_up",
        out_shape=jax.ShapeDtypeStruct((p, f), BF16),
        grid_spec=grid_spec,
        compiler_params=_params("arbitrary", "arbitrary"),
    )(tile_expert, n_tiles_used, xs, g.reshape(1, d).astype(F32), wg, wu)


def _moe_down_body(te_ref, nt_ref, a_ref, wd_ref, o_ref, wd_bf, *, chunk):
    i = pl.program_id(0)

    @pl.when(_expert_changed(te_ref, i))
    def _():
        for r in range(0, wd_bf.shape[0], chunk):
            wd_bf[r:r + chunk, :] = wd_ref[r:r + chunk, :].astype(BF16)

    @pl.when(i < nt_ref[0])
    def _():
        o_ref[...] = _dot(a_ref[...], wd_bf[...])

    @pl.when(i >= nt_ref[0])
    def _():
        o_ref[...] = jnp.zeros_like(o_ref)


def moe_down(act, wd, layer, tile_expert, n_tiles_used, tm, chunk=512):
    p, f = act.shape
    d = wd.shape[3]
    assert f % chunk == 0
    grid_spec = pltpu.PrefetchScalarGridSpec(
        num_scalar_prefetch=2,
        grid=(p // tm,),
        in_specs=[pl.BlockSpec((tm, f), lambda i, te, nt: (i, 0)),
                  pl.BlockSpec((None, None, f, d), lambda i, te, nt: (layer, te[i], 0, 0))],
        out_specs=pl.BlockSpec((tm, d), lambda i, te, nt: (i, 0)),
        scratch_shapes=[pltpu.VMEM((f, d), BF16)],
    )
    return pl.pallas_call(
        functools.partial(_moe_down_body, chunk=chunk),
        name="moe_down",
        out_shape=jax.ShapeDtypeStruct((p, d), F32),
        grid_spec=grid_spec,
        compiler_params=_params("arbitrary"),
    )(tile_expert, n_tiles_used, act, wd)


def _combine_body(pos_ref, pos_next_ref, x_ref, r_ref, ys_ref, o_ref, buf_ref, sem, *, rt):
    i = pl.program_id(0)
    slot = i % 2

    def gather(p_ref, s):
        def start(g, carry):
            r0 = pl.multiple_of(g * SUBLANES, SUBLANES)
            for u in range(SUBLANES):
                for k in range(TOP_K):
                    _row_copy(ys_ref, p_ref[0, 0, TOP_K * (r0 + u) + k], buf_ref.at[s, k], r0 + u,
                              sem.at[s]).start()
            return carry

        lax.fori_loop(0, rt // SUBLANES, start, 0)

    @pl.when(i == 0)
    def _():
        gather(pos_ref, 0)

    @pl.when(i + 1 < pl.num_programs(0))
    def _():
        gather(pos_next_ref, 1 - slot)

    for k in range(TOP_K):
        pltpu.make_async_copy(ys_ref.at[pl.ds(0, rt), :], buf_ref.at[slot, k], sem.at[slot]).wait()
    r = r_ref[...]
    o_ref[...] = x_ref[...] + r[:, 2:3] * buf_ref[slot, 0] + r[:, 3:4] * buf_ref[slot, 1]


def moe_combine(x, route, ys, pos, rt=256):
    n, d = x.shape
    rt = _row_tile(n, rt)
    steps = n // rt
    pos3 = pos.reshape(steps, 1, TOP_K * rt)
    grid_spec = pltpu.PrefetchScalarGridSpec(
        num_scalar_prefetch=0,
        grid=(steps,),
        in_specs=[pl.BlockSpec((1, 1, TOP_K * rt), lambda i: (i, 0, 0), memory_space=pltpu.SMEM),
                  pl.BlockSpec((1, 1, TOP_K * rt), lambda i: (jnp.minimum(i + 1, steps - 1), 0, 0),
                               memory_space=pltpu.SMEM),
                  pl.BlockSpec((rt, d), lambda i: (i, 0)),
                  pl.BlockSpec((rt, LANES), lambda i: (i, 0)),
                  pl.BlockSpec(memory_space=pl.ANY)],
        out_specs=pl.BlockSpec((rt, d), lambda i: (i, 0)),
        scratch_shapes=[pltpu.VMEM((2, TOP_K, rt, d), F32), pltpu.SemaphoreType.DMA((2,))],
    )
    return pl.pallas_call(
        functools.partial(_combine_body, rt=rt),
        name="moe_combine",
        out_shape=jax.ShapeDtypeStruct((n, d), F32),
        grid_spec=grid_spec,
        compiler_params=_params("arbitrary"),
    )(pos3, pos3, x, route, ys)


def moe_ffn(x, g, router, wg, wu, wd, layer, tm=512):
    n, d = x.shape
    route = moe_router(x, g, router)
    expert = route[:, :TOP_K].astype(jnp.int32).reshape(n * TOP_K)
    onehot = (expert[:, None] == jnp.arange(N_EXPERTS, dtype=jnp.int32)[None, :]).astype(jnp.int32)
    csum = jnp.cumsum(onehot, axis=0)
    rank = jnp.sum((csum - onehot) * onehot, axis=1)
    counts = csum[-1]
    padded = ((counts + tm - 1) // tm) * tm
    ends = jnp.cumsum(padded)
    starts = ends - padded
    pos = (jnp.sum(starts[None, :] * onehot, axis=1) + rank).astype(jnp.int32)
    n_rows = n * TOP_K + N_EXPERTS * tm
    n_tiles = n_rows // tm
    tile_start = jnp.arange(n_tiles, dtype=jnp.int32) * tm
    tile_expert = jnp.minimum(jnp.sum((tile_start[:, None] >= ends[None, :]).astype(jnp.int32), axis=1),
                              N_EXPERTS - 1).astype(jnp.int32)
    n_tiles_used = (ends[-1:] // tm).astype(jnp.int32)

    xs = moe_dispatch(x, pos, ends.astype(jnp.int32), padded.astype(jnp.int32), n_rows, tm)
    act = moe_up(xs, g, wg, wu, layer, tile_expert, n_tiles_used, tm)
    ys = moe_down(act, wd, layer, tile_expert, n_tiles_used, tm)
    return moe_combine(x, route, ys, pos)


def hgrn2_layer(x, g, w_in, o_norm, w_out, lower_bound, batch, seq):
    (qfig,) = norm_matmul(x, g, w_in.astype(BF16), [(4 * D_MODEL, BF16)])
    o = hgrn_recurrence(qfig, lower_bound, o_norm, batch, seq)
    return matmul_residual(o, w_out.astype(BF16), x)


def swa_layer(x, g, positions, w_in, q_norm, k_norm, sinks, w_out, batch, seq):
    q, kv = norm_matmul(x, g, w_in.astype(BF16),
                        [(D_MODEL, BF16), (2 * SW_KV_HEADS * SW_HEAD_DIM, BF16)])
    pos = positions.astype(F32).reshape(batch * seq, 1)
    o = swa_attention(q, kv, pos, q_norm, k_norm, sinks, batch, seq)
    return matmul_residual(o, w_out.astype(BF16), x)


def gmlp_layer(x, g, w_in, b_in, ln_g, ln_b, w_s, b_s, w_out):
    (z,) = norm_matmul(x, g, w_in.astype(BF16), [(2 * GM_WIDTH, BF16)], bias=b_in, act="gelu")
    o = gmlp_spatial(z, ln_g, ln_b, w_s, b_s)
    return matmul_residual(o, w_out.astype(BF16), x)


def gdn_layer(x, g, w_in, conv_w, a_log, dt_bias, o_norm, w_out, batch, seq):
    d = x.shape[1]
    n_small = w_in.shape[1] - GD_QKV - GD_Z
    w_main = w_in[:, :GD_QKV + GD_Z].astype(BF16)
    w_small = jnp.zeros((d, LANES), F32).at[:, :n_small].set(w_in[:, GD_QKV + GD_Z:].astype(F32))
    w_all = jnp.concatenate([w_main, w_small.astype(BF16)], axis=1)
    qkv, z, ba = norm_matmul(x, g, w_all, [(GD_QKV, BF16), (GD_Z, BF16), (LANES, F32)])
    o = gdn_recurrence(qkv, z, ba, conv_w, a_log, dt_bias, o_norm, batch, seq)
    return matmul_residual(o, w_out.astype(BF16), x)


def dense_ffn(x, g, w_gate, w_up, w_down):
    act = swiglu_up(x, g, w_gate.astype(BF16), w_up.astype(BF16))
    return matmul_residual(act, w_down.astype(BF16), x)


def kernel(x, positions, mix_norm, ffn_norm, hgrn_lb_logits, hgrn_w_in, hgrn_o_norm, hgrn_w_out,
           swa_w_in, swa_q_norm, swa_k_norm, swa_sinks, swa_w_out,
           gmlp_w_in, gmlp_b_in, gmlp_v_ln_g, gmlp_v_ln_b, gmlp_w_s, gmlp_b_s, gmlp_w_out,
           gdn_w_in, gdn_conv_w, gdn_a_log, gdn_dt_bias, gdn_o_norm, gdn_w_out,
           dense_w_gate, dense_w_up, dense_w_down,
           moe_router, moe_w_gate, moe_w_up, moe_w_down):
    batch, seq, d = x.shape
    depth = mix_norm.shape[0]
    lower_bounds = jnp.cumsum(jax.nn.softmax(hgrn_lb_logits.astype(F32), axis=0), axis=0)
    h = x.reshape(batch * seq, d)
    for i in range(depth):
        kind, j = i % 4, i // 4
        if kind == 0:
            h = hgrn2_layer(h, mix_norm[i], hgrn_w_in[j], hgrn_o_norm[j], hgrn_w_out[j],
                            lower_bounds[i], batch, seq)
        elif kind == 1:
            h = swa_layer(h, mix_norm[i], positions, swa_w_in[j], swa_q_norm[j], swa_k_norm[j],
                          swa_sinks[j], swa_w_out[j], batch, seq)
        elif kind == 2:
            h = gmlp_layer(h, mix_norm[i], gmlp_w_in[j], gmlp_b_in[j], gmlp_v_ln_g[j], gmlp_v_ln_b[j],
                           gmlp_w_s[j], gmlp_b_s[j], gmlp_w_out[j])
        else:
            h = gdn_layer(h, mix_norm[i], gdn_w_in[j], gdn_conv_w[j], gdn_a_log[j], gdn_dt_bias[j],
                          gdn_o_norm[j], gdn_w_out[j], batch, seq)
        if i % 2 == 0:
            h = dense_ffn(h, ffn_norm[i], dense_w_gate[i // 2], dense_w_up[i // 2], dense_w_down[i // 2])
        else:
            h = moe_ffn(h, ffn_norm[i], moe_router[i // 2], moe_w_gate, moe_w_up, moe_w_down, i // 2)
    return h.reshape(batch, seq, d)
```

```python
import functools

import jax
import jax.numpy as jnp
from jax import lax
from jax.experimental import pallas as pl
from jax.experimental.pallas import tpu as pltpu

F32 = jnp.float32
BF16 = jnp.bfloat16
EPS = 1e-6

D_MODEL = 1024
LANES = 128
SUBLANES = 8
HG_HEAD_DIM = 128
HG_HEADS = D_MODEL // HG_HEAD_DIM
HG_CHUNK = 32
SW_HEAD_DIM = 64
SW_HEADS = D_MODEL // SW_HEAD_DIM
SW_KV_HEADS = 4
SW_GROUP = SW_HEADS // SW_KV_HEADS
SW_WINDOW = 128
ROPE_THETA = 10000.0
GM_WIDTH = D_MODEL
GM_GROUPS = 8
GM_CHUNK = 128
GD_HEAD_DIM = 128
GD_QK_HEADS = D_MODEL // GD_HEAD_DIM
GD_V_HEADS = 2 * GD_QK_HEADS
GD_CONV = 4
GD_CHUNK = 64
GD_QKV = 2 * GD_QK_HEADS * GD_HEAD_DIM + GD_V_HEADS * GD_HEAD_DIM
GD_Z = GD_V_HEADS * GD_HEAD_DIM
GD_PAR_CHUNKS = 4
N_EXPERTS = 8
TOP_K = 2

V7X_VMEM_LIMIT_BYTES = 56 * 1024 * 1024

NT_DIMS = (((1,), (1,)), ((), ()))
TN_DIMS = (((0,), (0,)), ((), ()))


def _params(*sem):
    return pltpu.CompilerParams(dimension_semantics=sem, vmem_limit_bytes=V7X_VMEM_LIMIT_BYTES)


def _dot(a, b):
    return jnp.dot(a, b, preferred_element_type=F32)


def _dot_split(a_bf, x):
    hi = x.astype(BF16)
    lo = (x - hi.astype(F32)).astype(BF16)
    return _dot(a_bf, hi) + _dot(a_bf, lo)


def _rms(x, g):
    return x * lax.rsqrt(jnp.mean(x * x, axis=-1, keepdims=True) + EPS) * g


def _silu(x):
    return x * jax.nn.sigmoid(x)


def _row_tile(n, want):
    t = min(n, want)
    assert n % t == 0, (n, t)
    return t


def _norm_mm_body(x_ref, g_ref, w_ref, *rest, widths, chunk, use_bias, act):
    if use_bias:
        b_ref, out_refs = rest[0], rest[1:]
    else:
        b_ref, out_refs = None, rest
    h = _rms(x_ref[...], g_ref[...]).astype(BF16)
    off = 0
    for o_ref, width in zip(out_refs, widths):
        for c in range(0, width, chunk):
            cw = min(chunk, width - c)
            y = _dot(h, w_ref[:, off + c:off + c + cw])
            if use_bias:
                y = y + b_ref[:, off + c:off + c + cw]
            if act == "gelu":
                y = 0.5 * y * (1.0 + lax.erf(y * (0.5 ** 0.5)))
            o_ref[:, c:c + cw] = y.astype(o_ref.dtype)
        off += width


def norm_matmul(x, g, w, outs, bias=None, act=None, tm=512, chunk=512):
    n, d = x.shape
    m = w.shape[1]
    widths = tuple(o[0] for o in outs)
    assert sum(widths) == m
    tm = _row_tile(n, tm)
    in_specs = [pl.BlockSpec((tm, d), lambda i: (i, 0)),
                pl.BlockSpec((1, d), lambda i: (0, 0)),
                pl.BlockSpec((d, m), lambda i: (0, 0))]
    args = [x, g.reshape(1, d).astype(F32), w]
    if bias is not None:
        in_specs.append(pl.BlockSpec((1, m), lambda i: (0, 0)))
        args.append(bias.reshape(1, m).astype(F32))
    body = functools.partial(_norm_mm_body, widths=widths, chunk=chunk,
                             use_bias=bias is not None, act=act)
    return pl.pallas_call(
        body,
        name="norm_matmul",
        out_shape=[jax.ShapeDtypeStruct((n, wd), dt) for wd, dt in outs],
        grid=(n // tm,),
        in_specs=in_specs,
        out_specs=[pl.BlockSpec((tm, wd), lambda i: (i, 0)) for wd, _ in outs],
        compiler_params=_params("parallel"),
    )(*args)


def _mm_res_body(a_ref, w_ref, x_ref, o_ref):
    o_ref[...] = x_ref[...] + _dot(a_ref[...], w_ref[...])


def matmul_residual(a, w, x, tm=512):
    n, k = a.shape
    d = w.shape[1]
    tm = _row_tile(n, tm)
    return pl.pallas_call(
        _mm_res_body,
        name="matmul_residual",
        out_shape=jax.ShapeDtypeStruct((n, d), F32),
        grid=(n // tm,),
        in_specs=[pl.BlockSpec((tm, k), lambda i: (i, 0)),
                  pl.BlockSpec((k, d), lambda i: (0, 0)),
                  pl.BlockSpec((tm, d), lambda i: (i, 0))],
        out_specs=pl.BlockSpec((tm, d), lambda i: (i, 0)),
        compiler_params=_params("parallel"),
    )(a, w, x)


def _swiglu_up_body(x_ref, g_ref, wg_ref, wu_ref, o_ref, *, chunk):
    h = _rms(x_ref[...], g_ref[...]).astype(BF16)
    f = o_ref.shape[1]
    for c in range(0, f, chunk):
        a = _dot(h, wg_ref[:, c:c + chunk])
        b = _dot(h, wu_ref[:, c:c + chunk])
        o_ref[:, c:c + chunk] = (_silu(a) * b).astype(o_ref.dtype)


def swiglu_up(x, g, wg, wu, tm=512, chunk=256):
    n, d = x.shape
    f = wg.shape[1]
    assert f % chunk == 0
    tm = _row_tile(n, tm)
    return pl.pallas_call(
        functools.partial(_swiglu_up_body, chunk=chunk),
        name="swiglu_up",
        out_shape=jax.ShapeDtypeStruct((n, f), BF16),
        grid=(n // tm,),
        in_specs=[pl.BlockSpec((tm, d), lambda i: (i, 0)),
                  pl.BlockSpec((1, d), lambda i: (0, 0)),
                  pl.BlockSpec((d, f), lambda i: (0, 0)),
                  pl.BlockSpec((d, f), lambda i: (0, 0))],
        out_specs=pl.BlockSpec((tm, f), lambda i: (i, 0)),
        compiler_params=_params("parallel"),
    )(x, g.reshape(1, d).astype(F32), wg, wu)


def _hgrn_body(q_ref, f_ref, i_ref, g_ref, lb_ref, on_ref, o_ref, s_ref, *, tb):
    c32 = HG_CHUNK
    hd = HG_HEAD_DIM

    @pl.when(pl.program_id(1) == 0)
    def _():
        s_ref[...] = jnp.zeros_like(s_ref)

    lb = lb_ref[...]
    on = on_ref[...]
    row = lax.broadcasted_iota(jnp.int32, (c32, c32), 0)
    col = lax.broadcasted_iota(jnp.int32, (c32, c32), 1)
    causal = row >= col
    tri_bf = causal.astype(BF16)

    def chunk(c, carry):
        r0 = pl.multiple_of(c * c32, c32)
        rows = pl.ds(r0, c32)
        forget = lb + (1.0 - lb) * jax.nn.sigmoid(f_ref[rows, :].astype(F32))
        b = _dot_split(tri_bf, jnp.log(forget))
        b_mid = b[c32 // 2 - 1:c32 // 2, :]
        b_last = b[c32 - 1:c32, :]
        qs = _silu(q_ref[rows, :].astype(F32))
        k = 1.0 - forget
        v_bf = i_ref[rows, :]
        qa = (qs * jnp.exp(b - b_mid)).astype(BF16)
        ka = (k * jnp.exp(b_mid - b)).astype(BF16)
        q_in = (qs * jnp.exp(b)).astype(BF16)
        k_end = (k * jnp.exp(b_last - b)).astype(BF16)
        d_end = jnp.exp(b_last)
        d_cols = jnp.concatenate([d_end[:, h * hd:(h + 1) * hd] for h in range(HG_HEADS)], axis=0).T
        gate = _silu(g_ref[rows, :].astype(F32))
        heads = range(HG_HEADS)
        sls = [slice(h * hd, (h + 1) * hd) for h in heads]
        s_old = [s_ref[h] for h in heads]
        sc = [lax.dot_general(qa[:, sl], ka[:, sl], NT_DIMS, preferred_element_type=F32) for sl in sls]
        inter = [_dot(q_in[:, sl], s.astype(BF16)) for sl, s in zip(sls, s_old)]
        upd = [lax.dot_general(k_end[:, sl], v_bf[:, sl], TN_DIMS, preferred_element_type=F32) for sl in sls]
        sc_bf = [jnp.where(causal, a, 0.0).astype(BF16) for a in sc]
        o = [_dot(a, v_bf[:, sl]) + b for a, sl, b in zip(sc_bf, sls, inter)]
        for h in heads:
            s_ref[h] = s_old[h] * d_cols[:, h:h + 1] + upd[h]
        outs = [_rms(a, on[:, sl]) for a, sl in zip(o, sls)]
        o_ref[rows, :] = (jnp.concatenate(outs, axis=-1) * gate).astype(o_ref.dtype)
        return carry

    lax.fori_loop(0, tb // c32, chunk, 0, unroll=8)


def hgrn_recurrence(qfig, lb, o_norm, batch, seq, tb=256):
    n = batch * seq
    d = D_MODEL
    tb = _row_tile(seq, tb)
    nt = seq // tb
    col_spec = lambda j: pl.BlockSpec((tb, d), lambda b, t, j=j: (b * nt + t, j))
    return pl.pallas_call(
        functools.partial(_hgrn_body, tb=tb),
        name="hgrn_recurrence",
        out_shape=jax.ShapeDtypeStruct((n, d), BF16),
        grid=(batch, nt),
        in_specs=[col_spec(0), col_spec(1), col_spec(2), col_spec(3),
                  pl.BlockSpec((1, d), lambda b, t: (0, 0)),
                  pl.BlockSpec((1, d), lambda b, t: (0, 0))],
        out_specs=pl.BlockSpec((tb, d), lambda b, t: (b * nt + t, 0)),
        scratch_shapes=[pltpu.VMEM((HG_HEADS, HG_HEAD_DIM, HG_HEAD_DIM), F32)],
        compiler_params=_params("parallel", "arbitrary"),
    )(qfig, qfig, qfig, qfig, lb.reshape(1, d).astype(F32), o_norm.reshape(1, d).astype(F32))


def _swa_body(sink_ref, q_ref, kv_ref, pos_ref, qn_ref, kn_ref, invf_ref, sgn_ref, o_ref,
              kprev_ref, vprev_ref):
    w = SW_WINDOW
    hd = SW_HEAD_DIM
    j = pl.program_id(1)

    @pl.when(j == 0)
    def _():
        kprev_ref[...] = jnp.zeros_like(kprev_ref)
        vprev_ref[...] = jnp.zeros_like(vprev_ref)

    lane = lax.broadcasted_iota(jnp.int32, (w, LANES), 1)
    first_half = (lane % hd) < (hd // 2)
    low_head = lane < hd
    ang = pos_ref[...] * invf_ref[...]
    cos = jnp.cos(ang)
    sin_signed = jnp.sin(ang) * sgn_ref[...]
    gi = lax.broadcasted_iota(jnp.int32, (LANES, LANES), 0) // hd
    gj = lax.broadcasted_iota(jnp.int32, (LANES, LANES), 1) // hd
    head_mean = jnp.where(gi == gj, 1.0 / hd, 0.0).astype(BF16)

    def norm_rope(x, gain):
        xx = x * x
        hi = xx.astype(BF16)
        lo = (xx - hi.astype(F32)).astype(BF16)
        ms = _dot(hi, head_mean) + _dot(lo, head_mean)
        x = x * lax.rsqrt(ms + EPS) * gain
        partner = jnp.where(first_half, pltpu.roll(x, LANES - hd // 2, 1), pltpu.roll(x, hd // 2, 1))
        return x * cos + partner * sin_signed

    qn = qn_ref[...]
    kn = kn_ref[...]
    kv = kv_ref[...].astype(F32)
    n_kv_tiles = SW_KV_HEADS * hd // LANES
    k_cur = [norm_rope(kv[:, t * LANES:(t + 1) * LANES], kn) for t in range(n_kv_tiles)]
    v_cur = [kv[:, (n_kv_tiles + t) * LANES:(n_kv_tiles + t + 1) * LANES] for t in range(n_kv_tiles)]
    k_all = [jnp.concatenate([kprev_ref[t], k_cur[t]], axis=0) for t in range(n_kv_tiles)]
    v_all = [jnp.concatenate([vprev_ref[t], v_cur[t]], axis=0) for t in range(n_kv_tiles)]
    for t in range(n_kv_tiles):
        kprev_ref[t] = k_cur[t]
        vprev_ref[t] = v_cur[t]

    lane2 = lax.broadcasted_iota(jnp.int32, (2 * w, LANES), 1)
    low2 = lane2 < hd
    c_idx = lax.broadcasted_iota(jnp.int32, (w, 2 * w), 0)
    r_idx = lax.broadcasted_iota(jnp.int32, (w, 2 * w), 1)
    rel = c_idx + w - r_idx
    valid = (rel >= 0) & (rel < w) & ((j - 1) * w + r_idx >= 0)
    scale = hd ** -0.5

    k_half, v_half = [], []
    for kvh in range(SW_KV_HEADS):
        kt, vt = k_all[kvh // 2], v_all[kvh // 2]
        kt_sw = pltpu.roll(kt, hd, 1)
        vt_sw = pltpu.roll(vt, hd, 1)
        lo_src, hi_src = ((kt, vt), (kt_sw, vt_sw)) if kvh % 2 == 0 else ((kt_sw, vt_sw), (kt, vt))
        k_half.append((jnp.where(low2, lo_src[0], 0.0).astype(BF16), jnp.where(low2, 0.0, hi_src[0]).astype(BF16)))
        v_half.append((jnp.where(low2, lo_src[1], 0.0).astype(BF16), jnp.where(low2, 0.0, hi_src[1]).astype(BF16)))
    n_pairs = SW_HEADS // 2
    qp = [(norm_rope(q_ref[:, t * LANES:(t + 1) * LANES].astype(F32), qn) * scale).astype(BF16)
          for t in range(n_pairs)]
    heads = range(SW_HEADS)
    s = [jnp.where(valid, lax.dot_general(qp[h // 2], k_half[h // SW_GROUP][h % 2], NT_DIMS,
                                          preferred_element_type=F32), -jnp.inf) for h in heads]
    m = [jnp.maximum(jnp.max(s[h], axis=-1, keepdims=True), sink_ref[h]) for h in heads]
    p = [jnp.exp(s[h] - m[h]) for h in heads]
    denom = [jnp.sum(p[h], axis=-1, keepdims=True) + jnp.exp(sink_ref[h] - m[h]) for h in heads]
    pv = [_dot(p[h].astype(BF16), v_half[h // SW_GROUP][h % 2]) / denom[h] for h in heads]
    for t in range(n_pairs):
        o_ref[:, t * LANES:(t + 1) * LANES] = (pv[2 * t] + pv[2 * t + 1]).astype(o_ref.dtype)


def swa_attention(q, kv, pos, q_norm, k_norm, sinks, batch, seq):
    n = batch * seq
    w = SW_WINDOW
    nb = seq // w
    hd = SW_HEAD_DIM
    lane = jnp.arange(LANES)
    inv_freq = ROPE_THETA ** (-jnp.arange(0, hd, 2, dtype=F32) / hd)
    invf = inv_freq[lane % (hd // 2)].reshape(1, LANES)
    sgn = jnp.where((lane % hd) < hd // 2, -1.0, 1.0).astype(F32).reshape(1, LANES)
    tile2 = lambda g: jnp.tile(g.astype(F32), LANES // hd).reshape(1, LANES)
    n_kv_tiles = SW_KV_HEADS * hd // LANES
    row = lambda b, jj, s: (b * nb + jj, 0)
    const = lambda b, jj, s: (0, 0)
    grid_spec = pltpu.PrefetchScalarGridSpec(
        num_scalar_prefetch=1,
        grid=(batch, nb),
        in_specs=[pl.BlockSpec((w, D_MODEL), row),
                  pl.BlockSpec((w, 2 * SW_KV_HEADS * hd), row),
                  pl.BlockSpec((w, 1), row),
                  pl.BlockSpec((1, LANES), const),
                  pl.BlockSpec((1, LANES), const),
                  pl.BlockSpec((1, LANES), const),
                  pl.BlockSpec((1, LANES), const)],
        out_specs=pl.BlockSpec((w, D_MODEL), row),
        scratch_shapes=[pltpu.VMEM((n_kv_tiles, w, LANES), F32),
                        pltpu.VMEM((n_kv_tiles, w, LANES), F32)],
    )
    return pl.pallas_call(
        _swa_body,
        name="swa_attention",
        out_shape=jax.ShapeDtypeStruct((n, D_MODEL), BF16),
        grid_spec=grid_spec,
        compiler_params=_params("parallel", "arbitrary"),
    )(sinks.astype(F32), q, kv, pos, tile2(q_norm), tile2(k_norm), invf, sgn)


def _gmlp_body(u_ref, v_ref, lg_ref, lbias_ref, ws_ref, bs_ref, o_ref, *, tb):
    c = GM_CHUNK
    gw = GM_WIDTH // GM_GROUPS
    row = lax.broadcasted_iota(jnp.int32, (c, c), 0)
    col = lax.broadcasted_iota(jnp.int32, (c, c), 1)
    causal = row >= col
    wcs = [jnp.where(causal, ws_ref[g], 0.0).astype(BF16) for g in range(GM_GROUPS)]
    bs = bs_ref[...]
    for r in range(0, tb, c):
        v = v_ref[r:r + c, :].astype(F32)
        mu = jnp.mean(v, axis=-1, keepdims=True)
        vc = v - mu
        var = jnp.mean(vc * vc, axis=-1, keepdims=True)
        vn = (vc * lax.rsqrt(var + EPS) * lg_ref[...] + lbias_ref[...]).astype(BF16)
        for g in range(GM_GROUPS):
            sl = slice(g * gw, (g + 1) * gw)
            mixed = _dot(wcs[g], vn[:, sl]) + bs[:, g:g + 1]
            o_ref[r:r + c, sl] = (u_ref[r:r + c, sl].astype(F32) * mixed).astype(o_ref.dtype)


def gmlp_spatial(z, ln_g, ln_b, w_s, b_s, tb=512):
    n = z.shape[0]
    d = GM_WIDTH
    tb = _row_tile(n, tb)
    const2 = lambda i: (0, 0)
    return pl.pallas_call(
        functools.partial(_gmlp_body, tb=tb),
        name="gmlp_spatial",
        out_shape=jax.ShapeDtypeStruct((n, d), BF16),
        grid=(n // tb,),
        in_specs=[pl.BlockSpec((tb, d), lambda i: (i, 0)),
                  pl.BlockSpec((tb, d), lambda i: (i, 1)),
                  pl.BlockSpec((1, d), const2),
                  pl.BlockSpec((1, d), const2),
                  pl.BlockSpec((GM_GROUPS, GM_CHUNK, GM_CHUNK), lambda i: (0, 0, 0)),
                  pl.BlockSpec((GM_CHUNK, GM_GROUPS), const2)],
        out_specs=pl.BlockSpec((tb, d), lambda i: (i, 0)),
        compiler_params=_params("parallel"),
    )(z, z, ln_g.reshape(1, d).astype(F32), ln_b.reshape(1, d).astype(F32),
      w_s.astype(F32), b_s.T.astype(F32))


def _gdn_body(qkv_ref, z_ref, ba_ref, cw_ref, alog_ref, dtb_ref, on_ref, o_ref,
              s_ref, x_ref, y_ref, *, tb):
    c = GD_CHUNK
    hd = GD_HEAD_DIM
    halo = 8
    nq = GD_QK_HEADS * hd
    it = pl.program_id(1)

    @pl.when(it == 0)
    def _():
        s_ref[...] = jnp.zeros_like(s_ref)
        x_ref[0:halo, :] = jnp.zeros((halo, GD_QKV), F32)

    @pl.when(it > 0)
    def _():
        x_ref[0:halo, :] = x_ref[tb:tb + halo, :]

    x_ref[halo:halo + tb, :] = qkv_ref[...].astype(F32)
    acc = None
    for jj in range(GD_CONV):
        start = halo - (GD_CONV - 1) + jj
        term = cw_ref[jj:jj + 1, :] * x_ref[start:start + tb, :]
        acc = term if acc is None else acc + term
    y_ref[...] = _silu(acc)
    for h in range(2 * GD_QK_HEADS):
        sl = slice(h * hd, (h + 1) * hd)
        a = y_ref[:, sl]
        a = a * lax.rsqrt(jnp.sum(a * a, axis=-1, keepdims=True) + EPS)
        if h < GD_QK_HEADS:
            a = a * (hd ** -0.5)
        y_ref[:, sl] = a

    row = lax.broadcasted_iota(jnp.int32, (c, c), 0)
    col = lax.broadcasted_iota(jnp.int32, (c, c), 1)
    tri = row >= col
    tri_strict = row > col
    tri_bf = tri.astype(BF16)
    eye = (row == col).astype(F32)
    lane = lax.broadcasted_iota(jnp.int32, (1, LANES), 1)
    neg_a = -jnp.exp(alog_ref[...])
    on = on_ref[...]

    heads = range(GD_V_HEADS)
    rep = GD_V_HEADS // GD_QK_HEADS
    col = lambda a, hv: a[:, hv:hv + 1]
    gcol = lambda a, hv: a[:, GD_V_HEADS + hv:GD_V_HEADS + hv + 1]
    n_par = GD_PAR_CHUNKS

    def group(gi, carry):
        rows = [pl.ds(pl.multiple_of((gi * n_par + j) * c, c), c) for j in range(n_par)]
        par = range(n_par)
        jobs = [(j, hv) for j in par for hv in heads]
        ba = [ba_ref[r, :] for r in rows]
        beta_all = [jax.nn.sigmoid(a) for a in ba]
        in_rate_lanes = (lane >= GD_V_HEADS) & (lane < 2 * GD_V_HEADS)
        g_all = [jnp.where(in_rate_lanes, neg_a * jax.nn.softplus(a + dtb_ref[...]), 0.0) for a in ba]
        gam_all = [_dot_split(tri_bf, a) for a in g_all]
        gam_rows = [a.T for a in gam_all]
        e_gam_all = [jnp.exp(a) for a in gam_all]
        gam_last_all = [a[c - 1:c, :] for a in gam_all]
        e_end_all = [jnp.exp(b - a) for a, b in zip(gam_all, gam_last_all)]
        d_end_all = [jnp.exp(a) for a in gam_last_all]
        q = [[y_ref[r, hk * hd:(hk + 1) * hd] for hk in range(GD_QK_HEADS)] for r in rows]
        k = [[y_ref[r, nq + hk * hd:nq + (hk + 1) * hd] for hk in range(GD_QK_HEADS)] for r in rows]
        k_bf = [[a.astype(BF16) for a in kj] for kj in k]
        kk = [[lax.dot_general(a, a, NT_DIMS, preferred_element_type=F32) for a in kj] for kj in k_bf]
        qk = [[lax.dot_general(a.astype(BF16), b, NT_DIMS, preferred_element_type=F32)
               for a, b in zip(q[j], k_bf[j])] for j in par]
        decay = [jnp.exp(jnp.where(
            tri, gcol(gam_all[j], hv) - gam_rows[j][GD_V_HEADS + hv:GD_V_HEADS + hv + 1, :], -jnp.inf))
            for j, hv in jobs]
        p = [-jnp.where(tri_strict, kk[j][hv // rep] * decay[n] * col(beta_all[j], hv), 0.0)
             for n, (j, hv) in enumerate(jobs)]
        rhs = [jnp.concatenate(
            [y_ref[rows[j], 2 * nq + hv * hd:2 * nq + (hv + 1) * hd] * col(beta_all[j], hv),
             k[j][hv // rep] * (col(beta_all[j], hv) * gcol(e_gam_all[j], hv))], axis=-1).astype(BF16)
            for j, hv in jobs]
        u = [eye + a for a in p]
        p_bf = [a.astype(BF16) for a in p]
        n_steps = c.bit_length() - 1
        for step in range(1, n_steps):
            p_bf = [_dot(pb, pb).astype(BF16) for pb in p_bf]
            u = [a + _dot(a.astype(BF16), pb) for a, pb in zip(u, p_bf)]
        sol = [_dot(a.astype(BF16), b) for a, b in zip(u, rhs)]
        q_dec = [(q[j][hv // rep] * gcol(e_gam_all[j], hv)).astype(BF16) for j, hv in jobs]
        qk_dec = [(qk[j][hv // rep] * decay[n]).astype(BF16) for n, (j, hv) in enumerate(jobs)]
        k_end = [(k[j][hv // rep] * gcol(e_end_all[j], hv)).astype(BF16) for j, hv in jobs]
        gate = [_silu(z_ref[rows[j], hv * hd:(hv + 1) * hd].astype(F32)) for j, hv in jobs]
        for j in par:
            at = lambda hv: j * GD_V_HEADS + hv
            s_old = [s_ref[hv] for hv in heads]
            s_bf = [a.astype(BF16) for a in s_old]
            nv_bf = [(sol[at(hv)][:, :hd] - _dot(sol[at(hv)][:, hd:].astype(BF16), s_bf[hv])).astype(BF16)
                     for hv in heads]
            o = [_dot(q_dec[at(hv)], s_bf[hv]) + _dot(qk_dec[at(hv)], nv_bf[hv]) for hv in heads]
            s_new = [s_old[hv] * gcol(d_end_all[j], hv) + lax.dot_general(
                k_end[at(hv)], nv_bf[hv], TN_DIMS, preferred_element_type=F32) for hv in heads]
            for hv in heads:
                s_ref[hv] = s_new[hv]
            outs = [_rms(o[hv], on) * gate[at(hv)] for hv in heads]
            o_ref[rows[j], :] = jnp.concatenate(outs, axis=-1).astype(o_ref.dtype)
        return carry

    lax.fori_loop(0, tb // (c * n_par), group, 0)


def gdn_recurrence(qkv, z, ba, conv_w, a_log, dt_bias, o_norm, batch, seq, tb=256):
    n = batch * seq
    tb = _row_tile(seq, tb)
    nt = seq // tb
    pad_heads = lambda a: jnp.zeros((1, LANES), F32).at[0, GD_V_HEADS:2 * GD_V_HEADS].set(a.astype(F32))
    row = lambda b, t: (b * nt + t, 0)
    const = lambda b, t: (0, 0)
    return pl.pallas_call(
        functools.partial(_gdn_body, tb=tb),
        name="gdn_recurrence",
        out_shape=jax.ShapeDtypeStruct((n, GD_Z), BF16),
        grid=(batch, nt),
        in_specs=[pl.BlockSpec((tb, GD_QKV), row),
                  pl.BlockSpec((tb, GD_Z), row),
                  pl.BlockSpec((tb, LANES), row),
                  pl.BlockSpec((GD_CONV, GD_QKV), const),
                  pl.BlockSpec((1, LANES), const),
                  pl.BlockSpec((1, LANES), const),
                  pl.BlockSpec((1, GD_HEAD_DIM), const)],
        out_specs=pl.BlockSpec((tb, GD_Z), row),
        scratch_shapes=[pltpu.VMEM((GD_V_HEADS, GD_HEAD_DIM, GD_HEAD_DIM), F32),
                        pltpu.VMEM((tb + 16, GD_QKV), F32),
                        pltpu.VMEM((tb, GD_QKV), F32)],
        compiler_params=_params("parallel", "arbitrary"),
    )(qkv, z, ba, conv_w.astype(F32), pad_heads(a_log), pad_heads(dt_bias),
      o_norm.reshape(1, GD_HEAD_DIM).astype(F32))


def _router_body(x_ref, g_ref, wr_ref, o_ref):
    h = _rms(x_ref[...], g_ref[...])
    w = wr_ref[...]
    h_hi = h.astype(BF16)
    h_lo = (h - h_hi.astype(F32)).astype(BF16)
    w_hi = w.astype(BF16)
    w_lo = (w - w_hi.astype(F32)).astype(BF16)
    logits = _dot(h_hi, w_hi) + (_dot(h_lo, w_hi) + _dot(h_hi, w_lo))
    lane = lax.broadcasted_iota(jnp.int32, logits.shape, 1)
    logits = jnp.where(lane < N_EXPERTS, logits, -jnp.inf)
    m1 = jnp.max(logits, axis=-1, keepdims=True)
    i1 = jnp.min(jnp.where(logits == m1, lane, LANES), axis=-1, keepdims=True)
    rest = jnp.where(lane == i1, -jnp.inf, logits)
    m2 = jnp.max(rest, axis=-1, keepdims=True)
    i2 = jnp.min(jnp.where(rest == m2, lane, LANES), axis=-1, keepdims=True)
    e2 = jnp.exp(m2 - m1)
    w1 = 1.0 / (1.0 + e2)
    w2 = e2 / (1.0 + e2)
    out = jnp.where(lane == 0, i1.astype(F32),
                    jnp.where(lane == 1, i2.astype(F32),
                              jnp.where(lane == 2, w1, jnp.where(lane == 3, w2, 0.0))))
    o_ref[...] = out


def moe_router(x, g, router, tm=512):
    n, d = x.shape
    tm = _row_tile(n, tm)
    wr = jnp.zeros((d, LANES), F32).at[:, :N_EXPERTS].set(router.astype(F32))
    return pl.pallas_call(
        _router_body,
        name="moe_router",
        out_shape=jax.ShapeDtypeStruct((n, LANES), F32),
        grid=(n // tm,),
        in_specs=[pl.BlockSpec((tm, d), lambda i: (i, 0)),
                  pl.BlockSpec((1, d), lambda i: (0, 0)),
                  pl.BlockSpec((d, LANES), lambda i: (0, 0))],
        out_specs=pl.BlockSpec((tm, LANES), lambda i: (i, 0)),
        compiler_params=_params("parallel"),
    )(x, g.reshape(1, d).astype(F32), wr)


def _row_copy(src_ref, src_row, dst_ref, dst_row, sem):
    return pltpu.make_async_copy(src_ref.at[pl.ds(src_row, 1), :], dst_ref.at[pl.ds(dst_row, 1), :], sem)


def _expert_changed(te_ref, i):
    return (i == 0) | (te_ref[i] != te_ref[jnp.maximum(i - 1, 0)])


def _moe_up_body(te_ref, src_ref, src_next_ref, x_ref, g_ref, wg_ref, wu_ref, o_ref,
                 wg_bf, wu_bf, xbuf, sem, *, chunk, tm):
    j, i = pl.program_id(0), pl.program_id(1)
    n_i = pl.num_programs(1)
    step = j * n_i + i
    last_step = pl.num_programs(0) * n_i - 1
    slot = step % 2
    f = o_ref.shape[1]

    def row_gather(idx_ref, r, dst_slot):
        return pltpu.make_async_copy(x_ref.at[pl.ds(idx_ref[0, 0, r], 1), :],
                                     xbuf.at[dst_slot, pl.ds(r, 1), :], sem.at[dst_slot])

    def tile_wait(dst_slot):
        pltpu.make_async_copy(x_ref.at[pl.ds(0, tm), :], xbuf.at[dst_slot], sem.at[dst_slot]).wait()

    @pl.when(_expert_changed(te_ref, i))
    def _():
        for c in range(0, f, chunk):
            wg_bf[:, c:c + chunk] = wg_ref[:, c:c + chunk].astype(BF16)
            wu_bf[:, c:c + chunk] = wu_ref[:, c:c + chunk].astype(BF16)

    @pl.when(step == 0)
    def _():
        def start(grp, carry):
            r0 = pl.multiple_of(grp * SUBLANES, SUBLANES)
            for u in range(SUBLANES):
                row_gather(src_ref, r0 + u, 0).start(priority=u % 2)
            return carry

        lax.fori_loop(0, tm // SUBLANES, start, 0)

    tile_wait(slot)
    h = _rms(xbuf[slot], g_ref[...]).astype(BF16)
    n_chunks = f // chunk
    for ci in range(n_chunks):
        c = ci * chunk
        a = _dot(h, wg_bf[:, c:c + chunk])
        b = _dot(h, wu_bf[:, c:c + chunk])
        o_ref[:, c:c + chunk] = (_silu(a) * b).astype(o_ref.dtype)
        for r in range(ci * tm // n_chunks, (ci + 1) * tm // n_chunks):
            row_gather(src_next_ref, r, 1 - slot).start(priority=r % 2)

    @pl.when(step == last_step)
    def _():
        tile_wait(1 - slot)


def moe_up(x, src, g, wg, wu, layer, tile_expert, tm, fsplit=2, chunk=256):
    n, d = x.shape
    p = src.shape[0]
    f = wg.shape[3]
    fb = f // fsplit
    n_tiles = p // tm
    assert fb % chunk == 0 and p % tm == 0
    src3 = src.reshape(n_tiles, 1, tm)
    w_spec = pl.BlockSpec((None, None, d, fb), lambda j, i, te: (layer, te[i], 0, j))
    grid_spec = pltpu.PrefetchScalarGridSpec(
        num_scalar_prefetch=1,
        grid=(fsplit, n_tiles),
        in_specs=[pl.BlockSpec((1, 1, tm), lambda j, i, te: (i, 0, 0), memory_space=pltpu.SMEM),
                  pl.BlockSpec((1, 1, tm), lambda j, i, te: ((i + 1) % n_tiles, 0, 0), memory_space=pltpu.SMEM),
                  pl.BlockSpec(memory_space=pl.ANY),
                  pl.BlockSpec((1, d), lambda j, i, te: (0, 0)),
                  w_spec, w_spec],
        out_specs=pl.BlockSpec((tm, fb), lambda j, i, te: (i, j)),
        scratch_shapes=[pltpu.VMEM((d, fb), BF16), pltpu.VMEM((d, fb), BF16),
                        pltpu.VMEM((2, tm, d), F32), pltpu.SemaphoreType.DMA((2,))],
    )
    return pl.pallas_call(
        functools.partial(_moe_up_body, chunk=chunk, tm=tm),
        name="moe_up",
        out_shape=jax.ShapeDtypeStruct((p, f), BF16),
        grid_spec=grid_spec,
        compiler_params=_params("arbitrary", "arbitrary"),
    )(tile_expert, src3, src3, x, g.reshape(1, d).astype(F32), wg, wu)


def _moe_down_body(te_ref, nt_ref, a_ref, wd_ref, o_ref, wd_bf, *, chunk):
    i = pl.program_id(0)

    @pl.when(_expert_changed(te_ref, i))
    def _():
        for r in range(0, wd_bf.shape[0], chunk):
            wd_bf[r:r + chunk, :] = wd_ref[r:r + chunk, :].astype(BF16)

    @pl.when(i < nt_ref[0])
    def _():
        o_ref[...] = _dot(a_ref[...], wd_bf[...])

    @pl.when(i >= nt_ref[0])
    def _():
        o_ref[...] = jnp.zeros_like(o_ref)


def moe_down(act, wd, layer, tile_expert, n_tiles_used, tm, chunk=512):
    p, f = act.shape
    d = wd.shape[3]
    assert f % chunk == 0
    grid_spec = pltpu.PrefetchScalarGridSpec(
        num_scalar_prefetch=2,
        grid=(p // tm,),
        in_specs=[pl.BlockSpec((tm, f), lambda i, te, nt: (i, 0)),
                  pl.BlockSpec((None, None, f, d), lambda i, te, nt: (layer, te[i], 0, 0))],
        out_specs=pl.BlockSpec((tm, d), lambda i, te, nt: (i, 0)),
        scratch_shapes=[pltpu.VMEM((f, d), BF16)],
    )
    return pl.pallas_call(
        functools.partial(_moe_down_body, chunk=chunk),
        name="moe_down",
        out_shape=jax.ShapeDtypeStruct((p, d), F32),
        grid_spec=grid_spec,
        compiler_params=_params("arbitrary"),
    )(tile_expert, n_tiles_used, act, wd)


def _combine_body(pos_ref, pos_next_ref, x_ref, r_ref, ys_ref, o_ref, buf_ref, sem, *, rt):
    i = pl.program_id(0)
    slot = i % 2

    def gather(p_ref, s):
        def start(g, carry):
            r0 = pl.multiple_of(g * SUBLANES, SUBLANES)
            for u in range(SUBLANES):
                for k in range(TOP_K):
                    _row_copy(ys_ref, p_ref[0, 0, TOP_K * (r0 + u) + k], buf_ref.at[s, k], r0 + u,
                              sem.at[s]).start(priority=k)
            return carry

        lax.fori_loop(0, rt // SUBLANES, start, 0)

    @pl.when(i == 0)
    def _():
        gather(pos_ref, 0)

    @pl.when(i + 1 < pl.num_programs(0))
    def _():
        gather(pos_next_ref, 1 - slot)

    for k in range(TOP_K):
        pltpu.make_async_copy(ys_ref.at[pl.ds(0, rt), :], buf_ref.at[slot, k], sem.at[slot]).wait()
    r = r_ref[...]
    o_ref[...] = x_ref[...] + r[:, 2:3] * buf_ref[slot, 0] + r[:, 3:4] * buf_ref[slot, 1]


def moe_combine(x, route, ys, pos, rt=256):
    n, d = x.shape
    rt = _row_tile(n, rt)
    steps = n // rt
    pos3 = pos.reshape(steps, 1, TOP_K * rt)
    grid_spec = pltpu.PrefetchScalarGridSpec(
        num_scalar_prefetch=0,
        grid=(steps,),
        in_specs=[pl.BlockSpec((1, 1, TOP_K * rt), lambda i: (i, 0, 0), memory_space=pltpu.SMEM),
                  pl.BlockSpec((1, 1, TOP_K * rt), lambda i: (jnp.minimum(i + 1, steps - 1), 0, 0),
                               memory_space=pltpu.SMEM),
                  pl.BlockSpec((rt, d), lambda i: (i, 0)),
                  pl.BlockSpec((rt, LANES), lambda i: (i, 0)),
                  pl.BlockSpec(memory_space=pl.ANY)],
        out_specs=pl.BlockSpec((rt, d), lambda i: (i, 0)),
        scratch_shapes=[pltpu.VMEM((2, TOP_K, rt, d), F32), pltpu.SemaphoreType.DMA((2,))],
    )
    return pl.pallas_call(
        functools.partial(_combine_body, rt=rt),
        name="moe_combine",
        out_shape=jax.ShapeDtypeStruct((n, d), F32),
        grid_spec=grid_spec,
        compiler_params=_params("arbitrary"),
    )(pos3, pos3, x, route, ys)


def moe_ffn(x, g, router, wg, wu, wd, layer, tm=512):
    n, d = x.shape
    route = moe_router(x, g, router)
    expert = route[:, :TOP_K].astype(jnp.int32).reshape(n * TOP_K)
    onehot = (expert[:, None] == jnp.arange(N_EXPERTS, dtype=jnp.int32)[None, :]).astype(jnp.int32)
    csum = jnp.cumsum(onehot, axis=0)
    rank = jnp.sum((csum - onehot) * onehot, axis=1)
    counts = csum[-1]
    padded = ((counts + tm - 1) // tm) * tm
    ends = jnp.cumsum(padded)
    starts = ends - padded
    pos = (jnp.sum(starts[None, :] * onehot, axis=1) + rank).astype(jnp.int32)
    n_rows = n * TOP_K + N_EXPERTS * tm
    n_tiles = n_rows // tm
    tile_start = jnp.arange(n_tiles, dtype=jnp.int32) * tm
    tile_expert = jnp.minimum(jnp.sum((tile_start[:, None] >= ends[None, :]).astype(jnp.int32), axis=1),
                              N_EXPERTS - 1).astype(jnp.int32)
    n_tiles_used = (ends[-1:] // tm).astype(jnp.int32)
    token = jnp.arange(n * TOP_K, dtype=jnp.int32) // TOP_K
    src = jnp.zeros((n_rows,), jnp.int32).at[pos].set(token, unique_indices=True)

    act = moe_up(x, src, g, wg, wu, layer, tile_expert, tm)
    ys = moe_down(act, wd, layer, tile_expert, n_tiles_used, tm)
    return moe_combine(x, route, ys, pos)


def hgrn2_layer(x, g, w_in, o_norm, w_out, lower_bound, batch, seq):
    (qfig,) = norm_matmul(x, g, w_in.astype(BF16), [(4 * D_MODEL, BF16)])
    o = hgrn_recurrence(qfig, lower_bound, o_norm, batch, seq)
    return matmul_residual(o, w_out.astype(BF16), x)


def swa_layer(x, g, positions, w_in, q_norm, k_norm, sinks, w_out, batch, seq):
    q, kv = norm_matmul(x, g, w_in.astype(BF16),
                        [(D_MODEL, BF16), (2 * SW_KV_HEADS * SW_HEAD_DIM, BF16)])
    pos = positions.astype(F32).reshape(batch * seq, 1)
    o = swa_attention(q, kv, pos, q_norm, k_norm, sinks, batch, seq)
    return matmul_residual(o, w_out.astype(BF16), x)


def gmlp_layer(x, g, w_in, b_in, ln_g, ln_b, w_s, b_s, w_out):
    (z,) = norm_matmul(x, g, w_in.astype(BF16), [(2 * GM_WIDTH, BF16)], bias=b_in, act="gelu")
    o = gmlp_spatial(z, ln_g, ln_b, w_s, b_s)
    return matmul_residual(o, w_out.astype(BF16), x)


def gdn_layer(x, g, w_in, conv_w, a_log, dt_bias, o_norm, w_out, batch, seq):
    d = x.shape[1]
    n_small = w_in.shape[1] - GD_QKV - GD_Z
    w_main = w_in[:, :GD_QKV + GD_Z].astype(BF16)
    w_small = jnp.zeros((d, LANES), F32).at[:, :n_small].set(w_in[:, GD_QKV + GD_Z:].astype(F32))
    w_all = jnp.concatenate([w_main, w_small.astype(BF16)], axis=1)
    qkv, z, ba = norm_matmul(x, g, w_all, [(GD_QKV, BF16), (GD_Z, BF16), (LANES, F32)])
    o = gdn_recurrence(qkv, z, ba, conv_w, a_log, dt_bias, o_norm, batch, seq)
    return matmul_residual(o, w_out.astype(BF16), x)


def dense_ffn(x, g, w_gate, w_up, w_down):
    act = swiglu_up(x, g, w_gate.astype(BF16), w_up.astype(BF16))
    return matmul_residual(act, w_down.astype(BF16), x)


def kernel(x, positions, mix_norm, ffn_norm, hgrn_lb_logits, hgrn_w_in, hgrn_o_norm, hgrn_w_out,
           swa_w_in, swa_q_norm, swa_k_norm, swa_sinks, swa_w_out,
           gmlp_w_in, gmlp_b_in, gmlp_v_ln_g, gmlp_v_ln_b, gmlp_w_s, gmlp_b_s, gmlp_w_out,
           gdn_w_in, gdn_conv_w, gdn_a_log, gdn_dt_bias, gdn_o_norm, gdn_w_out,
           dense_w_gate, dense_w_up, dense_w_down,
           moe_router, moe_w_gate, moe_w_up, moe_w_down):
    batch, seq, d = x.shape
    depth = mix_norm.shape[0]
    lower_bounds = jnp.cumsum(jax.nn.softmax(hgrn_lb_logits.astype(F32), axis=0), axis=0)
    h = x.reshape(batch * seq, d)
    for i in range(depth):
        kind, j = i % 4, i // 4
        if kind == 0:
            h = hgrn2_layer(h, mix_norm[i], hgrn_w_in[j], hgrn_o_norm[j], hgrn_w_out[j],
                            lower_bounds[i], batch, seq)
        elif kind == 1:
            h = swa_layer(h, mix_norm[i], positions, swa_w_in[j], swa_q_norm[j], swa_k_norm[j],
                          swa_sinks[j], swa_w_out[j], batch, seq)
        elif kind == 2:
            h = gmlp_layer(h, mix_norm[i], gmlp_w_in[j], gmlp_b_in[j], gmlp_v_ln_g[j], gmlp_v_ln_b[j],
                           gmlp_w_s[j], gmlp_b_s[j], gmlp_w_out[j])
        else:
            h = gdn_layer(h, mix_norm[i], gdn_w_in[j], gdn_conv_w[j], gdn_a_log[j], gdn_dt_bias[j],
                          gdn_o_norm[j], gdn_w_out[j], batch, seq)
        if i % 2 == 0:
            h = dense_ffn(h, ffn_norm[i], dense_w_gate[i // 2], dense_w_up[i // 2], dense_w_down[i // 2])
        else:
            h = moe_ffn(h, ffn_norm[i], moe_router[i // 2], moe_w_gate, moe_w_up, moe_w_down, i // 2)
    return h.reshape(batch, seq, d)
```

```python
import functools

import jax
import jax.numpy as jnp
from jax import lax
from jax.experimental import pallas as pl
from jax.experimental.pallas import tpu as pltpu

F32 = jnp.float32
BF16 = jnp.bfloat16
EPS = 1e-6

D_MODEL = 1024
LANES = 128
SUBLANES = 8
HG_HEAD_DIM = 128
HG_HEADS = D_MODEL // HG_HEAD_DIM
HG_CHUNK = 32
SW_HEAD_DIM = 64
SW_HEADS = D_MODEL // SW_HEAD_DIM
SW_KV_HEADS = 4
SW_GROUP = SW_HEADS // SW_KV_HEADS
SW_WINDOW = 128
ROPE_THETA = 10000.0
GM_WIDTH = D_MODEL
GM_GROUPS = 8
GM_CHUNK = 128
GD_HEAD_DIM = 128
GD_QK_HEADS = D_MODEL // GD_HEAD_DIM
GD_V_HEADS = 2 * GD_QK_HEADS
GD_CONV = 4
GD_CHUNK = 64
GD_QKV = 2 * GD_QK_HEADS * GD_HEAD_DIM + GD_V_HEADS * GD_HEAD_DIM
GD_Z = GD_V_HEADS * GD_HEAD_DIM
GD_PAR_CHUNKS = 4
N_EXPERTS = 8
TOP_K = 2

V7X_VMEM_LIMIT_BYTES = 56 * 1024 * 1024

NT_DIMS = (((1,), (1,)), ((), ()))
TN_DIMS = (((0,), (0,)), ((), ()))


def _params(*sem):
    return pltpu.CompilerParams(dimension_semantics=sem, vmem_limit_bytes=V7X_VMEM_LIMIT_BYTES)


def _dot(a, b):
    return jnp.dot(a, b, preferred_element_type=F32)


def _dot_split(a_bf, x):
    hi = x.astype(BF16)
    lo = (x - hi.astype(F32)).astype(BF16)
    return _dot(a_bf, hi) + _dot(a_bf, lo)


def _rms(x, g):
    return x * lax.rsqrt(jnp.mean(x * x, axis=-1, keepdims=True) + EPS) * g


def _silu(x):
    return x * jax.nn.sigmoid(x)


def _row_tile(n, want):
    t = min(n, want)
    assert n % t == 0, (n, t)
    return t


def _norm_mm_body(x_ref, g_ref, w_ref, *rest, widths, chunk, use_bias, act):
    if use_bias:
        b_ref, out_refs = rest[0], rest[1:]
    else:
        b_ref, out_refs = None, rest
    h = _rms(x_ref[...], g_ref[...]).astype(BF16)
    off = 0
    for o_ref, width in zip(out_refs, widths):
        for c in range(0, width, chunk):
            cw = min(chunk, width - c)
            y = _dot(h, w_ref[:, off + c:off + c + cw])
            if use_bias:
                y = y + b_ref[:, off + c:off + c + cw]
            if act == "gelu":
                y = 0.5 * y * (1.0 + lax.erf(y * (0.5 ** 0.5)))
            o_ref[:, c:c + cw] = y.astype(o_ref.dtype)
        off += width


def norm_matmul(x, g, w, outs, bias=None, act=None, tm=512, chunk=512):
    n, d = x.shape
    m = w.shape[1]
    widths = tuple(o[0] for o in outs)
    assert sum(widths) == m
    tm = _row_tile(n, tm)
    in_specs = [pl.BlockSpec((tm, d), lambda i: (i, 0)),
                pl.BlockSpec((1, d), lambda i: (0, 0)),
                pl.BlockSpec((d, m), lambda i: (0, 0))]
    args = [x, g.reshape(1, d).astype(F32), w]
    if bias is not None:
        in_specs.append(pl.BlockSpec((1, m), lambda i: (0, 0)))
        args.append(bias.reshape(1, m).astype(F32))
    body = functools.partial(_norm_mm_body, widths=widths, chunk=chunk,
                             use_bias=bias is not None, act=act)
    return pl.pallas_call(
        body,
        name="norm_matmul",
        out_shape=[jax.ShapeDtypeStruct((n, wd), dt) for wd, dt in outs],
        grid=(n // tm,),
        in_specs=in_specs,
        out_specs=[pl.BlockSpec((tm, wd), lambda i: (i, 0)) for wd, _ in outs],
        compiler_params=_params("parallel"),
    )(*args)


def _mm_res_body(a_ref, w_ref, x_ref, o_ref):
    o_ref[...] = x_ref[...] + _dot(a_ref[...], w_ref[...])


def matmul_residual(a, w, x, tm=512):
    n, k = a.shape
    d = w.shape[1]
    tm = _row_tile(n, tm)
    return pl.pallas_call(
        _mm_res_body,
        name="matmul_residual",
        out_shape=jax.ShapeDtypeStruct((n, d), F32),
        grid=(n // tm,),
        in_specs=[pl.BlockSpec((tm, k), lambda i: (i, 0)),
                  pl.BlockSpec((k, d), lambda i: (0, 0)),
                  pl.BlockSpec((tm, d), lambda i: (i, 0))],
        out_specs=pl.BlockSpec((tm, d), lambda i: (i, 0)),
        compiler_params=_params("parallel"),
    )(a, w, x)


def _swiglu_up_body(x_ref, g_ref, wg_ref, wu_ref, o_ref, *, chunk):
    h = _rms(x_ref[...], g_ref[...]).astype(BF16)
    f = o_ref.shape[1]
    for c in range(0, f, chunk):
        a = _dot(h, wg_ref[:, c:c + chunk])
        b = _dot(h, wu_ref[:, c:c + chunk])
        o_ref[:, c:c + chunk] = (_silu(a) * b).astype(o_ref.dtype)


def swiglu_up(x, g, wg, wu, tm=512, chunk=256):
    n, d = x.shape
    f = wg.shape[1]
    assert f % chunk == 0
    tm = _row_tile(n, tm)
    return pl.pallas_call(
        functools.partial(_swiglu_up_body, chunk=chunk),
        name="swiglu_up",
        out_shape=jax.ShapeDtypeStruct((n, f), BF16),
        grid=(n // tm,),
        in_specs=[pl.BlockSpec((tm, d), lambda i: (i, 0)),
                  pl.BlockSpec((1, d), lambda i: (0, 0)),
                  pl.BlockSpec((d, f), lambda i: (0, 0)),
                  pl.BlockSpec((d, f), lambda i: (0, 0))],
        out_specs=pl.BlockSpec((tm, f), lambda i: (i, 0)),
        compiler_params=_params("parallel"),
    )(x, g.reshape(1, d).astype(F32), wg, wu)


def _hgrn_body(q_ref, f_ref, i_ref, g_ref, lb_ref, on_ref, x_ref, wo_ref, o_ref, s_ref, m_ref, *, tb):
    c32 = HG_CHUNK
    hd = HG_HEAD_DIM

    @pl.when(pl.program_id(1) == 0)
    def _():
        s_ref[...] = jnp.zeros_like(s_ref)

    lb = lb_ref[...]
    on = on_ref[...]
    row = lax.broadcasted_iota(jnp.int32, (c32, c32), 0)
    col = lax.broadcasted_iota(jnp.int32, (c32, c32), 1)
    causal = row >= col
    tri_bf = causal.astype(BF16)

    def chunk(c, carry):
        r0 = pl.multiple_of(c * c32, c32)
        rows = pl.ds(r0, c32)
        forget = lb + (1.0 - lb) * jax.nn.sigmoid(f_ref[rows, :].astype(F32))
        b = _dot_split(tri_bf, jnp.log(forget))
        b_mid = b[c32 // 2 - 1:c32 // 2, :]
        b_last = b[c32 - 1:c32, :]
        qs = _silu(q_ref[rows, :].astype(F32))
        k = 1.0 - forget
        v_bf = i_ref[rows, :]
        qa = (qs * jnp.exp(b - b_mid)).astype(BF16)
        ka = (k * jnp.exp(b_mid - b)).astype(BF16)
        q_in = (qs * jnp.exp(b)).astype(BF16)
        k_end = (k * jnp.exp(b_last - b)).astype(BF16)
        d_end = jnp.exp(b_last)
        d_cols = jnp.concatenate([d_end[:, h * hd:(h + 1) * hd] for h in range(HG_HEADS)], axis=0).T
        gate = _silu(g_ref[rows, :].astype(F32))
        heads = range(HG_HEADS)
        sls = [slice(h * hd, (h + 1) * hd) for h in heads]
        s_old = [s_ref[h] for h in heads]
        sc = [lax.dot_general(qa[:, sl], ka[:, sl], NT_DIMS, preferred_element_type=F32) for sl in sls]
        inter = [_dot(q_in[:, sl], s.astype(BF16)) for sl, s in zip(sls, s_old)]
        upd = [lax.dot_general(k_end[:, sl], v_bf[:, sl], TN_DIMS, preferred_element_type=F32) for sl in sls]
        sc_bf = [jnp.where(causal, a, 0.0).astype(BF16) for a in sc]
        o = [_dot(a, v_bf[:, sl]) + b for a, sl, b in zip(sc_bf, sls, inter)]
        for h in heads:
            s_ref[h] = s_old[h] * d_cols[:, h:h + 1] + upd[h]
        outs = [_rms(a, on[:, sl]) for a, sl in zip(o, sls)]
        m_ref[rows, :] = (jnp.concatenate(outs, axis=-1) * gate).astype(m_ref.dtype)
        return carry

    lax.fori_loop(0, tb // c32, chunk, 0, unroll=8)
    o_ref[...] = x_ref[...] + _dot(m_ref[...], wo_ref[...])


def hgrn_recurrence(qfig, lb, o_norm, x, w_out, batch, seq, tb=256):
    n = batch * seq
    d = D_MODEL
    tb = _row_tile(seq, tb)
    nt = seq // tb
    col_spec = lambda j: pl.BlockSpec((tb, d), lambda b, t, j=j: (b * nt + t, j))
    return pl.pallas_call(
        functools.partial(_hgrn_body, tb=tb),
        name="hgrn_recurrence",
        out_shape=jax.ShapeDtypeStruct((n, d), F32),
        grid=(batch, nt),
        in_specs=[col_spec(0), col_spec(1), col_spec(2), col_spec(3),
                  pl.BlockSpec((1, d), lambda b, t: (0, 0)),
                  pl.BlockSpec((1, d), lambda b, t: (0, 0)),
                  col_spec(0),
                  pl.BlockSpec((d, d), lambda b, t: (0, 0))],
        out_specs=pl.BlockSpec((tb, d), lambda b, t: (b * nt + t, 0)),
        scratch_shapes=[pltpu.VMEM((HG_HEADS, HG_HEAD_DIM, HG_HEAD_DIM), F32), pltpu.VMEM((tb, d), BF16)],
        compiler_params=_params("parallel", "arbitrary"),
    )(qfig, qfig, qfig, qfig, lb.reshape(1, d).astype(F32), o_norm.reshape(1, d).astype(F32), x, w_out)


def _swa_body(sink_ref, q_ref, kv_ref, pos_ref, qn_ref, kn_ref, invf_ref, sgn_ref, x_ref, wo_ref, o_ref,
              kprev_ref, vprev_ref, m_ref):
    w = SW_WINDOW
    hd = SW_HEAD_DIM
    j = pl.program_id(1)

    @pl.when(j == 0)
    def _():
        kprev_ref[...] = jnp.zeros_like(kprev_ref)
        vprev_ref[...] = jnp.zeros_like(vprev_ref)

    lane = lax.broadcasted_iota(jnp.int32, (w, LANES), 1)
    first_half = (lane % hd) < (hd // 2)
    low_head = lane < hd
    ang = pos_ref[...] * invf_ref[...]
    cos = jnp.cos(ang)
    sin_signed = jnp.sin(ang) * sgn_ref[...]
    gi = lax.broadcasted_iota(jnp.int32, (LANES, LANES), 0) // hd
    gj = lax.broadcasted_iota(jnp.int32, (LANES, LANES), 1) // hd
    head_mean = jnp.where(gi == gj, 1.0 / hd, 0.0).astype(BF16)

    def norm_rope(x, gain):
        xx = x * x
        hi = xx.astype(BF16)
        lo = (xx - hi.astype(F32)).astype(BF16)
        ms = _dot(hi, head_mean) + _dot(lo, head_mean)
        x = x * lax.rsqrt(ms + EPS) * gain
        partner = jnp.where(first_half, pltpu.roll(x, LANES - hd // 2, 1), pltpu.roll(x, hd // 2, 1))
        return x * cos + partner * sin_signed

    qn = qn_ref[...]
    kn = kn_ref[...]
    kv = kv_ref[...].astype(F32)
    n_kv_tiles = SW_KV_HEADS * hd // LANES
    k_cur = [norm_rope(kv[:, t * LANES:(t + 1) * LANES], kn) for t in range(n_kv_tiles)]
    v_cur = [kv[:, (n_kv_tiles + t) * LANES:(n_kv_tiles + t + 1) * LANES] for t in range(n_kv_tiles)]
    k_all = [jnp.concatenate([kprev_ref[t], k_cur[t]], axis=0) for t in range(n_kv_tiles)]
    v_all = [jnp.concatenate([vprev_ref[t], v_cur[t]], axis=0) for t in range(n_kv_tiles)]
    for t in range(n_kv_tiles):
        kprev_ref[t] = k_cur[t]
        vprev_ref[t] = v_cur[t]

    lane2 = lax.broadcasted_iota(jnp.int32, (2 * w, LANES), 1)
    low2 = lane2 < hd
    c_idx = lax.broadcasted_iota(jnp.int32, (w, 2 * w), 0)
    r_idx = lax.broadcasted_iota(jnp.int32, (w, 2 * w), 1)
    rel = c_idx + w - r_idx
    valid = (rel >= 0) & (rel < w) & ((j - 1) * w + r_idx >= 0)
    scale = hd ** -0.5

    k_half, v_half = [], []
    for kvh in range(SW_KV_HEADS):
        kt, vt = k_all[kvh // 2], v_all[kvh // 2]
        kt_sw = pltpu.roll(kt, hd, 1)
        vt_sw = pltpu.roll(vt, hd, 1)
        lo_src, hi_src = ((kt, vt), (kt_sw, vt_sw)) if kvh % 2 == 0 else ((kt_sw, vt_sw), (kt, vt))
        k_half.append((jnp.where(low2, lo_src[0], 0.0).astype(BF16), jnp.where(low2, 0.0, hi_src[0]).astype(BF16)))
        v_half.append((jnp.where(low2, lo_src[1], 0.0).astype(BF16), jnp.where(low2, 0.0, hi_src[1]).astype(BF16)))
    n_pairs = SW_HEADS // 2
    qp = [(norm_rope(q_ref[:, t * LANES:(t + 1) * LANES].astype(F32), qn) * scale).astype(BF16)
          for t in range(n_pairs)]
    heads = range(SW_HEADS)
    s = [jnp.where(valid, lax.dot_general(qp[h // 2], k_half[h // SW_GROUP][h % 2], NT_DIMS,
                                          preferred_element_type=F32), -jnp.inf) for h in heads]
    m = [jnp.maximum(jnp.max(s[h], axis=-1, keepdims=True), sink_ref[h]) for h in heads]
    p = [jnp.exp(s[h] - m[h]) for h in heads]
    denom = [jnp.sum(p[h], axis=-1, keepdims=True) + jnp.exp(sink_ref[h] - m[h]) for h in heads]
    pv = [_dot(p[h].astype(BF16), v_half[h // SW_GROUP][h % 2]) / denom[h] for h in heads]
    for t in range(n_pairs):
        m_ref[:, t * LANES:(t + 1) * LANES] = (pv[2 * t] + pv[2 * t + 1]).astype(m_ref.dtype)
    o_ref[...] = x_ref[...] + _dot(m_ref[...], wo_ref[...])


def swa_attention(q, kv, pos, q_norm, k_norm, sinks, x, w_out, batch, seq):
    n = batch * seq
    w = SW_WINDOW
    nb = seq // w
    hd = SW_HEAD_DIM
    lane = jnp.arange(LANES)
    inv_freq = ROPE_THETA ** (-jnp.arange(0, hd, 2, dtype=F32) / hd)
    invf = inv_freq[lane % (hd // 2)].reshape(1, LANES)
    sgn = jnp.where((lane % hd) < hd // 2, -1.0, 1.0).astype(F32).reshape(1, LANES)
    tile2 = lambda g: jnp.tile(g.astype(F32), LANES // hd).reshape(1, LANES)
    n_kv_tiles = SW_KV_HEADS * hd // LANES
    row = lambda b, jj, s: (b * nb + jj, 0)
    const = lambda b, jj, s: (0, 0)
    grid_spec = pltpu.PrefetchScalarGridSpec(
        num_scalar_prefetch=1,
        grid=(batch, nb),
        in_specs=[pl.BlockSpec((w, D_MODEL), row),
                  pl.BlockSpec((w, 2 * SW_KV_HEADS * hd), row),
                  pl.BlockSpec((w, 1), row),
                  pl.BlockSpec((1, LANES), const),
                  pl.BlockSpec((1, LANES), const),
                  pl.BlockSpec((1, LANES), const),
                  pl.BlockSpec((1, LANES), const),
                  pl.BlockSpec((w, D_MODEL), row),
                  pl.BlockSpec((D_MODEL, D_MODEL), const)],
        out_specs=pl.BlockSpec((w, D_MODEL), row),
        scratch_shapes=[pltpu.VMEM((n_kv_tiles, w, LANES), F32),
                        pltpu.VMEM((n_kv_tiles, w, LANES), F32),
                        pltpu.VMEM((w, D_MODEL), BF16)],
    )
    return pl.pallas_call(
        _swa_body,
        name="swa_attention",
        out_shape=jax.ShapeDtypeStruct((n, D_MODEL), F32),
        grid_spec=grid_spec,
        compiler_params=_params("parallel", "arbitrary"),
    )(sinks.astype(F32), q, kv, pos, tile2(q_norm), tile2(k_norm), invf, sgn, x, w_out)


def _gmlp_body(u_ref, v_ref, lg_ref, lbias_ref, ws_ref, bs_ref, x_ref, wo_ref, o_ref, m_ref, *, tb):
    c = GM_CHUNK
    gw = GM_WIDTH // GM_GROUPS
    row = lax.broadcasted_iota(jnp.int32, (c, c), 0)
    col = lax.broadcasted_iota(jnp.int32, (c, c), 1)
    causal = row >= col
    wcs = [jnp.where(causal, ws_ref[g], 0.0).astype(BF16) for g in range(GM_GROUPS)]
    bs = bs_ref[...]
    for r in range(0, tb, c):
        v = v_ref[r:r + c, :].astype(F32)
        mu = jnp.mean(v, axis=-1, keepdims=True)
        vc = v - mu
        var = jnp.mean(vc * vc, axis=-1, keepdims=True)
        vn = (vc * lax.rsqrt(var + EPS) * lg_ref[...] + lbias_ref[...]).astype(BF16)
        for g in range(GM_GROUPS):
            sl = slice(g * gw, (g + 1) * gw)
            mixed = _dot(wcs[g], vn[:, sl]) + bs[:, g:g + 1]
            m_ref[r:r + c, sl] = (u_ref[r:r + c, sl].astype(F32) * mixed).astype(m_ref.dtype)
    o_ref[...] = x_ref[...] + _dot(m_ref[...], wo_ref[...])


def gmlp_spatial(z, ln_g, ln_b, w_s, b_s, x, w_out, tb=512):
    n = z.shape[0]
    d = GM_WIDTH
    tb = _row_tile(n, tb)
    const2 = lambda i: (0, 0)
    return pl.pallas_call(
        functools.partial(_gmlp_body, tb=tb),
        name="gmlp_spatial",
        out_shape=jax.ShapeDtypeStruct((n, d), F32),
        grid=(n // tb,),
        in_specs=[pl.BlockSpec((tb, d), lambda i: (i, 0)),
                  pl.BlockSpec((tb, d), lambda i: (i, 1)),
                  pl.BlockSpec((1, d), const2),
                  pl.BlockSpec((1, d), const2),
                  pl.BlockSpec((GM_GROUPS, GM_CHUNK, GM_CHUNK), lambda i: (0, 0, 0)),
                  pl.BlockSpec((GM_CHUNK, GM_GROUPS), const2),
                  pl.BlockSpec((tb, d), lambda i: (i, 0)),
                  pl.BlockSpec((d, d), const2)],
        out_specs=pl.BlockSpec((tb, d), lambda i: (i, 0)),
        scratch_shapes=[pltpu.VMEM((tb, d), BF16)],
        compiler_params=_params("parallel"),
    )(z, z, ln_g.reshape(1, d).astype(F32), ln_b.reshape(1, d).astype(F32),
      w_s.astype(F32), b_s.T.astype(F32), x, w_out)


def _gdn_body(y_ref, z_ref, ba_ref, alog_ref, dtb_ref, on_ref, x_ref, wo_ref, o_ref, s_ref, m_ref, *, tb):
    c = GD_CHUNK
    hd = GD_HEAD_DIM
    nq = GD_QK_HEADS * hd

    @pl.when(pl.program_id(1) == 0)
    def _():
        s_ref[...] = jnp.zeros_like(s_ref)

    row = lax.broadcasted_iota(jnp.int32, (c, c), 0)
    col = lax.broadcasted_iota(jnp.int32, (c, c), 1)
    tri = row >= col
    tri_strict = row > col
    tri_bf = tri.astype(BF16)
    eye = (row == col).astype(F32)
    lane = lax.broadcasted_iota(jnp.int32, (1, LANES), 1)
    neg_a = -jnp.exp(alog_ref[...])
    on = on_ref[...]

    heads = range(GD_V_HEADS)
    rep = GD_V_HEADS // GD_QK_HEADS
    col = lambda a, hv: a[:, hv:hv + 1]
    gcol = lambda a, hv: a[:, GD_V_HEADS + hv:GD_V_HEADS + hv + 1]
    n_par = GD_PAR_CHUNKS

    def group(gi, carry):
        rows = [pl.ds(pl.multiple_of((gi * n_par + j) * c, c), c) for j in range(n_par)]
        par = range(n_par)
        jobs = [(j, hv) for j in par for hv in heads]
        ba = [ba_ref[r, :] for r in rows]
        beta_all = [jax.nn.sigmoid(a) for a in ba]
        in_rate_lanes = (lane >= GD_V_HEADS) & (lane < 2 * GD_V_HEADS)
        g_all = [jnp.where(in_rate_lanes, neg_a * jax.nn.softplus(a + dtb_ref[...]), 0.0) for a in ba]
        gam_all = [_dot_split(tri_bf, a) for a in g_all]
        gam_rows = [a.T for a in gam_all]
        e_gam_all = [jnp.exp(a) for a in gam_all]
        gam_last_all = [a[c - 1:c, :] for a in gam_all]
        e_end_all = [jnp.exp(b - a) for a, b in zip(gam_all, gam_last_all)]
        d_end_all = [jnp.exp(a) for a in gam_last_all]
        q_bf = [[y_ref[r, hk * hd:(hk + 1) * hd] for hk in range(GD_QK_HEADS)] for r in rows]
        k_bf = [[y_ref[r, nq + hk * hd:nq + (hk + 1) * hd] for hk in range(GD_QK_HEADS)] for r in rows]
        q = [[a.astype(F32) for a in qj] for qj in q_bf]
        k = [[a.astype(F32) for a in kj] for kj in k_bf]
        kk = [[lax.dot_general(a, a, NT_DIMS, preferred_element_type=F32) for a in kj] for kj in k_bf]
        qk = [[lax.dot_general(a, b, NT_DIMS, preferred_element_type=F32)
               for a, b in zip(q_bf[j], k_bf[j])] for j in par]
        decay = [jnp.exp(jnp.where(
            tri, gcol(gam_all[j], hv) - gam_rows[j][GD_V_HEADS + hv:GD_V_HEADS + hv + 1, :], -jnp.inf))
            for j, hv in jobs]
        p = [-jnp.where(tri_strict, kk[j][hv // rep] * decay[n] * col(beta_all[j], hv), 0.0)
             for n, (j, hv) in enumerate(jobs)]
        rhs = [jnp.concatenate(
            [y_ref[rows[j], 2 * nq + hv * hd:2 * nq + (hv + 1) * hd].astype(F32) * col(beta_all[j], hv),
             k[j][hv // rep] * (col(beta_all[j], hv) * gcol(e_gam_all[j], hv))], axis=-1).astype(BF16)
            for j, hv in jobs]
        u = [eye + a for a in p]
        p_bf = [a.astype(BF16) for a in p]
        n_steps = c.bit_length() - 1
        for step in range(1, n_steps):
            p_bf = [_dot(pb, pb).astype(BF16) for pb in p_bf]
            u = [a + _dot(a.astype(BF16), pb) for a, pb in zip(u, p_bf)]
        sol = [_dot(a.astype(BF16), b) for a, b in zip(u, rhs)]
        q_dec = [(q[j][hv // rep] * gcol(e_gam_all[j], hv)).astype(BF16) for j, hv in jobs]
        qk_dec = [(qk[j][hv // rep] * decay[n]).astype(BF16) for n, (j, hv) in enumerate(jobs)]
        k_end = [(k[j][hv // rep] * gcol(e_end_all[j], hv)).astype(BF16) for j, hv in jobs]
        gate = [_silu(z_ref[rows[j], hv * hd:(hv + 1) * hd].astype(F32)) for j, hv in jobs]
        for j in par:
            at = lambda hv: j * GD_V_HEADS + hv
            s_old = [s_ref[hv] for hv in heads]
            s_bf = [a.astype(BF16) for a in s_old]
            nv_bf = [(sol[at(hv)][:, :hd] - _dot(sol[at(hv)][:, hd:].astype(BF16), s_bf[hv])).astype(BF16)
                     for hv in heads]
            o = [_dot(q_dec[at(hv)], s_bf[hv]) + _dot(qk_dec[at(hv)], nv_bf[hv]) for hv in heads]
            s_new = [s_old[hv] * gcol(d_end_all[j], hv) + lax.dot_general(
                k_end[at(hv)], nv_bf[hv], TN_DIMS, preferred_element_type=F32) for hv in heads]
            for hv in heads:
                s_ref[hv] = s_new[hv]
            outs = [_rms(o[hv], on) * gate[at(hv)] for hv in heads]
            m_ref[rows[j], :] = jnp.concatenate(outs, axis=-1).astype(m_ref.dtype)
        return carry

    lax.fori_loop(0, tb // (c * n_par), group, 0)
    o_ref[...] = x_ref[...] + _dot(m_ref[...], wo_ref[...])


def gdn_recurrence(y, z, ba, a_log, dt_bias, o_norm, x, w_out, batch, seq, tb=256):
    n = batch * seq
    tb = _row_tile(seq, tb)
    nt = seq // tb
    pad_heads = lambda a: jnp.zeros((1, LANES), F32).at[0, GD_V_HEADS:2 * GD_V_HEADS].set(a.astype(F32))
    row = lambda b, t: (b * nt + t, 0)
    const = lambda b, t: (0, 0)
    return pl.pallas_call(
        functools.partial(_gdn_body, tb=tb),
        name="gdn_recurrence",
        out_shape=jax.ShapeDtypeStruct((n, D_MODEL), F32),
        grid=(batch, nt),
        in_specs=[pl.BlockSpec((tb, GD_QKV), row),
                  pl.BlockSpec((tb, GD_Z), row),
                  pl.BlockSpec((tb, LANES), row),
                  pl.BlockSpec((1, LANES), const),
                  pl.BlockSpec((1, LANES), const),
                  pl.BlockSpec((1, GD_HEAD_DIM), const),
                  pl.BlockSpec((tb, D_MODEL), row),
                  pl.BlockSpec((GD_Z, D_MODEL), const)],
        out_specs=pl.BlockSpec((tb, D_MODEL), row),
        scratch_shapes=[pltpu.VMEM((GD_V_HEADS, GD_HEAD_DIM, GD_HEAD_DIM), F32), pltpu.VMEM((tb, GD_Z), BF16)],
        compiler_params=_params("parallel", "arbitrary"),
    )(y, z, ba, pad_heads(a_log), pad_heads(dt_bias), o_norm.reshape(1, GD_HEAD_DIM).astype(F32), x, w_out)


def _gdn_in_body(x_ref, g_ref, w_ref, cw_ref, y_ref, z_ref, ba_ref, tail_ref, *, tiles_per_seq, chunk):
    tm = x_ref.shape[0]
    hd = GD_HEAD_DIM
    nq = GD_QK_HEADS * hd
    halo = SUBLANES
    first = pl.program_id(0) % tiles_per_seq == 0
    h = _rms(x_ref[...], g_ref[...]).astype(BF16)
    pres = [_dot(h, w_ref[:, c:c + chunk]) for c in range(0, GD_QKV, chunk)]
    for c, pre in zip(range(0, GD_QKV, chunk), pres):
        cols = slice(c, c + chunk)
        tail = jnp.where(first, 0.0, tail_ref[:, cols])
        tail_ref[:, cols] = pre[tm - halo:tm, :]
        ext = jnp.concatenate([tail, pre], axis=0)
        acc = cw_ref[GD_CONV - 1:GD_CONV, cols] * pre
        for jj in range(GD_CONV - 1):
            back = GD_CONV - 1 - jj
            acc = acc + cw_ref[jj:jj + 1, cols] * ext[halo - back:halo - back + tm, :]
        y = _silu(acc)
        if c < 2 * nq:
            parts = []
            for hh in range(chunk // hd):
                a = y[:, hh * hd:(hh + 1) * hd]
                a = a * lax.rsqrt(jnp.sum(a * a, axis=-1, keepdims=True) + EPS)
                parts.append(a * (hd ** -0.5) if c < nq else a)
            y = jnp.concatenate(parts, axis=-1)
        y_ref[:, cols] = y.astype(y_ref.dtype)
    for c in range(0, GD_Z, chunk):
        z_ref[:, c:c + chunk] = _dot(h, w_ref[:, GD_QKV + c:GD_QKV + c + chunk]).astype(z_ref.dtype)
    ba_ref[...] = _dot(h, w_ref[:, GD_QKV + GD_Z:])


def gdn_in_proj(x, g, w_all, conv_w, seq, tm=256, chunk=256):
    n, d = x.shape
    m = w_all.shape[1]
    tm = _row_tile(seq, tm)
    assert nq_aligned(chunk) and m == GD_QKV + GD_Z + LANES
    const = lambda i: (0, 0)
    row = lambda i: (i, 0)
    return pl.pallas_call(
        functools.partial(_gdn_in_body, tiles_per_seq=seq // tm, chunk=chunk),
        name="gdn_in_proj",
        out_shape=[jax.ShapeDtypeStruct((n, GD_QKV), BF16), jax.ShapeDtypeStruct((n, GD_Z), BF16),
                   jax.ShapeDtypeStruct((n, LANES), F32)],
        grid=(n // tm,),
        in_specs=[pl.BlockSpec((tm, d), row), pl.BlockSpec((1, d), const), pl.BlockSpec((d, m), const),
                  pl.BlockSpec((GD_CONV, GD_QKV), const)],
        out_specs=[pl.BlockSpec((tm, GD_QKV), row), pl.BlockSpec((tm, GD_Z), row), pl.BlockSpec((tm, LANES), row)],
        scratch_shapes=[pltpu.VMEM((SUBLANES, GD_QKV), F32)],
        compiler_params=_params("arbitrary"),
    )(x, g.reshape(1, d).astype(F32), w_all, conv_w.astype(F32))


def nq_aligned(chunk):
    nq = GD_QK_HEADS * GD_HEAD_DIM
    return chunk % GD_HEAD_DIM == 0 and nq % chunk == 0 and GD_QKV % chunk == 0 and GD_Z % chunk == 0


def _router_body(x_ref, g_ref, wr_ref, o_ref):
    h = _rms(x_ref[...], g_ref[...])
    w = wr_ref[...]
    h_hi = h.astype(BF16)
    h_lo = (h - h_hi.astype(F32)).astype(BF16)
    w_hi = w.astype(BF16)
    w_lo = (w - w_hi.astype(F32)).astype(BF16)
    logits = _dot(h_hi, w_hi) + (_dot(h_lo, w_hi) + _dot(h_hi, w_lo))
    lane = lax.broadcasted_iota(jnp.int32, logits.shape, 1)
    logits = jnp.where(lane < N_EXPERTS, logits, -jnp.inf)
    m1 = jnp.max(logits, axis=-1, keepdims=True)
    i1 = jnp.min(jnp.where(logits == m1, lane, LANES), axis=-1, keepdims=True)
    rest = jnp.where(lane == i1, -jnp.inf, logits)
    m2 = jnp.max(rest, axis=-1, keepdims=True)
    i2 = jnp.min(jnp.where(rest == m2, lane, LANES), axis=-1, keepdims=True)
    e2 = jnp.exp(m2 - m1)
    w1 = 1.0 / (1.0 + e2)
    w2 = e2 / (1.0 + e2)
    out = jnp.where(lane == 0, i1.astype(F32),
                    jnp.where(lane == 1, i2.astype(F32),
                              jnp.where(lane == 2, w1, jnp.where(lane == 3, w2, 0.0))))
    o_ref[...] = out


def moe_router(x, g, router, tm=512):
    n, d = x.shape
    tm = _row_tile(n, tm)
    wr = jnp.zeros((d, LANES), F32).at[:, :N_EXPERTS].set(router.astype(F32))
    return pl.pallas_call(
        _router_body,
        name="moe_router",
        out_shape=jax.ShapeDtypeStruct((n, LANES), F32),
        grid=(n // tm,),
        in_specs=[pl.BlockSpec((tm, d), lambda i: (i, 0)),
                  pl.BlockSpec((1, d), lambda i: (0, 0)),
                  pl.BlockSpec((d, LANES), lambda i: (0, 0))],
        out_specs=pl.BlockSpec((tm, LANES), lambda i: (i, 0)),
        compiler_params=_params("parallel"),
    )(x, g.reshape(1, d).astype(F32), wr)


def _row_copy(src_ref, src_row, dst_ref, dst_row, sem):
    return pltpu.make_async_copy(src_ref.at[pl.ds(src_row, 1), :], dst_ref.at[pl.ds(dst_row, 1), :], sem)


def _dispatch_body(ends_ref, padded_ref, pos_ref, x_ref, xs_ref, zero_ref, sem, zero_sem, *, rt, tm):
    @pl.when(pl.program_id(0) == 0)
    def _():
        zero_ref[...] = jnp.zeros_like(zero_ref)

        def zero_tile(row0):
            row0 = row0 if isinstance(row0, int) else pl.multiple_of(row0, tm)
            cp = pltpu.make_async_copy(zero_ref, xs_ref.at[pl.ds(row0, tm), :], zero_sem)
            cp.start()
            cp.wait()

        for e in range(N_EXPERTS):
            pl.when(padded_ref[e] > 0)(functools.partial(zero_tile, ends_ref[e] - tm))
        n_rows = xs_ref.shape[0]
        for back in range(1, N_EXPERTS + 1):
            row0 = n_rows - back * tm
            pl.when(row0 >= ends_ref[N_EXPERTS - 1])(functools.partial(zero_tile, row0))

    def start(g, carry):
        r0 = pl.multiple_of(g * SUBLANES, SUBLANES)
        for u in range(SUBLANES):
            for k in range(TOP_K):
                _row_copy(x_ref, r0 + u, xs_ref, pos_ref[0, 0, TOP_K * (r0 + u) + k], sem).start()
        return carry

    lax.fori_loop(0, rt // SUBLANES, start, 0)
    all_rows = xs_ref.at[pl.ds(0, TOP_K * rt), :]
    pltpu.make_async_copy(all_rows, all_rows, sem).wait()


def moe_dispatch(x, pos, ends, padded, n_rows, tm, rt=256):
    n, d = x.shape
    rt = _row_tile(n, rt)
    grid_spec = pltpu.PrefetchScalarGridSpec(
        num_scalar_prefetch=2,
        grid=(n // rt,),
        in_specs=[pl.BlockSpec((1, 1, TOP_K * rt), lambda i, en, pa: (i, 0, 0), memory_space=pltpu.SMEM),
                  pl.BlockSpec((rt, d), lambda i, en, pa: (i, 0))],
        out_specs=pl.BlockSpec(memory_space=pl.ANY),
        scratch_shapes=[pltpu.VMEM((tm, d), F32), pltpu.SemaphoreType.DMA(()),
                        pltpu.SemaphoreType.DMA(())],
    )
    return pl.pallas_call(
        functools.partial(_dispatch_body, rt=rt, tm=tm),
        name="moe_dispatch",
        out_shape=jax.ShapeDtypeStruct((n_rows, d), F32),
        grid_spec=grid_spec,
        compiler_params=_params("arbitrary"),
    )(ends, padded, pos.reshape(n // rt, 1, TOP_K * rt), x)


def _expert_changed(te_ref, i):
    return (i == 0) | (te_ref[i] != te_ref[jnp.maximum(i - 1, 0)])


def _moe_up_body(te_ref, nt_ref, xs_ref, g_ref, wg_ref, wu_ref, o_ref, wg_bf, wu_bf, *, chunk):
    i = pl.program_id(1)
    f = o_ref.shape[1]

    @pl.when(_expert_changed(te_ref, i))
    def _():
        for c in range(0, f, chunk):
            wg_bf[:, c:c + chunk] = wg_ref[:, c:c + chunk].astype(BF16)
            wu_bf[:, c:c + chunk] = wu_ref[:, c:c + chunk].astype(BF16)

    @pl.when(i < nt_ref[0])
    def _():
        h = _rms(xs_ref[...], g_ref[...]).astype(BF16)
        for c in range(0, f, chunk):
            a = _dot(h, wg_bf[:, c:c + chunk])
            b = _dot(h, wu_bf[:, c:c + chunk])
            o_ref[:, c:c + chunk] = (_silu(a) * b).astype(o_ref.dtype)

    @pl.when(i >= nt_ref[0])
    def _():
        o_ref[...] = jnp.zeros_like(o_ref)


def moe_up(xs, g, wg, wu, layer, tile_expert, n_tiles_used, tm, fsplit=2, chunk=256):
    p, d = xs.shape
    f = wg.shape[3]
    fb = f // fsplit
    assert fb % chunk == 0 and p % tm == 0
    w_spec = pl.BlockSpec((None, None, d, fb), lambda j, i, te, nt: (layer, te[i], 0, j))
    grid_spec = pltpu.PrefetchScalarGridSpec(
        num_scalar_prefetch=2,
        grid=(fsplit, p // tm),
        in_specs=[pl.BlockSpec((tm, d), lambda j, i, te, nt: (jnp.minimum(i, nt[0] - 1), 0)),
                  pl.BlockSpec((1, d), lambda j, i, te, nt: (0, 0)),
                  w_spec, w_spec],
        out_specs=pl.BlockSpec((tm, fb), lambda j, i, te, nt: (i, j)),
        scratch_shapes=[pltpu.VMEM((d, fb), BF16), pltpu.VMEM((d, fb), BF16)],
    )
    return pl.pallas_call(
        functools.partial(_moe_up_body, chunk=chunk),
        name="moe_up",
        out_shape=jax.ShapeDtypeStruct((p, f), BF16),
        grid_spec=grid_spec,
        compiler_params=_params("arbitrary", "arbitrary"),
    )(tile_expert, n_tiles_used, xs, g.reshape(1, d).astype(F32), wg, wu)


def _moe_down_body(te_ref, nt_ref, a_ref, wd_ref, o_ref, wd_bf, *, chunk):
    i = pl.program_id(0)

    @pl.when(_expert_changed(te_ref, i))
    def _():
        for r in range(0, wd_bf.shape[0], chunk):
            wd_bf[r:r + chunk, :] = wd_ref[r:r + chunk, :].astype(BF16)

    @pl.when(i < nt_ref[0])
    def _():
        o_ref[...] = _dot(a_ref[...], wd_bf[...])

    @pl.when(i >= nt_ref[0])
    def _():
        o_ref[...] = jnp.zeros_like(o_ref)


def moe_down(act, wd, layer, tile_expert, n_tiles_used, tm, chunk=512):
    p, f = act.shape
    d = wd.shape[3]
    assert f % chunk == 0
    grid_spec = pltpu.PrefetchScalarGridSpec(
        num_scalar_prefetch=2,
        grid=(p // tm,),
        in_specs=[pl.BlockSpec((tm, f), lambda i, te, nt: (i, 0)),
                  pl.BlockSpec((None, None, f, d), lambda i, te, nt: (layer, te[i], 0, 0))],
        out_specs=pl.BlockSpec((tm, d), lambda i, te, nt: (i, 0)),
        scratch_shapes=[pltpu.VMEM((f, d), BF16)],
    )
    return pl.pallas_call(
        functools.partial(_moe_down_body, chunk=chunk),
        name="moe_down",
        out_shape=jax.ShapeDtypeStruct((p, d), F32),
        grid_spec=grid_spec,
        compiler_params=_params("arbitrary"),
    )(tile_expert, n_tiles_used, act, wd)


def _combine_body(pos_ref, pos_next_ref, x_ref, r_ref, ys_ref, o_ref, buf_ref, sem, *, rt):
    i = pl.program_id(0)
    slot = i % 2

    def gather(p_ref, s):
        def start(g, carry):
            r0 = pl.multiple_of(g * SUBLANES, SUBLANES)
            for u in range(SUBLANES):
                for k in range(TOP_K):
                    _row_copy(ys_ref, p_ref[0, 0, TOP_K * (r0 + u) + k], buf_ref.at[s, k], r0 + u,
                              sem.at[s]).start()
            return carry

        lax.fori_loop(0, rt // SUBLANES, start, 0)

    @pl.when(i == 0)
    def _():
        gather(pos_ref, 0)

    @pl.when(i + 1 < pl.num_programs(0))
    def _():
        gather(pos_next_ref, 1 - slot)

    for k in range(TOP_K):
        pltpu.make_async_copy(ys_ref.at[pl.ds(0, rt), :], buf_ref.at[slot, k], sem.at[slot]).wait()
    r = r_ref[...]
    o_ref[...] = x_ref[...] + r[:, 2:3] * buf_ref[slot, 0] + r[:, 3:4] * buf_ref[slot, 1]


def moe_combine(x, route, ys, pos, rt=256):
    n, d = x.shape
    rt = _row_tile(n, rt)
    steps = n // rt
    pos3 = pos.reshape(steps, 1, TOP_K * rt)
    grid_spec = pltpu.PrefetchScalarGridSpec(
        num_scalar_prefetch=0,
        grid=(steps,),
        in_specs=[pl.BlockSpec((1, 1, TOP_K * rt), lambda i: (i, 0, 0), memory_space=pltpu.SMEM),
                  pl.BlockSpec((1, 1, TOP_K * rt), lambda i: (jnp.minimum(i + 1, steps - 1), 0, 0),
                               memory_space=pltpu.SMEM),
                  pl.BlockSpec((rt, d), lambda i: (i, 0)),
                  pl.BlockSpec((rt, LANES), lambda i: (i, 0)),
                  pl.BlockSpec(memory_space=pl.ANY)],
        out_specs=pl.BlockSpec((rt, d), lambda i: (i, 0)),
        scratch_shapes=[pltpu.VMEM((2, TOP_K, rt, d), F32), pltpu.SemaphoreType.DMA((2,))],
    )
    return pl.pallas_call(
        functools.partial(_combine_body, rt=rt),
        name="moe_combine",
        out_shape=jax.ShapeDtypeStruct((n, d), F32),
        grid_spec=grid_spec,
        compiler_params=_params("arbitrary"),
    )(pos3, pos3, x, route, ys)


def moe_ffn(x, g, router, wg, wu, wd, layer, tm=512):
    n, d = x.shape
    route = moe_router(x, g, router)
    expert = route[:, :TOP_K].astype(jnp.int32).reshape(n * TOP_K)
    onehot = (expert[:, None] == jnp.arange(N_EXPERTS, dtype=jnp.int32)[None, :]).astype(jnp.int32)
    csum = jnp.cumsum(onehot, axis=0)
    rank = jnp.sum((csum - onehot) * onehot, axis=1)
    counts = csum[-1]
    padded = ((counts + tm - 1) // tm) * tm
    ends = jnp.cumsum(padded)
    starts = ends - padded
    pos = (jnp.sum(starts[None, :] * onehot, axis=1) + rank).astype(jnp.int32)
    n_rows = n * TOP_K + N_EXPERTS * tm
    n_tiles = n_rows // tm
    tile_start = jnp.arange(n_tiles, dtype=jnp.int32) * tm
    tile_expert = jnp.minimum(jnp.sum((tile_start[:, None] >= ends[None, :]).astype(jnp.int32), axis=1),
                              N_EXPERTS - 1).astype(jnp.int32)
    n_tiles_used = (ends[-1:] // tm).astype(jnp.int32)

    xs = moe_dispatch(x, pos, ends.astype(jnp.int32), padded.astype(jnp.int32), n_rows, tm)
    act = moe_up(xs, g, wg, wu, layer, tile_expert, n_tiles_used, tm)
    ys = moe_down(act, wd, layer, tile_expert, n_tiles_used, tm)
    return moe_combine(x, route, ys, pos)


def hgrn2_layer(x, g, w_in, o_norm, w_out, lower_bound, batch, seq):
    (qfig,) = norm_matmul(x, g, w_in.astype(BF16), [(4 * D_MODEL, BF16)])
    return hgrn_recurrence(qfig, lower_bound, o_norm, x, w_out.astype(BF16), batch, seq)


def swa_layer(x, g, positions, w_in, q_norm, k_norm, sinks, w_out, batch, seq):
    q, kv = norm_matmul(x, g, w_in.astype(BF16),
                        [(D_MODEL, BF16), (2 * SW_KV_HEADS * SW_HEAD_DIM, BF16)])
    pos = positions.astype(F32).reshape(batch * seq, 1)
    return swa_attention(q, kv, pos, q_norm, k_norm, sinks, x, w_out.astype(BF16), batch, seq)


def gmlp_layer(x, g, w_in, b_in, ln_g, ln_b, w_s, b_s, w_out):
    (z,) = norm_matmul(x, g, w_in.astype(BF16), [(2 * GM_WIDTH, BF16)], bias=b_in, act="gelu")
    return gmlp_spatial(z, ln_g, ln_b, w_s, b_s, x, w_out.astype(BF16))


def gdn_layer(x, g, w_in, conv_w, a_log, dt_bias, o_norm, w_out, batch, seq):
    d = x.shape[1]
    n_small = w_in.shape[1] - GD_QKV - GD_Z
    w_main = w_in[:, :GD_QKV + GD_Z].astype(BF16)
    w_small = jnp.zeros((d, LANES), F32).at[:, :n_small].set(w_in[:, GD_QKV + GD_Z:].astype(F32))
    w_all = jnp.concatenate([w_main, w_small.astype(BF16)], axis=1)
    y, z, ba = gdn_in_proj(x, g, w_all, conv_w, seq)
    return gdn_recurrence(y, z, ba, a_log, dt_bias, o_norm, x, w_out.astype(BF16), batch, seq)


def dense_ffn(x, g, w_gate, w_up, w_down):
    act = swiglu_up(x, g, w_gate.astype(BF16), w_up.astype(BF16))
    return matmul_residual(act, w_down.astype(BF16), x)


def kernel(x, positions, mix_norm, ffn_norm, hgrn_lb_logits, hgrn_w_in, hgrn_o_norm, hgrn_w_out,
           swa_w_in, swa_q_norm, swa_k_norm, swa_sinks, swa_w_out,
           gmlp_w_in, gmlp_b_in, gmlp_v_ln_g, gmlp_v_ln_b, gmlp_w_s, gmlp_b_s, gmlp_w_out,
           gdn_w_in, gdn_conv_w, gdn_a_log, gdn_dt_bias, gdn_o_norm, gdn_w_out,
           dense_w_gate, dense_w_up, dense_w_down,
           moe_router, moe_w_gate, moe_w_up, moe_w_down):
    batch, seq, d = x.shape
    depth = mix_norm.shape[0]
    lower_bounds = jnp.cumsum(jax.nn.softmax(hgrn_lb_logits.astype(F32), axis=0), axis=0)
    h = x.reshape(batch * seq, d)
    for i in range(depth):
        kind, j = i % 4, i // 4
        if kind == 0:
            h = hgrn2_layer(h, mix_norm[i], hgrn_w_in[j], hgrn_o_norm[j], hgrn_w_out[j],
                            lower_bounds[i], batch, seq)
        elif kind == 1:
            h = swa_layer(h, mix_norm[i], positions, swa_w_in[j], swa_q_norm[j], swa_k_norm[j],
                          swa_sinks[j], swa_w_out[j], batch, seq)
        elif kind == 2:
            h = gmlp_layer(h, mix_norm[i], gmlp_w_in[j], gmlp_b_in[j], gmlp_v_ln_g[j], gmlp_v_ln_b[j],
                           gmlp_w_s[j], gmlp_b_s[j], gmlp_w_out[j])
        else:
            h = gdn_layer(h, mix_norm[i], gdn_w_in[j], gdn_conv_w[j], gdn_a_log[j], gdn_dt_bias[j],
                          gdn_o_norm[j], gdn_w_out[j], batch, seq)
        if i % 2 == 0:
            h = dense_ffn(h, ffn_norm[i], dense_w_gate[i // 2], dense_w_up[i // 2], dense_w_down[i // 2])
        else:
            h = moe_ffn(h, ffn_norm[i], moe_router[i // 2], moe_w_gate, moe_w_up, moe_w_down, i // 2)
    return h.reshape(batch, seq, d)
```

```python
import functools

import jax
import jax.numpy as jnp
from jax import lax
from jax.experimental import pallas as pl
from jax.experimental.pallas import tpu as pltpu

F32 = jnp.float32
BF16 = jnp.bfloat16
EPS = 1e-6

D_MODEL = 1024
LANES = 128
SUBLANES = 8
HG_HEAD_DIM = 128
HG_HEADS = D_MODEL // HG_HEAD_DIM
HG_CHUNK = 32
SW_HEAD_DIM = 64
SW_HEADS = D_MODEL // SW_HEAD_DIM
SW_KV_HEADS = 4
SW_GROUP = SW_HEADS // SW_KV_HEADS
SW_WINDOW = 128
ROPE_THETA = 10000.0
GM_WIDTH = D_MODEL
GM_GROUPS = 8
GM_CHUNK = 128
GD_HEAD_DIM = 128
GD_QK_HEADS = D_MODEL // GD_HEAD_DIM
GD_V_HEADS = 2 * GD_QK_HEADS
GD_CONV = 4
GD_CHUNK = 64
GD_QKV = 2 * GD_QK_HEADS * GD_HEAD_DIM + GD_V_HEADS * GD_HEAD_DIM
GD_Z = GD_V_HEADS * GD_HEAD_DIM
GD_PAR_CHUNKS = 4
N_EXPERTS = 8
TOP_K = 2

V7X_VMEM_LIMIT_BYTES = 56 * 1024 * 1024

NT_DIMS = (((1,), (1,)), ((), ()))
TN_DIMS = (((0,), (0,)), ((), ()))


def _params(*sem):
    return pltpu.CompilerParams(dimension_semantics=sem, vmem_limit_bytes=V7X_VMEM_LIMIT_BYTES)


def _dot(a, b):
    return jnp.dot(a, b, preferred_element_type=F32)


def _dot_split(a_bf, x):
    hi = x.astype(BF16)
    lo = (x - hi.astype(F32)).astype(BF16)
    return _dot(a_bf, hi) + _dot(a_bf, lo)


def _rms(x, g):
    return x * lax.rsqrt(jnp.mean(x * x, axis=-1, keepdims=True) + EPS) * g


def _silu(x):
    return x * jax.nn.sigmoid(x)


def _row_tile(n, want):
    t = min(n, want)
    assert n % t == 0, (n, t)
    return t


def _norm_mm_body(x_ref, g_ref, w_ref, *rest, widths, chunk, use_bias, act):
    if use_bias:
        b_ref, out_refs = rest[0], rest[1:]
    else:
        b_ref, out_refs = None, rest
    h = _rms(x_ref[...], g_ref[...]).astype(BF16)
    off = 0
    for o_ref, width in zip(out_refs, widths):
        for c in range(0, width, chunk):
            cw = min(chunk, width - c)
            y = _dot(h, w_ref[:, off + c:off + c + cw])
            if use_bias:
                y = y + b_ref[:, off + c:off + c + cw]
            if act == "gelu":
                y = 0.5 * y * (1.0 + lax.erf(y * (0.5 ** 0.5)))
            o_ref[:, c:c + cw] = y.astype(o_ref.dtype)
        off += width


def norm_matmul(x, g, w, outs, bias=None, act=None, tm=512, chunk=512):
    n, d = x.shape
    m = w.shape[1]
    widths = tuple(o[0] for o in outs)
    assert sum(widths) == m
    tm = _row_tile(n, tm)
    in_specs = [pl.BlockSpec((tm, d), lambda i: (i, 0)),
                pl.BlockSpec((1, d), lambda i: (0, 0)),
                pl.BlockSpec((d, m), lambda i: (0, 0))]
    args = [x, g.reshape(1, d).astype(F32), w]
    if bias is not None:
        in_specs.append(pl.BlockSpec((1, m), lambda i: (0, 0)))
        args.append(bias.reshape(1, m).astype(F32))
    body = functools.partial(_norm_mm_body, widths=widths, chunk=chunk,
                             use_bias=bias is not None, act=act)
    return pl.pallas_call(
        body,
        name="norm_matmul",
        out_shape=[jax.ShapeDtypeStruct((n, wd), dt) for wd, dt in outs],
        grid=(n // tm,),
        in_specs=in_specs,
        out_specs=[pl.BlockSpec((tm, wd), lambda i: (i, 0)) for wd, _ in outs],
        compiler_params=_params("parallel"),
    )(*args)


def _mm_res_body(a_ref, w_ref, x_ref, o_ref):
    o_ref[...] = x_ref[...] + _dot(a_ref[...], w_ref[...])


def matmul_residual(a, w, x, tm=512):
    n, k = a.shape
    d = w.shape[1]
    tm = _row_tile(n, tm)
    return pl.pallas_call(
        _mm_res_body,
        name="matmul_residual",
        out_shape=jax.ShapeDtypeStruct((n, d), F32),
        grid=(n // tm,),
        in_specs=[pl.BlockSpec((tm, k), lambda i: (i, 0)),
                  pl.BlockSpec((k, d), lambda i: (0, 0)),
                  pl.BlockSpec((tm, d), lambda i: (i, 0))],
        out_specs=pl.BlockSpec((tm, d), lambda i: (i, 0)),
        compiler_params=_params("parallel"),
    )(a, w, x)


def _swiglu_up_body(x_ref, g_ref, wg_ref, wu_ref, o_ref, *, chunk):
    h = _rms(x_ref[...], g_ref[...]).astype(BF16)
    f = o_ref.shape[1]
    for c in range(0, f, chunk):
        a = _dot(h, wg_ref[:, c:c + chunk])
        b = _dot(h, wu_ref[:, c:c + chunk])
        o_ref[:, c:c + chunk] = (_silu(a) * b).astype(o_ref.dtype)


def swiglu_up(x, g, wg, wu, tm=512, chunk=256):
    n, d = x.shape
    f = wg.shape[1]
    assert f % chunk == 0
    tm = _row_tile(n, tm)
    return pl.pallas_call(
        functools.partial(_swiglu_up_body, chunk=chunk),
        name="swiglu_up",
        out_shape=jax.ShapeDtypeStruct((n, f), BF16),
        grid=(n // tm,),
        in_specs=[pl.BlockSpec((tm, d), lambda i: (i, 0)),
                  pl.BlockSpec((1, d), lambda i: (0, 0)),
                  pl.BlockSpec((d, f), lambda i: (0, 0)),
                  pl.BlockSpec((d, f), lambda i: (0, 0))],
        out_specs=pl.BlockSpec((tm, f), lambda i: (i, 0)),
        compiler_params=_params("parallel"),
    )(x, g.reshape(1, d).astype(F32), wg, wu)


def _hgrn_body(q_ref, f_ref, i_ref, g_ref, lb_ref, on_ref, x_ref, wo_ref, o_ref, s_ref, m_ref, *, tb):
    c32 = HG_CHUNK
    hd = HG_HEAD_DIM

    @pl.when(pl.program_id(1) == 0)
    def _():
        s_ref[...] = jnp.zeros_like(s_ref)

    lb = lb_ref[...]
    on = on_ref[...]
    row = lax.broadcasted_iota(jnp.int32, (c32, c32), 0)
    col = lax.broadcasted_iota(jnp.int32, (c32, c32), 1)
    causal = row >= col
    tri_bf = causal.astype(BF16)

    def chunk(c, carry):
        r0 = pl.multiple_of(c * c32, c32)
        rows = pl.ds(r0, c32)
        forget = lb + (1.0 - lb) * jax.nn.sigmoid(f_ref[rows, :].astype(F32))
        b = _dot_split(tri_bf, jnp.log(forget))
        b_mid = b[c32 // 2 - 1:c32 // 2, :]
        b_last = b[c32 - 1:c32, :]
        qs = _silu(q_ref[rows, :].astype(F32))
        k = 1.0 - forget
        v_bf = i_ref[rows, :]
        qa = (qs * jnp.exp(b - b_mid)).astype(BF16)
        ka = (k * jnp.exp(b_mid - b)).astype(BF16)
        q_in = (qs * jnp.exp(b)).astype(BF16)
        k_end = (k * jnp.exp(b_last - b)).astype(BF16)
        d_end = jnp.exp(b_last)
        d_cols = jnp.concatenate([d_end[:, h * hd:(h + 1) * hd] for h in range(HG_HEADS)], axis=0).T
        gate = _silu(g_ref[rows, :].astype(F32))
        heads = range(HG_HEADS)
        sls = [slice(h * hd, (h + 1) * hd) for h in heads]
        s_old = [s_ref[h] for h in heads]
        sc = [lax.dot_general(qa[:, sl], ka[:, sl], NT_DIMS, preferred_element_type=F32) for sl in sls]
        inter = [_dot(q_in[:, sl], s.astype(BF16)) for sl, s in zip(sls, s_old)]
        upd = [lax.dot_general(k_end[:, sl], v_bf[:, sl], TN_DIMS, preferred_element_type=F32) for sl in sls]
        sc_bf = [jnp.where(causal, a, 0.0).astype(BF16) for a in sc]
        o = [_dot(a, v_bf[:, sl]) + b for a, sl, b in zip(sc_bf, sls, inter)]
        for h in heads:
            s_ref[h] = s_old[h] * d_cols[:, h:h + 1] + upd[h]
        outs = [_rms(a, on[:, sl]) for a, sl in zip(o, sls)]
        m_ref[rows, :] = (jnp.concatenate(outs, axis=-1) * gate).astype(m_ref.dtype)
        return carry

    lax.fori_loop(0, tb // c32, chunk, 0, unroll=8)
    o_ref[...] = x_ref[...] + _dot(m_ref[...], wo_ref[...])


def hgrn_recurrence(qfig, lb, o_norm, x, w_out, batch, seq, tb=256):
    n = batch * seq
    d = D_MODEL
    tb = _row_tile(seq, tb)
    nt = seq // tb
    col_spec = lambda j: pl.BlockSpec((tb, d), lambda b, t, j=j: (b * nt + t, j))
    return pl.pallas_call(
        functools.partial(_hgrn_body, tb=tb),
        name="hgrn_recurrence",
        out_shape=jax.ShapeDtypeStruct((n, d), F32),
        grid=(batch, nt),
        in_specs=[col_spec(0), col_spec(1), col_spec(2), col_spec(3),
                  pl.BlockSpec((1, d), lambda b, t: (0, 0)),
                  pl.BlockSpec((1, d), lambda b, t: (0, 0)),
                  col_spec(0),
                  pl.BlockSpec((d, d), lambda b, t: (0, 0))],
        out_specs=pl.BlockSpec((tb, d), lambda b, t: (b * nt + t, 0)),
        scratch_shapes=[pltpu.VMEM((HG_HEADS, HG_HEAD_DIM, HG_HEAD_DIM), F32), pltpu.VMEM((tb, d), BF16)],
        compiler_params=_params("parallel", "arbitrary"),
    )(qfig, qfig, qfig, qfig, lb.reshape(1, d).astype(F32), o_norm.reshape(1, d).astype(F32), x, w_out)


def _swa_body(sink_ref, q_ref, kv_ref, pos_ref, qn_ref, kn_ref, invf_ref, sgn_ref, x_ref, wo_ref, o_ref,
              kprev_ref, vprev_ref, m_ref, *, nblk):
    w = SW_WINDOW
    hd = SW_HEAD_DIM
    j = pl.program_id(1)

    @pl.when(j == 0)
    def _():
        kprev_ref[...] = jnp.zeros_like(kprev_ref)
        vprev_ref[...] = jnp.zeros_like(vprev_ref)

    lane = lax.broadcasted_iota(jnp.int32, (nblk * w, LANES), 1)
    first_half = (lane % hd) < (hd // 2)
    low_head = lane < hd
    ang = pos_ref[...] * invf_ref[...]
    cos = jnp.cos(ang)
    sin_signed = jnp.sin(ang) * sgn_ref[...]
    gi = lax.broadcasted_iota(jnp.int32, (LANES, LANES), 0) // hd
    gj = lax.broadcasted_iota(jnp.int32, (LANES, LANES), 1) // hd
    head_mean = jnp.where(gi == gj, 1.0 / hd, 0.0).astype(BF16)

    def norm_rope(x, gain):
        xx = x * x
        hi = xx.astype(BF16)
        lo = (xx - hi.astype(F32)).astype(BF16)
        ms = _dot(hi, head_mean) + _dot(lo, head_mean)
        x = x * lax.rsqrt(ms + EPS) * gain
        partner = jnp.where(first_half, pltpu.roll(x, LANES - hd // 2, 1), pltpu.roll(x, hd // 2, 1))
        return x * cos + partner * sin_signed

    qn = qn_ref[...]
    kn = kn_ref[...]
    kv = kv_ref[...].astype(F32)
    n_kv_tiles = SW_KV_HEADS * hd // LANES
    blocks = range(nblk)
    blk = [slice(b * w, (b + 1) * w) for b in blocks]
    k_cur = [norm_rope(kv[:, t * LANES:(t + 1) * LANES], kn) for t in range(n_kv_tiles)]
    v_cur = [kv[:, (n_kv_tiles + t) * LANES:(n_kv_tiles + t + 1) * LANES] for t in range(n_kv_tiles)]
    k_all = [[jnp.concatenate([kprev_ref[t] if b == 0 else k_cur[t][blk[b - 1]], k_cur[t][blk[b]]], axis=0)
              for t in range(n_kv_tiles)] for b in blocks]
    v_all = [[jnp.concatenate([vprev_ref[t] if b == 0 else v_cur[t][blk[b - 1]], v_cur[t][blk[b]]], axis=0)
              for t in range(n_kv_tiles)] for b in blocks]
    for t in range(n_kv_tiles):
        kprev_ref[t] = k_cur[t][blk[nblk - 1]]
        vprev_ref[t] = v_cur[t][blk[nblk - 1]]

    lane2 = lax.broadcasted_iota(jnp.int32, (2 * w, LANES), 1)
    low2 = lane2 < hd
    c_idx = lax.broadcasted_iota(jnp.int32, (w, 2 * w), 0)
    r_idx = lax.broadcasted_iota(jnp.int32, (w, 2 * w), 1)
    rel = c_idx + w - r_idx
    band = (rel >= 0) & (rel < w)
    valid = [band & ((j * nblk + b - 1) * w + r_idx >= 0) for b in blocks]
    scale = hd ** -0.5

    k_half, v_half = [], []
    for b in blocks:
        kb, vb = [], []
        for kvh in range(SW_KV_HEADS):
            kt, vt = k_all[b][kvh // 2], v_all[b][kvh // 2]
            kt_sw = pltpu.roll(kt, hd, 1)
            vt_sw = pltpu.roll(vt, hd, 1)
            lo_src, hi_src = ((kt, vt), (kt_sw, vt_sw)) if kvh % 2 == 0 else ((kt_sw, vt_sw), (kt, vt))
            kb.append((jnp.where(low2, lo_src[0], 0.0).astype(BF16), jnp.where(low2, 0.0, hi_src[0]).astype(BF16)))
            vb.append((jnp.where(low2, lo_src[1], 0.0).astype(BF16), jnp.where(low2, 0.0, hi_src[1]).astype(BF16)))
        k_half.append(kb)
        v_half.append(vb)
    n_pairs = SW_HEADS // 2
    qp = [(norm_rope(q_ref[:, t * LANES:(t + 1) * LANES].astype(F32), qn) * scale).astype(BF16)
          for t in range(n_pairs)]
    jobs = [(b, h) for b in blocks for h in range(SW_HEADS)]
    s = [jnp.where(valid[b], lax.dot_general(qp[h // 2][blk[b]], k_half[b][h // SW_GROUP][h % 2], NT_DIMS,
                                             preferred_element_type=F32), -jnp.inf) for b, h in jobs]
    m = [jnp.maximum(jnp.max(s[n], axis=-1, keepdims=True), sink_ref[h]) for n, (b, h) in enumerate(jobs)]
    p = [jnp.exp(s[n] - m[n]) for n in range(len(jobs))]
    denom = [jnp.sum(p[n], axis=-1, keepdims=True) + jnp.exp(sink_ref[h] - m[n]) for n, (b, h) in enumerate(jobs)]
    pv = [_dot(p[n].astype(BF16), v_half[b][h // SW_GROUP][h % 2]) / denom[n] for n, (b, h) in enumerate(jobs)]
    for b in blocks:
        for t in range(n_pairs):
            pair = pv[b * SW_HEADS + 2 * t] + pv[b * SW_HEADS + 2 * t + 1]
            m_ref[blk[b], t * LANES:(t + 1) * LANES] = pair.astype(m_ref.dtype)
    o_ref[...] = x_ref[...] + _dot(m_ref[...], wo_ref[...])


def swa_attention(q, kv, pos, q_norm, k_norm, sinks, x, w_out, batch, seq, nblk=4):
    n = batch * seq
    w = nblk * SW_WINDOW
    assert seq % w == 0
    nb = seq // w
    hd = SW_HEAD_DIM
    lane = jnp.arange(LANES)
    inv_freq = ROPE_THETA ** (-jnp.arange(0, hd, 2, dtype=F32) / hd)
    invf = inv_freq[lane % (hd // 2)].reshape(1, LANES)
    sgn = jnp.where((lane % hd) < hd // 2, -1.0, 1.0).astype(F32).reshape(1, LANES)
    tile2 = lambda g: jnp.tile(g.astype(F32), LANES // hd).reshape(1, LANES)
    n_kv_tiles = SW_KV_HEADS * hd // LANES
    row = lambda b, jj, s: (b * nb + jj, 0)
    const = lambda b, jj, s: (0, 0)
    grid_spec = pltpu.PrefetchScalarGridSpec(
        num_scalar_prefetch=1,
        grid=(batch, nb),
        in_specs=[pl.BlockSpec((w, D_MODEL), row),
                  pl.BlockSpec((w, 2 * SW_KV_HEADS * hd), row),
                  pl.BlockSpec((w, 1), row),
                  pl.BlockSpec((1, LANES), const),
                  pl.BlockSpec((1, LANES), const),
                  pl.BlockSpec((1, LANES), const),
                  pl.BlockSpec((1, LANES), const),
                  pl.BlockSpec((w, D_MODEL), row),
                  pl.BlockSpec((D_MODEL, D_MODEL), const)],
        out_specs=pl.BlockSpec((w, D_MODEL), row),
        scratch_shapes=[pltpu.VMEM((n_kv_tiles, SW_WINDOW, LANES), F32),
                        pltpu.VMEM((n_kv_tiles, SW_WINDOW, LANES), F32),
                        pltpu.VMEM((w, D_MODEL), BF16)],
    )
    return pl.pallas_call(
        functools.partial(_swa_body, nblk=nblk),
        name="swa_attention",
        out_shape=jax.ShapeDtypeStruct((n, D_MODEL), F32),
        grid_spec=grid_spec,
        compiler_params=_params("parallel", "arbitrary"),
    )(sinks.astype(F32), q, kv, pos, tile2(q_norm), tile2(k_norm), invf, sgn, x, w_out)


def _gmlp_body(u_ref, v_ref, lg_ref, lbias_ref, ws_ref, bs_ref, x_ref, wo_ref, o_ref, m_ref, *, tb):
    c = GM_CHUNK
    gw = GM_WIDTH // GM_GROUPS
    row = lax.broadcasted_iota(jnp.int32, (c, c), 0)
    col = lax.broadcasted_iota(jnp.int32, (c, c), 1)
    causal = row >= col
    wcs = [jnp.where(causal, ws_ref[g], 0.0).astype(BF16) for g in range(GM_GROUPS)]
    bs = bs_ref[...]
    for r in range(0, tb, c):
        v = v_ref[r:r + c, :].astype(F32)
        mu = jnp.mean(v, axis=-1, keepdims=True)
        vc = v - mu
        var = jnp.mean(vc * vc, axis=-1, keepdims=True)
        vn = (vc * lax.rsqrt(var + EPS) * lg_ref[...] + lbias_ref[...]).astype(BF16)
        for g in range(GM_GROUPS):
            sl = slice(g * gw, (g + 1) * gw)
            mixed = _dot(wcs[g], vn[:, sl]) + bs[:, g:g + 1]
            m_ref[r:r + c, sl] = (u_ref[r:r + c, sl].astype(F32) * mixed).astype(m_ref.dtype)
    o_ref[...] = x_ref[...] + _dot(m_ref[...], wo_ref[...])


def gmlp_spatial(z, ln_g, ln_b, w_s, b_s, x, w_out, tb=512):
    n = z.shape[0]
    d = GM_WIDTH
    tb = _row_tile(n, tb)
    const2 = lambda i: (0, 0)
    return pl.pallas_call(
        functools.partial(_gmlp_body, tb=tb),
        name="gmlp_spatial",
        out_shape=jax.ShapeDtypeStruct((n, d), F32),
        grid=(n // tb,),
        in_specs=[pl.BlockSpec((tb, d), lambda i: (i, 0)),
                  pl.BlockSpec((tb, d), lambda i: (i, 1)),
                  pl.BlockSpec((1, d), const2),
                  pl.BlockSpec((1, d), const2),
                  pl.BlockSpec((GM_GROUPS, GM_CHUNK, GM_CHUNK), lambda i: (0, 0, 0)),
                  pl.BlockSpec((GM_CHUNK, GM_GROUPS), const2),
                  pl.BlockSpec((tb, d), lambda i: (i, 0)),
                  pl.BlockSpec((d, d), const2)],
        out_specs=pl.BlockSpec((tb, d), lambda i: (i, 0)),
        scratch_shapes=[pltpu.VMEM((tb, d), BF16)],
        compiler_params=_params("parallel"),
    )(z, z, ln_g.reshape(1, d).astype(F32), ln_b.reshape(1, d).astype(F32),
      w_s.astype(F32), b_s.T.astype(F32), x, w_out)


def _gdn_body(y_ref, z_ref, ba_ref, alog_ref, dtb_ref, on_ref, x_ref, wo_ref, o_ref, s_ref, m_ref, *, tb):
    c = GD_CHUNK
    hd = GD_HEAD_DIM
    nq = GD_QK_HEADS * hd

    @pl.when(pl.program_id(1) == 0)
    def _():
        s_ref[...] = jnp.zeros_like(s_ref)

    row = lax.broadcasted_iota(jnp.int32, (c, c), 0)
    col = lax.broadcasted_iota(jnp.int32, (c, c), 1)
    tri = row >= col
    tri_strict = row > col
    tri_bf = tri.astype(BF16)
    eye = (row == col).astype(F32)
    lane = lax.broadcasted_iota(jnp.int32, (1, LANES), 1)
    neg_a = -jnp.exp(alog_ref[...])
    on = on_ref[...]

    heads = range(GD_V_HEADS)
    rep = GD_V_HEADS // GD_QK_HEADS
    col = lambda a, hv: a[:, hv:hv + 1]
    gcol = lambda a, hv: a[:, GD_V_HEADS + hv:GD_V_HEADS + hv + 1]
    n_par = GD_PAR_CHUNKS

    def group(gi, carry):
        rows = [pl.ds(pl.multiple_of((gi * n_par + j) * c, c), c) for j in range(n_par)]
        par = range(n_par)
        jobs = [(j, hv) for j in par for hv in heads]
        ba = [ba_ref[r, :] for r in rows]
        beta_all = [jax.nn.sigmoid(a) for a in ba]
        in_rate_lanes = (lane >= GD_V_HEADS) & (lane < 2 * GD_V_HEADS)
        g_all = [jnp.where(in_rate_lanes, neg_a * jax.nn.softplus(a + dtb_ref[...]), 0.0) for a in ba]
        gam_all = [_dot_split(tri_bf, a) for a in g_all]
        gam_rows = [a.T for a in gam_all]
        e_gam_all = [jnp.exp(a) for a in gam_all]
        gam_last_all = [a[c - 1:c, :] for a in gam_all]
        e_end_all = [jnp.exp(b - a) for a, b in zip(gam_all, gam_last_all)]
        d_end_all = [jnp.exp(a) for a in gam_last_all]
        q_bf = [[y_ref[r, hk * hd:(hk + 1) * hd] for hk in range(GD_QK_HEADS)] for r in rows]
        k_bf = [[y_ref[r, nq + hk * hd:nq + (hk + 1) * hd] for hk in range(GD_QK_HEADS)] for r in rows]
        q = [[a.astype(F32) for a in qj] for qj in q_bf]
        k = [[a.astype(F32) for a in kj] for kj in k_bf]
        kk = [[lax.dot_general(a, a, NT_DIMS, preferred_element_type=F32) for a in kj] for kj in k_bf]
        qk = [[lax.dot_general(a, b, NT_DIMS, preferred_element_type=F32)
               for a, b in zip(q_bf[j], k_bf[j])] for j in par]
        decay = [jnp.exp(jnp.where(
            tri, gcol(gam_all[j], hv) - gam_rows[j][GD_V_HEADS + hv:GD_V_HEADS + hv + 1, :], -jnp.inf))
            for j, hv in jobs]
        p = [-jnp.where(tri_strict, kk[j][hv // rep] * decay[n] * col(beta_all[j], hv), 0.0)
             for n, (j, hv) in enumerate(jobs)]
        rhs = [jnp.concatenate(
            [y_ref[rows[j], 2 * nq + hv * hd:2 * nq + (hv + 1) * hd].astype(F32) * col(beta_all[j], hv),
             k[j][hv // rep] * (col(beta_all[j], hv) * gcol(e_gam_all[j], hv))], axis=-1).astype(BF16)
            for j, hv in jobs]
        u = [eye + a for a in p]
        p_bf = [a.astype(BF16) for a in p]
        n_steps = c.bit_length() - 1
        for step in range(1, n_steps):
            p_bf = [_dot(pb, pb).astype(BF16) for pb in p_bf]
            u = [a + _dot(a.astype(BF16), pb) for a, pb in zip(u, p_bf)]
        sol = [_dot(a.astype(BF16), b) for a, b in zip(u, rhs)]
        q_dec = [(q[j][hv // rep] * gcol(e_gam_all[j], hv)).astype(BF16) for j, hv in jobs]
        qk_dec = [(qk[j][hv // rep] * decay[n]).astype(BF16) for n, (j, hv) in enumerate(jobs)]
        k_end = [(k[j][hv // rep] * gcol(e_end_all[j], hv)).astype(BF16) for j, hv in jobs]
        gate = [_silu(z_ref[rows[j], hv * hd:(hv + 1) * hd].astype(F32)) for j, hv in jobs]
        for j in par:
            at = lambda hv: j * GD_V_HEADS + hv
            s_old = [s_ref[hv] for hv in heads]
            s_bf = [a.astype(BF16) for a in s_old]
            nv_bf = [(sol[at(hv)][:, :hd] - _dot(sol[at(hv)][:, hd:].astype(BF16), s_bf[hv])).astype(BF16)
                     for hv in heads]
            o = [_dot(q_dec[at(hv)], s_bf[hv]) + _dot(qk_dec[at(hv)], nv_bf[hv]) for hv in heads]
            s_new = [s_old[hv] * gcol(d_end_all[j], hv) + lax.dot_general(
                k_end[at(hv)], nv_bf[hv], TN_DIMS, preferred_element_type=F32) for hv in heads]
            for hv in heads:
                s_ref[hv] = s_new[hv]
            outs = [_rms(o[hv], on) * gate[at(hv)] for hv in heads]
            m_ref[rows[j], :] = jnp.concatenate(outs, axis=-1).astype(m_ref.dtype)
        return carry

    lax.fori_loop(0, tb // (c * n_par), group, 0)
    o_ref[...] = x_ref[...] + _dot(m_ref[...], wo_ref[...])


def gdn_recurrence(y, z, ba, a_log, dt_bias, o_norm, x, w_out, batch, seq, tb=256):
    n = batch * seq
    tb = _row_tile(seq, tb)
    nt = seq // tb
    pad_heads = lambda a: jnp.zeros((1, LANES), F32).at[0, GD_V_HEADS:2 * GD_V_HEADS].set(a.astype(F32))
    row = lambda b, t: (b * nt + t, 0)
    const = lambda b, t: (0, 0)
    return pl.pallas_call(
        functools.partial(_gdn_body, tb=tb),
        name="gdn_recurrence",
        out_shape=jax.ShapeDtypeStruct((n, D_MODEL), F32),
        grid=(batch, nt),
        in_specs=[pl.BlockSpec((tb, GD_QKV), row),
                  pl.BlockSpec((tb, GD_Z), row),
                  pl.BlockSpec((tb, LANES), row),
                  pl.BlockSpec((1, LANES), const),
                  pl.BlockSpec((1, LANES), const),
                  pl.BlockSpec((1, GD_HEAD_DIM), const),
                  pl.BlockSpec((tb, D_MODEL), row),
                  pl.BlockSpec((GD_Z, D_MODEL), const)],
        out_specs=pl.BlockSpec((tb, D_MODEL), row),
        scratch_shapes=[pltpu.VMEM((GD_V_HEADS, GD_HEAD_DIM, GD_HEAD_DIM), F32), pltpu.VMEM((tb, GD_Z), BF16)],
        compiler_params=_params("parallel", "arbitrary"),
    )(y, z, ba, pad_heads(a_log), pad_heads(dt_bias), o_norm.reshape(1, GD_HEAD_DIM).astype(F32), x, w_out)


def _gdn_in_body(x_ref, g_ref, w_ref, cw_ref, y_ref, z_ref, ba_ref, tail_ref, *, tiles_per_seq, chunk):
    tm = x_ref.shape[0]
    hd = GD_HEAD_DIM
    nq = GD_QK_HEADS * hd
    halo = SUBLANES
    first = pl.program_id(0) % tiles_per_seq == 0
    h = _rms(x_ref[...], g_ref[...]).astype(BF16)
    pres = [_dot(h, w_ref[:, c:c + chunk]) for c in range(0, GD_QKV, chunk)]
    for c, pre in zip(range(0, GD_QKV, chunk), pres):
        cols = slice(c, c + chunk)
        tail = jnp.where(first, 0.0, tail_ref[:, cols])
        tail_ref[:, cols] = pre[tm - halo:tm, :]
        ext = jnp.concatenate([tail, pre], axis=0)
        acc = cw_ref[GD_CONV - 1:GD_CONV, cols] * pre
        for jj in range(GD_CONV - 1):
            back = GD_CONV - 1 - jj
            acc = acc + cw_ref[jj:jj + 1, cols] * ext[halo - back:halo - back + tm, :]
        y = _silu(acc)
        if c < 2 * nq:
            parts = []
            for hh in range(chunk // hd):
                a = y[:, hh * hd:(hh + 1) * hd]
                a = a * lax.rsqrt(jnp.sum(a * a, axis=-1, keepdims=True) + EPS)
                parts.append(a * (hd ** -0.5) if c < nq else a)
            y = jnp.concatenate(parts, axis=-1)
        y_ref[:, cols] = y.astype(y_ref.dtype)
    for c in range(0, GD_Z, chunk):
        z_ref[:, c:c + chunk] = _dot(h, w_ref[:, GD_QKV + c:GD_QKV + c + chunk]).astype(z_ref.dtype)
    ba_ref[...] = _dot(h, w_ref[:, GD_QKV + GD_Z:])


def gdn_in_proj(x, g, w_all, conv_w, seq, tm=256, chunk=256):
    n, d = x.shape
    m = w_all.shape[1]
    tm = _row_tile(seq, tm)
    assert nq_aligned(chunk) and m == GD_QKV + GD_Z + LANES
    const = lambda i: (0, 0)
    row = lambda i: (i, 0)
    return pl.pallas_call(
        functools.partial(_gdn_in_body, tiles_per_seq=seq // tm, chunk=chunk),
        name="gdn_in_proj",
        out_shape=[jax.ShapeDtypeStruct((n, GD_QKV), BF16), jax.ShapeDtypeStruct((n, GD_Z), BF16),
                   jax.ShapeDtypeStruct((n, LANES), F32)],
        grid=(n // tm,),
        in_specs=[pl.BlockSpec((tm, d), row), pl.BlockSpec((1, d), const), pl.BlockSpec((d, m), const),
                  pl.BlockSpec((GD_CONV, GD_QKV), const)],
        out_specs=[pl.BlockSpec((tm, GD_QKV), row), pl.BlockSpec((tm, GD_Z), row), pl.BlockSpec((tm, LANES), row)],
        scratch_shapes=[pltpu.VMEM((SUBLANES, GD_QKV), F32)],
        compiler_params=_params("arbitrary"),
    )(x, g.reshape(1, d).astype(F32), w_all, conv_w.astype(F32))


def nq_aligned(chunk):
    nq = GD_QK_HEADS * GD_HEAD_DIM
    return chunk % GD_HEAD_DIM == 0 and nq % chunk == 0 and GD_QKV % chunk == 0 and GD_Z % chunk == 0


def _router_body(x_ref, g_ref, wr_ref, o_ref):
    h = _rms(x_ref[...], g_ref[...])
    w = wr_ref[...]
    h_hi = h.astype(BF16)
    h_lo = (h - h_hi.astype(F32)).astype(BF16)
    w_hi = w.astype(BF16)
    w_lo = (w - w_hi.astype(F32)).astype(BF16)
    logits = _dot(h_hi, w_hi) + (_dot(h_lo, w_hi) + _dot(h_hi, w_lo))
    lane = lax.broadcasted_iota(jnp.int32, logits.shape, 1)
    logits = jnp.where(lane < N_EXPERTS, logits, -jnp.inf)
    m1 = jnp.max(logits, axis=-1, keepdims=True)
    i1 = jnp.min(jnp.where(logits == m1, lane, LANES), axis=-1, keepdims=True)
    rest = jnp.where(lane == i1, -jnp.inf, logits)
    m2 = jnp.max(rest, axis=-1, keepdims=True)
    i2 = jnp.min(jnp.where(rest == m2, lane, LANES), axis=-1, keepdims=True)
    e2 = jnp.exp(m2 - m1)
    w1 = 1.0 / (1.0 + e2)
    w2 = e2 / (1.0 + e2)
    out = jnp.where(lane == 0, i1.astype(F32),
                    jnp.where(lane == 1, i2.astype(F32),
                              jnp.where(lane == 2, w1, jnp.where(lane == 3, w2, 0.0))))
    o_ref[...] = out


def moe_router(x, g, router, tm=512):
    n, d = x.shape
    tm = _row_tile(n, tm)
    wr = jnp.zeros((d, LANES), F32).at[:, :N_EXPERTS].set(router.astype(F32))
    return pl.pallas_call(
        _router_body,
        name="moe_router",
        out_shape=jax.ShapeDtypeStruct((n, LANES), F32),
        grid=(n // tm,),
        in_specs=[pl.BlockSpec((tm, d), lambda i: (i, 0)),
                  pl.BlockSpec((1, d), lambda i: (0, 0)),
                  pl.BlockSpec((d, LANES), lambda i: (0, 0))],
        out_specs=pl.BlockSpec((tm, LANES), lambda i: (i, 0)),
        compiler_params=_params("parallel"),
    )(x, g.reshape(1, d).astype(F32), wr)


def _row_copy(src_ref, src_row, dst_ref, dst_row, sem):
    return pltpu.make_async_copy(src_ref.at[pl.ds(src_row, 1), :], dst_ref.at[pl.ds(dst_row, 1), :], sem)


def _dispatch_body(ends_ref, padded_ref, pos_ref, x_ref, xs_ref, zero_ref, sem, zero_sem, *, rt, tm):
    @pl.when(pl.program_id(0) == 0)
    def _():
        zero_ref[...] = jnp.zeros_like(zero_ref)

        def zero_tile(row0):
            row0 = row0 if isinstance(row0, int) else pl.multiple_of(row0, tm)
            cp = pltpu.make_async_copy(zero_ref, xs_ref.at[pl.ds(row0, tm), :], zero_sem)
            cp.start()
            cp.wait()

        for e in range(N_EXPERTS):
            pl.when(padded_ref[e] > 0)(functools.partial(zero_tile, ends_ref[e] - tm))
        n_rows = xs_ref.shape[0]
        for back in range(1, N_EXPERTS + 1):
            row0 = n_rows - back * tm
            pl.when(row0 >= ends_ref[N_EXPERTS - 1])(functools.partial(zero_tile, row0))

    def start(g, carry):
        r0 = pl.multiple_of(g * SUBLANES, SUBLANES)
        for u in range(SUBLANES):
            for k in range(TOP_K):
                _row_copy(x_ref, r0 + u, xs_ref, pos_ref[0, 0, TOP_K * (r0 + u) + k], sem).start()
        return carry

    lax.fori_loop(0, rt // SUBLANES, start, 0)
    all_rows = xs_ref.at[pl.ds(0, TOP_K * rt), :]
    pltpu.make_async_copy(all_rows, all_rows, sem).wait()


def moe_dispatch(x, pos, ends, padded, n_rows, tm, rt=256):
    n, d = x.shape
    rt = _row_tile(n, rt)
    grid_spec = pltpu.PrefetchScalarGridSpec(
        num_scalar_prefetch=2,
        grid=(n // rt,),
        in_specs=[pl.BlockSpec((1, 1, TOP_K * rt), lambda i, en, pa: (i, 0, 0), memory_space=pltpu.SMEM),
                  pl.BlockSpec((rt, d), lambda i, en, pa: (i, 0))],
        out_specs=pl.BlockSpec(memory_space=pl.ANY),
        scratch_shapes=[pltpu.VMEM((tm, d), F32), pltpu.SemaphoreType.DMA(()),
                        pltpu.SemaphoreType.DMA(())],
    )
    return pl.pallas_call(
        functools.partial(_dispatch_body, rt=rt, tm=tm),
        name="moe_dispatch",
        out_shape=jax.ShapeDtypeStruct((n_rows, d), F32),
        grid_spec=grid_spec,
        compiler_params=_params("arbitrary"),
    )(ends, padded, pos.reshape(n // rt, 1, TOP_K * rt), x)


def _expert_changed(te_ref, i):
    return (i == 0) | (te_ref[i] != te_ref[jnp.maximum(i - 1, 0)])


def _moe_up_body(te_ref, nt_ref, xs_ref, g_ref, wg_ref, wu_ref, o_ref, wg_bf, wu_bf, *, chunk):
    i = pl.program_id(1)
    f = o_ref.shape[1]

    @pl.when(_expert_changed(te_ref, i))
    def _():
        for c in range(0, f, chunk):
            wg_bf[:, c:c + chunk] = wg_ref[:, c:c + chunk].astype(BF16)
            wu_bf[:, c:c + chunk] = wu_ref[:, c:c + chunk].astype(BF16)

    @pl.when(i < nt_ref[0])
    def _():
        h = _rms(xs_ref[...], g_ref[...]).astype(BF16)
        for c in range(0, f, chunk):
            a = _dot(h, wg_bf[:, c:c + chunk])
            b = _dot(h, wu_bf[:, c:c + chunk])
            o_ref[:, c:c + chunk] = (_silu(a) * b).astype(o_ref.dtype)

    @pl.when(i >= nt_ref[0])
    def _():
        o_ref[...] = jnp.zeros_like(o_ref)


def moe_up(xs, g, wg, wu, layer, tile_expert, n_tiles_used, tm, fsplit=2, chunk=256):
    p, d = xs.shape
    f = wg.shape[3]
    fb = f // fsplit
    assert fb % chunk == 0 and p % tm == 0
    w_spec = pl.BlockSpec((None, None, d, fb), lambda j, i, te, nt: (layer, te[i], 0, j))
    grid_spec = pltpu.PrefetchScalarGridSpec(
        num_scalar_prefetch=2,
        grid=(fsplit, p // tm),
        in_specs=[pl.BlockSpec((tm, d), lambda j, i, te, nt: (jnp.minimum(i, nt[0] - 1), 0)),
                  pl.BlockSpec((1, d), lambda j, i, te, nt: (0, 0)),
                  w_spec, w_spec],
        out_specs=pl.BlockSpec((tm, fb), lambda j, i, te, nt: (i, j)),
        scratch_shapes=[pltpu.VMEM((d, fb), BF16), pltpu.VMEM((d, fb), BF16)],
    )
    return pl.pallas_call(
        functools.partial(_moe_up_body, chunk=chunk),
        name="moe_up",
        out_shape=jax.ShapeDtypeStruct((p, f), BF16),
        grid_spec=grid_spec,
        compiler_params=_params("arbitrary", "arbitrary"),
    )(tile_expert, n_tiles_used, xs, g.reshape(1, d).astype(F32), wg, wu)


def _moe_down_body(te_ref, nt_ref, a_ref, wd_ref, o_ref, wd_bf, *, chunk):
    i = pl.program_id(0)

    @pl.when(_expert_changed(te_ref, i))
    def _():
        for r in range(0, wd_bf.shape[0], chunk):
            wd_bf[r:r + chunk, :] = wd_ref[r:r + chunk, :].astype(BF16)

    @pl.when(i < nt_ref[0])
    def _():
        o_ref[...] = _dot(a_ref[...], wd_bf[...])

    @pl.when(i >= nt_ref[0])
    def _():
        o_ref[...] = jnp.zeros_like(o_ref)


def moe_down(act, wd, layer, tile_expert, n_tiles_used, tm, chunk=512):
    p, f = act.shape
    d = wd.shape[3]
    assert f % chunk == 0
    grid_spec = pltpu.PrefetchScalarGridSpec(
        num_scalar_prefetch=2,
        grid=(p // tm,),
        in_specs=[pl.BlockSpec((tm, f), lambda i, te, nt: (i, 0)),
                  pl.BlockSpec((None, None, f, d), lambda i, te, nt: (layer, te[i], 0, 0))],
        out_specs=pl.BlockSpec((tm, d), lambda i, te, nt: (i, 0)),
        scratch_shapes=[pltpu.VMEM((f, d), BF16)],
    )
    return pl.pallas_call(
        functools.partial(_moe_down_body, chunk=chunk),
        name="moe_down",
        out_shape=jax.ShapeDtypeStruct((p, d), F32),
        grid_spec=grid_spec,
        compiler_params=_params("arbitrary"),
    )(tile_expert, n_tiles_used, act, wd)


def _combine_body(pos_ref, pos_next_ref, x_ref, r_ref, ys_ref, o_ref, buf_ref, sem, *, rt):
    i = pl.program_id(0)
    slot = i % 2

    def gather(p_ref, s):
        def start(g, carry):
            r0 = pl.multiple_of(g * SUBLANES, SUBLANES)
            for u in range(SUBLANES):
                for k in range(TOP_K):
                    _row_copy(ys_ref, p_ref[0, 0, TOP_K * (r0 + u) + k], buf_ref.at[s, k], r0 + u,
                              sem.at[s]).start()
            return carry

        lax.fori_loop(0, rt // SUBLANES, start, 0)

    @pl.when(i == 0)
    def _():
        gather(pos_ref, 0)

    @pl.when(i + 1 < pl.num_programs(0))
    def _():
        gather(pos_next_ref, 1 - slot)

    for k in range(TOP_K):
        pltpu.make_async_copy(ys_ref.at[pl.ds(0, rt), :], buf_ref.at[slot, k], sem.at[slot]).wait()
    r = r_ref[...]
    o_ref[...] = x_ref[...] + r[:, 2:3] * buf_ref[slot, 0] + r[:, 3:4] * buf_ref[slot, 1]


def moe_combine(x, route, ys, pos, rt=256):
    n, d = x.shape
    rt = _row_tile(n, rt)
    steps = n // rt
    pos3 = pos.reshape(steps, 1, TOP_K * rt)
    grid_spec = pltpu.PrefetchScalarGridSpec(
        num_scalar_prefetch=0,
        grid=(steps,),
        in_specs=[pl.BlockSpec((1, 1, TOP_K * rt), lambda i: (i, 0, 0), memory_space=pltpu.SMEM),
                  pl.BlockSpec((1, 1, TOP_K * rt), lambda i: (jnp.minimum(i + 1, steps - 1), 0, 0),
                               memory_space=pltpu.SMEM),
                  pl.BlockSpec((rt, d), lambda i: (i, 0)),
                  pl.BlockSpec((rt, LANES), lambda i: (i, 0)),
                  pl.BlockSpec(memory_space=pl.ANY)],
        out_specs=pl.BlockSpec((rt, d), lambda i: (i, 0)),
        scratch_shapes=[pltpu.VMEM((2, TOP_K, rt, d), F32), pltpu.SemaphoreType.DMA((2,))],
    )
    return pl.pallas_call(
        functools.partial(_combine_body, rt=rt),
        name="moe_combine",
        out_shape=jax.ShapeDtypeStruct((n, d), F32),
        grid_spec=grid_spec,
        compiler_params=_params("arbitrary"),
    )(pos3, pos3, x, route, ys)


def moe_ffn(x, g, router, wg, wu, wd, layer, tm=512):
    n, d = x.shape
    route = moe_router(x, g, router)
    expert = route[:, :TOP_K].astype(jnp.int32).reshape(n * TOP_K)
    onehot = (expert[:, None] == jnp.arange(N_EXPERTS, dtype=jnp.int32)[None, :]).astype(jnp.int32)
    csum = jnp.cumsum(onehot, axis=0)
    rank = jnp.sum((csum - onehot) * onehot, axis=1)
    counts = csum[-1]
    padded = ((counts + tm - 1) // tm) * tm
    ends = jnp.cumsum(padded)
    starts = ends - padded
    pos = (jnp.sum(starts[None, :] * onehot, axis=1) + rank).astype(jnp.int32)
    n_rows = n * TOP_K + N_EXPERTS * tm
    n_tiles = n_rows // tm
    tile_start = jnp.arange(n_tiles, dtype=jnp.int32) * tm
    tile_expert = jnp.minimum(jnp.sum((tile_start[:, None] >= ends[None, :]).astype(jnp.int32), axis=1),
                              N_EXPERTS - 1).astype(jnp.int32)
    n_tiles_used = (ends[-1:] // tm).astype(jnp.int32)

    xs = moe_dispatch(x, pos, ends.astype(jnp.int32), padded.astype(jnp.int32), n_rows, tm)
    act = moe_up(xs, g, wg, wu, layer, tile_expert, n_tiles_used, tm)
    ys = moe_down(act, wd, layer, tile_expert, n_tiles_used, tm)
    return moe_combine(x, route, ys, pos)


def hgrn2_layer(x, g, w_in, o_norm, w_out, lower_bound, batch, seq):
    (qfig,) = norm_matmul(x, g, w_in.astype(BF16), [(4 * D_MODEL, BF16)])
    return hgrn_recurrence(qfig, lower_bound, o_norm, x, w_out.astype(BF16), batch, seq)


def swa_layer(x, g, positions, w_in, q_norm, k_norm, sinks, w_out, batch, seq):
    q, kv = norm_matmul(x, g, w_in.astype(BF16),
                        [(D_MODEL, BF16), (2 * SW_KV_HEADS * SW_HEAD_DIM, BF16)])
    pos = positions.astype(F32).reshape(batch * seq, 1)
    return swa_attention(q, kv, pos, q_norm, k_norm, sinks, x, w_out.astype(BF16), batch, seq)


def gmlp_layer(x, g, w_in, b_in, ln_g, ln_b, w_s, b_s, w_out):
    (z,) = norm_matmul(x, g, w_in.astype(BF16), [(2 * GM_WIDTH, BF16)], bias=b_in, act="gelu")
    return gmlp_spatial(z, ln_g, ln_b, w_s, b_s, x, w_out.astype(BF16))


def gdn_layer(x, g, w_in, conv_w, a_log, dt_bias, o_norm, w_out, batch, seq):
    d = x.shape[1]
    n_small = w_in.shape[1] - GD_QKV - GD_Z
    w_main = w_in[:, :GD_QKV + GD_Z].astype(BF16)
    w_small = jnp.zeros((d, LANES), F32).at[:, :n_small].set(w_in[:, GD_QKV + GD_Z:].astype(F32))
    w_all = jnp.concatenate([w_main, w_small.astype(BF16)], axis=1)
    y, z, ba = gdn_in_proj(x, g, w_all, conv_w, seq)
    return gdn_recurrence(y, z, ba, a_log, dt_bias, o_norm, x, w_out.astype(BF16), batch, seq)


def dense_ffn(x, g, w_gate, w_up, w_down):
    act = swiglu_up(x, g, w_gate.astype(BF16), w_up.astype(BF16))
    return matmul_residual(act, w_down.astype(BF16), x)


def kernel(x, positions, mix_norm, ffn_norm, hgrn_lb_logits, hgrn_w_in, hgrn_o_norm, hgrn_w_out,
           swa_w_in, swa_q_norm, swa_k_norm, swa_sinks, swa_w_out,
           gmlp_w_in, gmlp_b_in, gmlp_v_ln_g, gmlp_v_ln_b, gmlp_w_s, gmlp_b_s, gmlp_w_out,
           gdn_w_in, gdn_conv_w, gdn_a_log, gdn_dt_bias, gdn_o_norm, gdn_w_out,
           dense_w_gate, dense_w_up, dense_w_down,
           moe_router, moe_w_gate, moe_w_up, moe_w_down):
    batch, seq, d = x.shape
    depth = mix_norm.shape[0]
    lower_bounds = jnp.cumsum(jax.nn.softmax(hgrn_lb_logits.astype(F32), axis=0), axis=0)
    h = x.reshape(batch * seq, d)
    for i in range(depth):
        kind, j = i % 4, i // 4
        if kind == 0:
            h = hgrn2_layer(h, mix_norm[i], hgrn_w_in[j], hgrn_o_norm[j], hgrn_w_out[j],
                            lower_bounds[i], batch, seq)
        elif kind == 1:
            h = swa_layer(h, mix_norm[i], positions, swa_w_in[j], swa_q_norm[j], swa_k_norm[j],
                          swa_sinks[j], swa_w_out[j], batch, seq)
        elif kind == 2:
            h = gmlp_layer(h, mix_norm[i], gmlp_w_in[j], gmlp_b_in[j], gmlp_v_ln_g[j], gmlp_v_ln_b[j],
                           gmlp_w_s[j], gmlp_b_s[j], gmlp_w_out[j])
        else:
            h = gdn_layer(h, mix_norm[i], gdn_w_in[j], gdn_conv_w[j], gdn_a_log[j], gdn_dt_bias[j],
                          gdn_o_norm[j], gdn_w_out[j], batch, seq)
        if i % 2 == 0:
            h = dense_ffn(h, ffn_norm[i], dense_w_gate[i // 2], dense_w_up[i // 2], dense_w_down[i // 2])
        else:
            h = moe_ffn(h, ffn_norm[i], moe_router[i // 2], moe_w_gate, moe_w_up, moe_w_down, i // 2)
    return h.reshape(batch, seq, d)
```

```python
import functools

import jax
import jax.numpy as jnp
from jax import lax
from jax.experimental import pallas as pl
from jax.experimental.pallas import tpu as pltpu

F32 = jnp.float32
BF16 = jnp.bfloat16
EPS = 1e-6

D_MODEL = 1024
LANES = 128
SUBLANES = 8
HG_HEAD_DIM = 128
HG_HEADS = D_MODEL // HG_HEAD_DIM
HG_CHUNK = 32
SW_HEAD_DIM = 64
SW_HEADS = D_MODEL // SW_HEAD_DIM
SW_KV_HEADS = 4
SW_GROUP = SW_HEADS // SW_KV_HEADS
SW_WINDOW = 128
ROPE_THETA = 10000.0
GM_WIDTH = D_MODEL
GM_GROUPS = 8
GM_CHUNK = 128
GD_HEAD_DIM = 128
GD_QK_HEADS = D_MODEL // GD_HEAD_DIM
GD_V_HEADS = 2 * GD_QK_HEADS
GD_CONV = 4
GD_CHUNK = 64
GD_QKV = 2 * GD_QK_HEADS * GD_HEAD_DIM + GD_V_HEADS * GD_HEAD_DIM
GD_Z = GD_V_HEADS * GD_HEAD_DIM
GD_PAR_CHUNKS = 4
N_EXPERTS = 8
TOP_K = 2

V7X_VMEM_LIMIT_BYTES = 56 * 1024 * 1024

NT_DIMS = (((1,), (1,)), ((), ()))
TN_DIMS = (((0,), (0,)), ((), ()))


def _params(*sem):
    return pltpu.CompilerParams(dimension_semantics=sem, vmem_limit_bytes=V7X_VMEM_LIMIT_BYTES)


def _dot(a, b):
    return jnp.dot(a, b, preferred_element_type=F32)


def _dot_split(a_bf, x):
    hi = x.astype(BF16)
    lo = (x - hi.astype(F32)).astype(BF16)
    return _dot(a_bf, hi) + _dot(a_bf, lo)


def _rms(x, g):
    return x * lax.rsqrt(jnp.mean(x * x, axis=-1, keepdims=True) + EPS) * g


def _silu(x):
    return x * jax.nn.sigmoid(x)


def _row_tile(n, want):
    t = min(n, want)
    assert n % t == 0, (n, t)
    return t


def _norm_mm_body(x_ref, g_ref, w_ref, *rest, widths, chunk, use_bias, act):
    if use_bias:
        b_ref, out_refs = rest[0], rest[1:]
    else:
        b_ref, out_refs = None, rest
    h = _rms(x_ref[...], g_ref[...]).astype(BF16)
    off = 0
    for o_ref, width in zip(out_refs, widths):
        for c in range(0, width, chunk):
            cw = min(chunk, width - c)
            y = _dot(h, w_ref[:, off + c:off + c + cw])
            if use_bias:
                y = y + b_ref[:, off + c:off + c + cw]
            if act == "gelu":
                y = 0.5 * y * (1.0 + lax.erf(y * (0.5 ** 0.5)))
            o_ref[:, c:c + cw] = y.astype(o_ref.dtype)
        off += width


def norm_matmul(x, g, w, outs, bias=None, act=None, tm=512, chunk=512):
    n, d = x.shape
    m = w.shape[1]
    widths = tuple(o[0] for o in outs)
    assert sum(widths) == m
    tm = _row_tile(n, tm)
    in_specs = [pl.BlockSpec((tm, d), lambda i: (i, 0)),
                pl.BlockSpec((1, d), lambda i: (0, 0)),
                pl.BlockSpec((d, m), lambda i: (0, 0))]
    args = [x, g.reshape(1, d).astype(F32), w]
    if bias is not None:
        in_specs.append(pl.BlockSpec((1, m), lambda i: (0, 0)))
        args.append(bias.reshape(1, m).astype(F32))
    body = functools.partial(_norm_mm_body, widths=widths, chunk=chunk,
                             use_bias=bias is not None, act=act)
    return pl.pallas_call(
        body,
        name="norm_matmul",
        out_shape=[jax.ShapeDtypeStruct((n, wd), dt) for wd, dt in outs],
        grid=(n // tm,),
        in_specs=in_specs,
        out_specs=[pl.BlockSpec((tm, wd), lambda i: (i, 0)) for wd, _ in outs],
        compiler_params=_params("parallel"),
    )(*args)


def _mm_res_body(a_ref, w_ref, x_ref, o_ref):
    o_ref[...] = x_ref[...] + _dot(a_ref[...], w_ref[...])


def matmul_residual(a, w, x, tm=512):
    n, k = a.shape
    d = w.shape[1]
    tm = _row_tile(n, tm)
    return pl.pallas_call(
        _mm_res_body,
        name="matmul_residual",
        out_shape=jax.ShapeDtypeStruct((n, d), F32),
        grid=(n // tm,),
        in_specs=[pl.BlockSpec((tm, k), lambda i: (i, 0)),
                  pl.BlockSpec((k, d), lambda i: (0, 0)),
                  pl.BlockSpec((tm, d), lambda i: (i, 0))],
        out_specs=pl.BlockSpec((tm, d), lambda i: (i, 0)),
        compiler_params=_params("parallel"),
    )(a, w, x)


def _swiglu_up_body(x_ref, g_ref, wg_ref, wu_ref, o_ref, *, chunk):
    h = _rms(x_ref[...], g_ref[...]).astype(BF16)
    f = o_ref.shape[1]
    for c in range(0, f, chunk):
        a = _dot(h, wg_ref[:, c:c + chunk])
        b = _dot(h, wu_ref[:, c:c + chunk])
        o_ref[:, c:c + chunk] = (_silu(a) * b).astype(o_ref.dtype)


def swiglu_up(x, g, wg, wu, tm=512, chunk=256):
    n, d = x.shape
    f = wg.shape[1]
    assert f % chunk == 0
    tm = _row_tile(n, tm)
    return pl.pallas_call(
        functools.partial(_swiglu_up_body, chunk=chunk),
        name="swiglu_up",
        out_shape=jax.ShapeDtypeStruct((n, f), BF16),
        grid=(n // tm,),
        in_specs=[pl.BlockSpec((tm, d), lambda i: (i, 0)),
                  pl.BlockSpec((1, d), lambda i: (0, 0)),
                  pl.BlockSpec((d, f), lambda i: (0, 0)),
                  pl.BlockSpec((d, f), lambda i: (0, 0))],
        out_specs=pl.BlockSpec((tm, f), lambda i: (i, 0)),
        compiler_params=_params("parallel"),
    )(x, g.reshape(1, d).astype(F32), wg, wu)


def _hgrn_body(q_ref, f_ref, i_ref, g_ref, lb_ref, on_ref, x_ref, wo_ref, o_ref, s_ref, m_ref, *, tb):
    c32 = HG_CHUNK
    hd = HG_HEAD_DIM

    @pl.when(pl.program_id(1) == 0)
    def _():
        s_ref[...] = jnp.zeros_like(s_ref)

    lb = lb_ref[...]
    on = on_ref[...]
    row = lax.broadcasted_iota(jnp.int32, (c32, c32), 0)
    col = lax.broadcasted_iota(jnp.int32, (c32, c32), 1)
    causal = row >= col
    tri_bf = causal.astype(BF16)

    def chunk(c, carry):
        r0 = pl.multiple_of(c * c32, c32)
        rows = pl.ds(r0, c32)
        forget = lb + (1.0 - lb) * jax.nn.sigmoid(f_ref[rows, :].astype(F32))
        b = _dot_split(tri_bf, jnp.log(forget))
        b_mid = b[c32 // 2 - 1:c32 // 2, :]
        b_last = b[c32 - 1:c32, :]
        qs = _silu(q_ref[rows, :].astype(F32))
        k = 1.0 - forget
        v_bf = i_ref[rows, :]
        qa = (qs * jnp.exp(b - b_mid)).astype(BF16)
        ka = (k * jnp.exp(b_mid - b)).astype(BF16)
        q_in = (qs * jnp.exp(b)).astype(BF16)
        k_end = (k * jnp.exp(b_last - b)).astype(BF16)
        d_end = jnp.exp(b_last)
        d_cols = jnp.concatenate([d_end[:, h * hd:(h + 1) * hd] for h in range(HG_HEADS)], axis=0).T
        gate = _silu(g_ref[rows, :].astype(F32))
        heads = range(HG_HEADS)
        sls = [slice(h * hd, (h + 1) * hd) for h in heads]
        s_old = [s_ref[h] for h in heads]
        sc = [lax.dot_general(qa[:, sl], ka[:, sl], NT_DIMS, preferred_element_type=F32) for sl in sls]
        inter = [_dot(q_in[:, sl], s.astype(BF16)) for sl, s in zip(sls, s_old)]
        upd = [lax.dot_general(k_end[:, sl], v_bf[:, sl], TN_DIMS, preferred_element_type=F32) for sl in sls]
        sc_bf = [jnp.where(causal, a, 0.0).astype(BF16) for a in sc]
        o = [_dot(a, v_bf[:, sl]) + b for a, sl, b in zip(sc_bf, sls, inter)]
        for h in heads:
            s_ref[h] = s_old[h] * d_cols[:, h:h + 1] + upd[h]
        outs = [_rms(a, on[:, sl]) for a, sl in zip(o, sls)]
        m_ref[rows, :] = (jnp.concatenate(outs, axis=-1) * gate).astype(m_ref.dtype)
        return carry

    lax.fori_loop(0, tb // c32, chunk, 0, unroll=16)
    o_ref[...] = x_ref[...] + _dot(m_ref[...], wo_ref[...])


def hgrn_recurrence(qfig, lb, o_norm, x, w_out, batch, seq, tb=512):
    n = batch * seq
    d = D_MODEL
    tb = _row_tile(seq, tb)
    nt = seq // tb
    col_spec = lambda j: pl.BlockSpec((tb, d), lambda b, t, j=j: (b * nt + t, j))
    return pl.pallas_call(
        functools.partial(_hgrn_body, tb=tb),
        name="hgrn_recurrence",
        out_shape=jax.ShapeDtypeStruct((n, d), F32),
        grid=(batch, nt),
        in_specs=[col_spec(0), col_spec(1), col_spec(2), col_spec(3),
                  pl.BlockSpec((1, d), lambda b, t: (0, 0)),
                  pl.BlockSpec((1, d), lambda b, t: (0, 0)),
                  col_spec(0),
                  pl.BlockSpec((d, d), lambda b, t: (0, 0))],
        out_specs=pl.BlockSpec((tb, d), lambda b, t: (b * nt + t, 0)),
        scratch_shapes=[pltpu.VMEM((HG_HEADS, HG_HEAD_DIM, HG_HEAD_DIM), F32), pltpu.VMEM((tb, d), BF16)],
        compiler_params=_params("parallel", "arbitrary"),
    )(qfig, qfig, qfig, qfig, lb.reshape(1, d).astype(F32), o_norm.reshape(1, d).astype(F32), x, w_out)


def _swa_body(sink_ref, q_ref, kv_ref, pos_ref, qn_ref, kn_ref, invf_ref, sgn_ref, x_ref, wo_ref, o_ref,
              kprev_ref, vprev_ref, m_ref, *, nblk):
    w = SW_WINDOW
    hd = SW_HEAD_DIM
    j = pl.program_id(1)

    @pl.when(j == 0)
    def _():
        kprev_ref[...] = jnp.zeros_like(kprev_ref)
        vprev_ref[...] = jnp.zeros_like(vprev_ref)

    lane = lax.broadcasted_iota(jnp.int32, (nblk * w, LANES), 1)
    first_half = (lane % hd) < (hd // 2)
    low_head = lane < hd
    ang = pos_ref[...] * invf_ref[...]
    cos = jnp.cos(ang)
    sin_signed = jnp.sin(ang) * sgn_ref[...]
    gi = lax.broadcasted_iota(jnp.int32, (LANES, LANES), 0) // hd
    gj = lax.broadcasted_iota(jnp.int32, (LANES, LANES), 1) // hd
    head_mean = jnp.where(gi == gj, 1.0 / hd, 0.0).astype(BF16)

    def norm_rope(x, gain):
        xx = x * x
        hi = xx.astype(BF16)
        lo = (xx - hi.astype(F32)).astype(BF16)
        ms = _dot(hi, head_mean) + _dot(lo, head_mean)
        x = x * lax.rsqrt(ms + EPS) * gain
        partner = jnp.where(first_half, pltpu.roll(x, LANES - hd // 2, 1), pltpu.roll(x, hd // 2, 1))
        return x * cos + partner * sin_signed

    qn = qn_ref[...]
    kn = kn_ref[...]
    kv = kv_ref[...].astype(F32)
    n_kv_tiles = SW_KV_HEADS * hd // LANES
    blocks = range(nblk)
    blk = [slice(b * w, (b + 1) * w) for b in blocks]
    k_cur = [norm_rope(kv[:, t * LANES:(t + 1) * LANES], kn) for t in range(n_kv_tiles)]
    v_cur = [kv[:, (n_kv_tiles + t) * LANES:(n_kv_tiles + t + 1) * LANES] for t in range(n_kv_tiles)]
    k_all = [[jnp.concatenate([kprev_ref[t] if b == 0 else k_cur[t][blk[b - 1]], k_cur[t][blk[b]]], axis=0)
              for t in range(n_kv_tiles)] for b in blocks]
    v_all = [[jnp.concatenate([vprev_ref[t] if b == 0 else v_cur[t][blk[b - 1]], v_cur[t][blk[b]]], axis=0)
              for t in range(n_kv_tiles)] for b in blocks]
    for t in range(n_kv_tiles):
        kprev_ref[t] = k_cur[t][blk[nblk - 1]]
        vprev_ref[t] = v_cur[t][blk[nblk - 1]]

    lane2 = lax.broadcasted_iota(jnp.int32, (2 * w, LANES), 1)
    low2 = lane2 < hd
    c_idx = lax.broadcasted_iota(jnp.int32, (w, 2 * w), 0)
    r_idx = lax.broadcasted_iota(jnp.int32, (w, 2 * w), 1)
    rel = c_idx + w - r_idx
    band = (rel >= 0) & (rel < w)
    valid = [band & ((j * nblk + b - 1) * w + r_idx >= 0) for b in blocks]
    scale = hd ** -0.5

    k_half, v_half = [], []
    for b in blocks:
        kb, vb = [], []
        for kvh in range(SW_KV_HEADS):
            kt, vt = k_all[b][kvh // 2], v_all[b][kvh // 2]
            kt_sw = pltpu.roll(kt, hd, 1)
            vt_sw = pltpu.roll(vt, hd, 1)
            lo_src, hi_src = ((kt, vt), (kt_sw, vt_sw)) if kvh % 2 == 0 else ((kt_sw, vt_sw), (kt, vt))
            kb.append((jnp.where(low2, lo_src[0], 0.0).astype(BF16), jnp.where(low2, 0.0, hi_src[0]).astype(BF16)))
            vb.append((jnp.where(low2, lo_src[1], 0.0).astype(BF16), jnp.where(low2, 0.0, hi_src[1]).astype(BF16)))
        k_half.append(kb)
        v_half.append(vb)
    n_pairs = SW_HEADS // 2
    qp = [(norm_rope(q_ref[:, t * LANES:(t + 1) * LANES].astype(F32), qn) * scale).astype(BF16)
          for t in range(n_pairs)]
    jobs = [(b, h) for b in blocks for h in range(SW_HEADS)]
    s = [jnp.where(valid[b], lax.dot_general(qp[h // 2][blk[b]], k_half[b][h // SW_GROUP][h % 2], NT_DIMS,
                                             preferred_element_type=F32), -jnp.inf) for b, h in jobs]
    m = [jnp.maximum(jnp.max(s[n], axis=-1, keepdims=True), sink_ref[h]) for n, (b, h) in enumerate(jobs)]
    p = [jnp.exp(s[n] - m[n]) for n in range(len(jobs))]
    denom = [jnp.sum(p[n], axis=-1, keepdims=True) + jnp.exp(sink_ref[h] - m[n]) for n, (b, h) in enumerate(jobs)]
    pv = [_dot(p[n].astype(BF16), v_half[b][h // SW_GROUP][h % 2]) / denom[n] for n, (b, h) in enumerate(jobs)]
    for b in blocks:
        for t in range(n_pairs):
            pair = pv[b * SW_HEADS + 2 * t] + pv[b * SW_HEADS + 2 * t + 1]
            m_ref[blk[b], t * LANES:(t + 1) * LANES] = pair.astype(m_ref.dtype)
    o_ref[...] = x_ref[...] + _dot(m_ref[...], wo_ref[...])


def swa_attention(q, kv, pos, q_norm, k_norm, sinks, x, w_out, batch, seq, nblk=4):
    n = batch * seq
    w = nblk * SW_WINDOW
    assert seq % w == 0
    nb = seq // w
    hd = SW_HEAD_DIM
    lane = jnp.arange(LANES)
    inv_freq = ROPE_THETA ** (-jnp.arange(0, hd, 2, dtype=F32) / hd)
    invf = inv_freq[lane % (hd // 2)].reshape(1, LANES)
    sgn = jnp.where((lane % hd) < hd // 2, -1.0, 1.0).astype(F32).reshape(1, LANES)
    tile2 = lambda g: jnp.tile(g.astype(F32), LANES // hd).reshape(1, LANES)
    n_kv_tiles = SW_KV_HEADS * hd // LANES
    row = lambda b, jj, s: (b * nb + jj, 0)
    const = lambda b, jj, s: (0, 0)
    grid_spec = pltpu.PrefetchScalarGridSpec(
        num_scalar_prefetch=1,
        grid=(batch, nb),
        in_specs=[pl.BlockSpec((w, D_MODEL), row),
                  pl.BlockSpec((w, 2 * SW_KV_HEADS * hd), row),
                  pl.BlockSpec((w, 1), row),
                  pl.BlockSpec((1, LANES), const),
                  pl.BlockSpec((1, LANES), const),
                  pl.BlockSpec((1, LANES), const),
                  pl.BlockSpec((1, LANES), const),
                  pl.BlockSpec((w, D_MODEL), row),
                  pl.BlockSpec((D_MODEL, D_MODEL), const)],
        out_specs=pl.BlockSpec((w, D_MODEL), row),
        scratch_shapes=[pltpu.VMEM((n_kv_tiles, SW_WINDOW, LANES), F32),
                        pltpu.VMEM((n_kv_tiles, SW_WINDOW, LANES), F32),
                        pltpu.VMEM((w, D_MODEL), BF16)],
    )
    return pl.pallas_call(
        functools.partial(_swa_body, nblk=nblk),
        name="swa_attention",
        out_shape=jax.ShapeDtypeStruct((n, D_MODEL), F32),
        grid_spec=grid_spec,
        compiler_params=_params("parallel", "arbitrary"),
    )(sinks.astype(F32), q, kv, pos, tile2(q_norm), tile2(k_norm), invf, sgn, x, w_out)


def _gmlp_body(u_ref, v_ref, lg_ref, lbias_ref, ws_ref, bs_ref, x_ref, wo_ref, o_ref, m_ref, *, tb):
    c = GM_CHUNK
    gw = GM_WIDTH // GM_GROUPS
    row = lax.broadcasted_iota(jnp.int32, (c, c), 0)
    col = lax.broadcasted_iota(jnp.int32, (c, c), 1)
    causal = row >= col
    wcs = [jnp.where(causal, ws_ref[g], 0.0).astype(BF16) for g in range(GM_GROUPS)]
    bs = bs_ref[...]
    for r in range(0, tb, c):
        v = v_ref[r:r + c, :].astype(F32)
        mu = jnp.mean(v, axis=-1, keepdims=True)
        vc = v - mu
        var = jnp.mean(vc * vc, axis=-1, keepdims=True)
        vn = (vc * lax.rsqrt(var + EPS) * lg_ref[...] + lbias_ref[...]).astype(BF16)
        for g in range(GM_GROUPS):
            sl = slice(g * gw, (g + 1) * gw)
            mixed = _dot(wcs[g], vn[:, sl]) + bs[:, g:g + 1]
            m_ref[r:r + c, sl] = (u_ref[r:r + c, sl].astype(F32) * mixed).astype(m_ref.dtype)
    o_ref[...] = x_ref[...] + _dot(m_ref[...], wo_ref[...])


def gmlp_spatial(z, ln_g, ln_b, w_s, b_s, x, w_out, tb=512):
    n = z.shape[0]
    d = GM_WIDTH
    tb = _row_tile(n, tb)
    const2 = lambda i: (0, 0)
    return pl.pallas_call(
        functools.partial(_gmlp_body, tb=tb),
        name="gmlp_spatial",
        out_shape=jax.ShapeDtypeStruct((n, d), F32),
        grid=(n // tb,),
        in_specs=[pl.BlockSpec((tb, d), lambda i: (i, 0)),
                  pl.BlockSpec((tb, d), lambda i: (i, 1)),
                  pl.BlockSpec((1, d), const2),
                  pl.BlockSpec((1, d), const2),
                  pl.BlockSpec((GM_GROUPS, GM_CHUNK, GM_CHUNK), lambda i: (0, 0, 0)),
                  pl.BlockSpec((GM_CHUNK, GM_GROUPS), const2),
                  pl.BlockSpec((tb, d), lambda i: (i, 0)),
                  pl.BlockSpec((d, d), const2)],
        out_specs=pl.BlockSpec((tb, d), lambda i: (i, 0)),
        scratch_shapes=[pltpu.VMEM((tb, d), BF16)],
        compiler_params=_params("parallel"),
    )(z, z, ln_g.reshape(1, d).astype(F32), ln_b.reshape(1, d).astype(F32),
      w_s.astype(F32), b_s.T.astype(F32), x, w_out)


def _gdn_body(y_ref, z_ref, ba_ref, alog_ref, dtb_ref, on_ref, x_ref, wo_ref, o_ref, s_ref, m_ref, *, tb):
    c = GD_CHUNK
    hd = GD_HEAD_DIM
    nq = GD_QK_HEADS * hd

    @pl.when(pl.program_id(1) == 0)
    def _():
        s_ref[...] = jnp.zeros_like(s_ref)

    row = lax.broadcasted_iota(jnp.int32, (c, c), 0)
    col = lax.broadcasted_iota(jnp.int32, (c, c), 1)
    tri = row >= col
    tri_strict = row > col
    tri_bf = tri.astype(BF16)
    eye = (row == col).astype(F32)
    lane = lax.broadcasted_iota(jnp.int32, (1, LANES), 1)
    neg_a = -jnp.exp(alog_ref[...])
    on = on_ref[...]

    heads = range(GD_V_HEADS)
    rep = GD_V_HEADS // GD_QK_HEADS
    col = lambda a, hv: a[:, hv:hv + 1]
    gcol = lambda a, hv: a[:, GD_V_HEADS + hv:GD_V_HEADS + hv + 1]
    n_par = GD_PAR_CHUNKS

    def group(gi, carry):
        rows = [pl.ds(pl.multiple_of((gi * n_par + j) * c, c), c) for j in range(n_par)]
        par = range(n_par)
        jobs = [(j, hv) for j in par for hv in heads]
        ba = [ba_ref[r, :] for r in rows]
        beta_all = [jax.nn.sigmoid(a) for a in ba]
        in_rate_lanes = (lane >= GD_V_HEADS) & (lane < 2 * GD_V_HEADS)
        g_all = [jnp.where(in_rate_lanes, neg_a * jax.nn.softplus(a + dtb_ref[...]), 0.0) for a in ba]
        gam_all = [_dot_split(tri_bf, a) for a in g_all]
        gam_rows = [a.T for a in gam_all]
        e_gam_all = [jnp.exp(a) for a in gam_all]
        gam_last_all = [a[c - 1:c, :] for a in gam_all]
        e_end_all = [jnp.exp(b - a) for a, b in zip(gam_all, gam_last_all)]
        d_end_all = [jnp.exp(a) for a in gam_last_all]
        q_bf = [[y_ref[r, hk * hd:(hk + 1) * hd] for hk in range(GD_QK_HEADS)] for r in rows]
        k_bf = [[y_ref[r, nq + hk * hd:nq + (hk + 1) * hd] for hk in range(GD_QK_HEADS)] for r in rows]
        q = [[a.astype(F32) for a in qj] for qj in q_bf]
        k = [[a.astype(F32) for a in kj] for kj in k_bf]
        kk = [[lax.dot_general(a, a, NT_DIMS, preferred_element_type=F32) for a in kj] for kj in k_bf]
        qk = [[lax.dot_general(a, b, NT_DIMS, preferred_element_type=F32)
               for a, b in zip(q_bf[j], k_bf[j])] for j in par]
        decay = [jnp.exp(jnp.where(
            tri, gcol(gam_all[j], hv) - gam_rows[j][GD_V_HEADS + hv:GD_V_HEADS + hv + 1, :], -jnp.inf))
            for j, hv in jobs]
        p = [-jnp.where(tri_strict, kk[j][hv // rep] * decay[n] * col(beta_all[j], hv), 0.0)
             for n, (j, hv) in enumerate(jobs)]
        rhs = [jnp.concatenate(
            [y_ref[rows[j], 2 * nq + hv * hd:2 * nq + (hv + 1) * hd].astype(F32) * col(beta_all[j], hv),
             k[j][hv // rep] * (col(beta_all[j], hv) * gcol(e_gam_all[j], hv))], axis=-1).astype(BF16)
            for j, hv in jobs]
        u = [eye + a for a in p]
        p_bf = [a.astype(BF16) for a in p]
        n_steps = c.bit_length() - 1
        for step in range(1, n_steps):
            p_bf = [_dot(pb, pb).astype(BF16) for pb in p_bf]
            u = [a + _dot(a.astype(BF16), pb) for a, pb in zip(u, p_bf)]
        sol = [_dot(a.astype(BF16), b) for a, b in zip(u, rhs)]
        q_dec = [(q[j][hv // rep] * gcol(e_gam_all[j], hv)).astype(BF16) for j, hv in jobs]
        qk_dec = [(qk[j][hv // rep] * decay[n]).astype(BF16) for n, (j, hv) in enumerate(jobs)]
        k_end = [(k[j][hv // rep] * gcol(e_end_all[j], hv)).astype(BF16) for j, hv in jobs]
        gate = [_silu(z_ref[rows[j], hv * hd:(hv + 1) * hd].astype(F32)) for j, hv in jobs]
        for j in par:
            at = lambda hv: j * GD_V_HEADS + hv
            s_old = [s_ref[hv] for hv in heads]
            s_bf = [a.astype(BF16) for a in s_old]
            nv_bf = [(sol[at(hv)][:, :hd] - _dot(sol[at(hv)][:, hd:].astype(BF16), s_bf[hv])).astype(BF16)
                     for hv in heads]
            o = [_dot(q_dec[at(hv)], s_bf[hv]) + _dot(qk_dec[at(hv)], nv_bf[hv]) for hv in heads]
            s_new = [s_old[hv] * gcol(d_end_all[j], hv) + lax.dot_general(
                k_end[at(hv)], nv_bf[hv], TN_DIMS, preferred_element_type=F32) for hv in heads]
            for hv in heads:
                s_ref[hv] = s_new[hv]
            outs = [_rms(o[hv], on) * gate[at(hv)] for hv in heads]
            m_ref[rows[j], :] = jnp.concatenate(outs, axis=-1).astype(m_ref.dtype)
        return carry

    lax.fori_loop(0, tb // (c * n_par), group, 0)
    o_ref[...] = x_ref[...] + _dot(m_ref[...], wo_ref[...])


def gdn_recurrence(y, z, ba, a_log, dt_bias, o_norm, x, w_out, batch, seq, tb=256):
    n = batch * seq
    tb = _row_tile(seq, tb)
    nt = seq // tb
    pad_heads = lambda a: jnp.zeros((1, LANES), F32).at[0, GD_V_HEADS:2 * GD_V_HEADS].set(a.astype(F32))
    row = lambda b, t: (b * nt + t, 0)
    const = lambda b, t: (0, 0)
    return pl.pallas_call(
        functools.partial(_gdn_body, tb=tb),
        name="gdn_recurrence",
        out_shape=jax.ShapeDtypeStruct((n, D_MODEL), F32),
        grid=(batch, nt),
        in_specs=[pl.BlockSpec((tb, GD_QKV), row),
                  pl.BlockSpec((tb, GD_Z), row),
                  pl.BlockSpec((tb, LANES), row),
                  pl.BlockSpec((1, LANES), const),
                  pl.BlockSpec((1, LANES), const),
                  pl.BlockSpec((1, GD_HEAD_DIM), const),
                  pl.BlockSpec((tb, D_MODEL), row),
                  pl.BlockSpec((GD_Z, D_MODEL), const)],
        out_specs=pl.BlockSpec((tb, D_MODEL), row),
        scratch_shapes=[pltpu.VMEM((GD_V_HEADS, GD_HEAD_DIM, GD_HEAD_DIM), F32), pltpu.VMEM((tb, GD_Z), BF16)],
        compiler_params=_params("parallel", "arbitrary"),
    )(y, z, ba, pad_heads(a_log), pad_heads(dt_bias), o_norm.reshape(1, GD_HEAD_DIM).astype(F32), x, w_out)


def _gdn_in_body(x_ref, g_ref, w_ref, cw_ref, y_ref, z_ref, ba_ref, tail_ref, *, tiles_per_seq, chunk):
    tm = x_ref.shape[0]
    hd = GD_HEAD_DIM
    nq = GD_QK_HEADS * hd
    halo = SUBLANES
    first = pl.program_id(0) % tiles_per_seq == 0
    h = _rms(x_ref[...], g_ref[...]).astype(BF16)
    pres = [_dot(h, w_ref[:, c:c + chunk]) for c in range(0, GD_QKV, chunk)]
    for c, pre in zip(range(0, GD_QKV, chunk), pres):
        cols = slice(c, c + chunk)
        tail = jnp.where(first, 0.0, tail_ref[:, cols])
        tail_ref[:, cols] = pre[tm - halo:tm, :]
        ext = jnp.concatenate([tail, pre], axis=0)
        acc = cw_ref[GD_CONV - 1:GD_CONV, cols] * pre
        for jj in range(GD_CONV - 1):
            back = GD_CONV - 1 - jj
            acc = acc + cw_ref[jj:jj + 1, cols] * ext[halo - back:halo - back + tm, :]
        y = _silu(acc)
        if c < 2 * nq:
            parts = []
            for hh in range(chunk // hd):
                a = y[:, hh * hd:(hh + 1) * hd]
                a = a * lax.rsqrt(jnp.sum(a * a, axis=-1, keepdims=True) + EPS)
                parts.append(a * (hd ** -0.5) if c < nq else a)
            y = jnp.concatenate(parts, axis=-1)
        y_ref[:, cols] = y.astype(y_ref.dtype)
    for c in range(0, GD_Z, chunk):
        z_ref[:, c:c + chunk] = _dot(h, w_ref[:, GD_QKV + c:GD_QKV + c + chunk]).astype(z_ref.dtype)
    ba_ref[...] = _dot(h, w_ref[:, GD_QKV + GD_Z:])


def gdn_in_proj(x, g, w_all, conv_w, seq, tm=256, chunk=256):
    n, d = x.shape
    m = w_all.shape[1]
    tm = _row_tile(seq, tm)
    assert nq_aligned(chunk) and m == GD_QKV + GD_Z + LANES
    const = lambda i: (0, 0)
    row = lambda i: (i, 0)
    return pl.pallas_call(
        functools.partial(_gdn_in_body, tiles_per_seq=seq // tm, chunk=chunk),
        name="gdn_in_proj",
        out_shape=[jax.ShapeDtypeStruct((n, GD_QKV), BF16), jax.ShapeDtypeStruct((n, GD_Z), BF16),
                   jax.ShapeDtypeStruct((n, LANES), F32)],
        grid=(n // tm,),
        in_specs=[pl.BlockSpec((tm, d), row), pl.BlockSpec((1, d), const), pl.BlockSpec((d, m), const),
                  pl.BlockSpec((GD_CONV, GD_QKV), const)],
        out_specs=[pl.BlockSpec((tm, GD_QKV), row), pl.BlockSpec((tm, GD_Z), row), pl.BlockSpec((tm, LANES), row)],
        scratch_shapes=[pltpu.VMEM((SUBLANES, GD_QKV), F32)],
        compiler_params=_params("arbitrary"),
    )(x, g.reshape(1, d).astype(F32), w_all, conv_w.astype(F32))


def nq_aligned(chunk):
    nq = GD_QK_HEADS * GD_HEAD_DIM
    return chunk % GD_HEAD_DIM == 0 and nq % chunk == 0 and GD_QKV % chunk == 0 and GD_Z % chunk == 0


def _router_body(x_ref, g_ref, wr_ref, o_ref):
    h = _rms(x_ref[...], g_ref[...])
    w = wr_ref[...]
    h_hi = h.astype(BF16)
    h_lo = (h - h_hi.astype(F32)).astype(BF16)
    w_hi = w.astype(BF16)
    w_lo = (w - w_hi.astype(F32)).astype(BF16)
    logits = _dot(h_hi, w_hi) + (_dot(h_lo, w_hi) + _dot(h_hi, w_lo))
    lane = lax.broadcasted_iota(jnp.int32, logits.shape, 1)
    logits = jnp.where(lane < N_EXPERTS, logits, -jnp.inf)
    m1 = jnp.max(logits, axis=-1, keepdims=True)
    i1 = jnp.min(jnp.where(logits == m1, lane, LANES), axis=-1, keepdims=True)
    rest = jnp.where(lane == i1, -jnp.inf, logits)
    m2 = jnp.max(rest, axis=-1, keepdims=True)
    i2 = jnp.min(jnp.where(rest == m2, lane, LANES), axis=-1, keepdims=True)
    e2 = jnp.exp(m2 - m1)
    w1 = 1.0 / (1.0 + e2)
    w2 = e2 / (1.0 + e2)
    out = jnp.where(lane == 0, i1.astype(F32),
                    jnp.where(lane == 1, i2.astype(F32),
                              jnp.where(lane == 2, w1, jnp.where(lane == 3, w2, 0.0))))
    o_ref[...] = out


def moe_router(x, g, router, tm=512):
    n, d = x.shape
    tm = _row_tile(n, tm)
    wr = jnp.zeros((d, LANES), F32).at[:, :N_EXPERTS].set(router.astype(F32))
    return pl.pallas_call(
        _router_body,
        name="moe_router",
        out_shape=jax.ShapeDtypeStruct((n, LANES), F32),
        grid=(n // tm,),
        in_specs=[pl.BlockSpec((tm, d), lambda i: (i, 0)),
                  pl.BlockSpec((1, d), lambda i: (0, 0)),
                  pl.BlockSpec((d, LANES), lambda i: (0, 0))],
        out_specs=pl.BlockSpec((tm, LANES), lambda i: (i, 0)),
        compiler_params=_params("parallel"),
    )(x, g.reshape(1, d).astype(F32), wr)


def _row_copy(src_ref, src_row, dst_ref, dst_row, sem):
    return pltpu.make_async_copy(src_ref.at[pl.ds(src_row, 1), :], dst_ref.at[pl.ds(dst_row, 1), :], sem)


def _dispatch_body(ends_ref, padded_ref, pos_ref, x_ref, xs_ref, zero_ref, sem, zero_sem, *, rt, tm):
    @pl.when(pl.program_id(0) == 0)
    def _():
        zero_ref[...] = jnp.zeros_like(zero_ref)

        def zero_tile(row0):
            row0 = row0 if isinstance(row0, int) else pl.multiple_of(row0, tm)
            cp = pltpu.make_async_copy(zero_ref, xs_ref.at[pl.ds(row0, tm), :], zero_sem)
            cp.start()
            cp.wait()

        for e in range(N_EXPERTS):
            pl.when(padded_ref[e] > 0)(functools.partial(zero_tile, ends_ref[e] - tm))
        n_rows = xs_ref.shape[0]
        for back in range(1, N_EXPERTS + 1):
            row0 = n_rows - back * tm
            pl.when(row0 >= ends_ref[N_EXPERTS - 1])(functools.partial(zero_tile, row0))

    def start(g, carry):
        r0 = pl.multiple_of(g * SUBLANES, SUBLANES)
        for u in range(SUBLANES):
            for k in range(TOP_K):
                _row_copy(x_ref, r0 + u, xs_ref, pos_ref[0, 0, TOP_K * (r0 + u) + k], sem).start()
        return carry

    lax.fori_loop(0, rt // SUBLANES, start, 0)
    all_rows = xs_ref.at[pl.ds(0, TOP_K * rt), :]
    pltpu.make_async_copy(all_rows, all_rows, sem).wait()


def moe_dispatch(x, pos, ends, padded, n_rows, tm, rt=256):
    n, d = x.shape
    rt = _row_tile(n, rt)
    grid_spec = pltpu.PrefetchScalarGridSpec(
        num_scalar_prefetch=2,
        grid=(n // rt,),
        in_specs=[pl.BlockSpec((1, 1, TOP_K * rt), lambda i, en, pa: (i, 0, 0), memory_space=pltpu.SMEM),
                  pl.BlockSpec((rt, d), lambda i, en, pa: (i, 0))],
        out_specs=pl.BlockSpec(memory_space=pl.ANY),
        scratch_shapes=[pltpu.VMEM((tm, d), F32), pltpu.SemaphoreType.DMA(()),
                        pltpu.SemaphoreType.DMA(())],
    )
    return pl.pallas_call(
        functools.partial(_dispatch_body, rt=rt, tm=tm),
        name="moe_dispatch",
        out_shape=jax.ShapeDtypeStruct((n_rows, d), F32),
        grid_spec=grid_spec,
        compiler_params=_params("arbitrary"),
    )(ends, padded, pos.reshape(n // rt, 1, TOP_K * rt), x)


def _expert_changed(te_ref, i):
    return (i == 0) | (te_ref[i] != te_ref[jnp.maximum(i - 1, 0)])


def _moe_up_body(te_ref, nt_ref, xs_ref, g_ref, wg_ref, wu_ref, o_ref, wg_bf, wu_bf, *, chunk):
    i = pl.program_id(1)
    f = o_ref.shape[1]

    @pl.when(_expert_changed(te_ref, i))
    def _():
        for c in range(0, f, chunk):
            wg_bf[:, c:c + chunk] = wg_ref[:, c:c + chunk].astype(BF16)
            wu_bf[:, c:c + chunk] = wu_ref[:, c:c + chunk].astype(BF16)

    @pl.when(i < nt_ref[0])
    def _():
        h = _rms(xs_ref[...], g_ref[...]).astype(BF16)
        for c in range(0, f, chunk):
            a = _dot(h, wg_bf[:, c:c + chunk])
            b = _dot(h, wu_bf[:, c:c + chunk])
            o_ref[:, c:c + chunk] = (_silu(a) * b).astype(o_ref.dtype)

    @pl.when(i >= nt_ref[0])
    def _():
        o_ref[...] = jnp.zeros_like(o_ref)


def moe_up(xs, g, wg, wu, layer, tile_expert, n_tiles_used, tm, fsplit=2, chunk=256):
    p, d = xs.shape
    f = wg.shape[3]
    fb = f // fsplit
    assert fb % chunk == 0 and p % tm == 0
    w_spec = pl.BlockSpec((None, None, d, fb), lambda j, i, te, nt: (layer, te[i], 0, j))
    grid_spec = pltpu.PrefetchScalarGridSpec(
        num_scalar_prefetch=2,
        grid=(fsplit, p // tm),
        in_specs=[pl.BlockSpec((tm, d), lambda j, i, te, nt: (jnp.minimum(i, nt[0] - 1), 0)),
                  pl.BlockSpec((1, d), lambda j, i, te, nt: (0, 0)),
                  w_spec, w_spec],
        out_specs=pl.BlockSpec((tm, fb), lambda j, i, te, nt: (i, j)),
        scratch_shapes=[pltpu.VMEM((d, fb), BF16), pltpu.VMEM((d, fb), BF16)],
    )
    return pl.pallas_call(
        functools.partial(_moe_up_body, chunk=chunk),
        name="moe_up",
        out_shape=jax.ShapeDtypeStruct((p, f), BF16),
        grid_spec=grid_spec,
        compiler_params=_params("arbitrary", "arbitrary"),
    )(tile_expert, n_tiles_used, xs, g.reshape(1, d).astype(F32), wg, wu)


def _moe_down_body(te_ref, nt_ref, a_ref, wd_ref, o_ref, wd_bf, *, chunk):
    i = pl.program_id(0)

    @pl.when(_expert_changed(te_ref, i))
    def _():
        for r in range(0, wd_bf.shape[0], chunk):
            wd_bf[r:r + chunk, :] = wd_ref[r:r + chunk, :].astype(BF16)

    @pl.when(i < nt_ref[0])
    def _():
        o_ref[...] = _dot(a_ref[...], wd_bf[...])

    @pl.when(i >= nt_ref[0])
    def _():
        o_ref[...] = jnp.zeros_like(o_ref)


def moe_down(act, wd, layer, tile_expert, n_tiles_used, tm, chunk=512):
    p, f = act.shape
    d = wd.shape[3]
    assert f % chunk == 0
    grid_spec = pltpu.PrefetchScalarGridSpec(
        num_scalar_prefetch=2,
        grid=(p // tm,),
        in_specs=[pl.BlockSpec((tm, f), lambda i, te, nt: (i, 0)),
                  pl.BlockSpec((None, None, f, d), lambda i, te, nt: (layer, te[i], 0, 0))],
        out_specs=pl.BlockSpec((tm, d), lambda i, te, nt: (i, 0)),
        scratch_shapes=[pltpu.VMEM((f, d), BF16)],
    )
    return pl.pallas_call(
        functools.partial(_moe_down_body, chunk=chunk),
        name="moe_down",
        out_shape=jax.ShapeDtypeStruct((p, d), F32),
        grid_spec=grid_spec,
        compiler_params=_params("arbitrary"),
    )(tile_expert, n_tiles_used, act, wd)


def _combine_body(pos_ref, pos_next_ref, x_ref, r_ref, ys_ref, o_ref, buf_ref, sem, *, rt):
    i = pl.program_id(0)
    slot = i % 2

    def gather(p_ref, s):
        def start(g, carry):
            r0 = pl.multiple_of(g * SUBLANES, SUBLANES)
            for u in range(SUBLANES):
                for k in range(TOP_K):
                    _row_copy(ys_ref, p_ref[0, 0, TOP_K * (r0 + u) + k], buf_ref.at[s, k], r0 + u,
                              sem.at[s]).start()
            return carry

        lax.fori_loop(0, rt // SUBLANES, start, 0)

    @pl.when(i == 0)
    def _():
        gather(pos_ref, 0)

    @pl.when(i + 1 < pl.num_programs(0))
    def _():
        gather(pos_next_ref, 1 - slot)

    for k in range(TOP_K):
        pltpu.make_async_copy(ys_ref.at[pl.ds(0, rt), :], buf_ref.at[slot, k], sem.at[slot]).wait()
    r = r_ref[...]
    o_ref[...] = x_ref[...] + r[:, 2:3] * buf_ref[slot, 0] + r[:, 3:4] * buf_ref[slot, 1]


def moe_combine(x, route, ys, pos, rt=256):
    n, d = x.shape
    rt = _row_tile(n, rt)
    steps = n // rt
    pos3 = pos.reshape(steps, 1, TOP_K * rt)
    grid_spec = pltpu.PrefetchScalarGridSpec(
        num_scalar_prefetch=0,
        grid=(steps,),
        in_specs=[pl.BlockSpec((1, 1, TOP_K * rt), lambda i: (i, 0, 0), memory_space=pltpu.SMEM),
                  pl.BlockSpec((1, 1, TOP_K * rt), lambda i: (jnp.minimum(i + 1, steps - 1), 0, 0),
                               memory_space=pltpu.SMEM),
                  pl.BlockSpec((rt, d), lambda i: (i, 0)),
                  pl.BlockSpec((rt, LANES), lambda i: (i, 0)),
                  pl.BlockSpec(memory_space=pl.ANY)],
        out_specs=pl.BlockSpec((rt, d), lambda i: (i, 0)),
        scratch_shapes=[pltpu.VMEM((2, TOP_K, rt, d), F32), pltpu.SemaphoreType.DMA((2,))],
    )
    return pl.pallas_call(
        functools.partial(_combine_body, rt=rt),
        name="moe_combine",
        out_shape=jax.ShapeDtypeStruct((n, d), F32),
        grid_spec=grid_spec,
        compiler_params=_params("arbitrary"),
    )(pos3, pos3, x, route, ys)


def moe_ffn(x, g, router, wg, wu, wd, layer, tm=512):
    n, d = x.shape
    route = moe_router(x, g, router)
    expert = route[:, :TOP_K].astype(jnp.int32).reshape(n * TOP_K)
    onehot = (expert[:, None] == jnp.arange(N_EXPERTS, dtype=jnp.int32)[None, :]).astype(jnp.int32)
    csum = jnp.cumsum(onehot, axis=0)
    rank = jnp.sum((csum - onehot) * onehot, axis=1)
    counts = csum[-1]
    padded = ((counts + tm - 1) // tm) * tm
    ends = jnp.cumsum(padded)
    starts = ends - padded
    pos = (jnp.sum(starts[None, :] * onehot, axis=1) + rank).astype(jnp.int32)
    n_rows = n * TOP_K + N_EXPERTS * tm
    n_tiles = n_rows // tm
    tile_start = jnp.arange(n_tiles, dtype=jnp.int32) * tm
    tile_expert = jnp.minimum(jnp.sum((tile_start[:, None] >= ends[None, :]).astype(jnp.int32), axis=1),
                              N_EXPERTS - 1).astype(jnp.int32)
    n_tiles_used = (ends[-1:] // tm).astype(jnp.int32)

    xs = moe_dispatch(x, pos, ends.astype(jnp.int32), padded.astype(jnp.int32), n_rows, tm)
    act = moe_up(xs, g, wg, wu, layer, tile_expert, n_tiles_used, tm)
    ys = moe_down(act, wd, layer, tile_expert, n_tiles_used, tm)
    return moe_combine(x, route, ys, pos)


def hgrn2_layer(x, g, w_in, o_norm, w_out, lower_bound, batch, seq):
    (qfig,) = norm_matmul(x, g, w_in.astype(BF16), [(4 * D_MODEL, BF16)])
    return hgrn_recurrence(qfig, lower_bound, o_norm, x, w_out.astype(BF16), batch, seq)


def swa_layer(x, g, positions, w_in, q_norm, k_norm, sinks, w_out, batch, seq):
    q, kv = norm_matmul(x, g, w_in.astype(BF16),
                        [(D_MODEL, BF16), (2 * SW_KV_HEADS * SW_HEAD_DIM, BF16)])
    pos = positions.astype(F32).reshape(batch * seq, 1)
    return swa_attention(q, kv, pos, q_norm, k_norm, sinks, x, w_out.astype(BF16), batch, seq)


def gmlp_layer(x, g, w_in, b_in, ln_g, ln_b, w_s, b_s, w_out):
    (z,) = norm_matmul(x, g, w_in.astype(BF16), [(2 * GM_WIDTH, BF16)], bias=b_in, act="gelu")
    return gmlp_spatial(z, ln_g, ln_b, w_s, b_s, x, w_out.astype(BF16))


def gdn_layer(x, g, w_in, conv_w, a_log, dt_bias, o_norm, w_out, batch, seq):
    d = x.shape[1]
    n_small = w_in.shape[1] - GD_QKV - GD_Z
    w_main = w_in[:, :GD_QKV + GD_Z].astype(BF16)
    w_small = jnp.zeros((d, LANES), F32).at[:, :n_small].set(w_in[:, GD_QKV + GD_Z:].astype(F32))
    w_all = jnp.concatenate([w_main, w_small.astype(BF16)], axis=1)
    y, z, ba = gdn_in_proj(x, g, w_all, conv_w, seq)
    return gdn_recurrence(y, z, ba, a_log, dt_bias, o_norm, x, w_out.astype(BF16), batch, seq)


def dense_ffn(x, g, w_gate, w_up, w_down):
    act = swiglu_up(x, g, w_gate.astype(BF16), w_up.astype(BF16))
    return matmul_residual(act, w_down.astype(BF16), x)


def kernel(x, positions, mix_norm, ffn_norm, hgrn_lb_logits, hgrn_w_in, hgrn_o_norm, hgrn_w_out,
           swa_w_in, swa_q_norm, swa_k_norm, swa_sinks, swa_w_out,
           gmlp_w_in, gmlp_b_in, gmlp_v_ln_g, gmlp_v_ln_b, gmlp_w_s, gmlp_b_s, gmlp_w_out,
           gdn_w_in, gdn_conv_w, gdn_a_log, gdn_dt_bias, gdn_o_norm, gdn_w_out,
           dense_w_gate, dense_w_up, dense_w_down,
           moe_router, moe_w_gate, moe_w_up, moe_w_down):
    batch, seq, d = x.shape
    depth = mix_norm.shape[0]
    lower_bounds = jnp.cumsum(jax.nn.softmax(hgrn_lb_logits.astype(F32), axis=0), axis=0)
    h = x.reshape(batch * seq, d)
    for i in range(depth):
        kind, j = i % 4, i // 4
        if kind == 0:
            h = hgrn2_layer(h, mix_norm[i], hgrn_w_in[j], hgrn_o_norm[j], hgrn_w_out[j],
                            lower_bounds[i], batch, seq)
        elif kind == 1:
            h = swa_layer(h, mix_norm[i], positions, swa_w_in[j], swa_q_norm[j], swa_k_norm[j],
                          swa_sinks[j], swa_w_out[j], batch, seq)
        elif kind == 2:
            h = gmlp_layer(h, mix_norm[i], gmlp_w_in[j], gmlp_b_in[j], gmlp_v_ln_g[j], gmlp_v_ln_b[j],
                           gmlp_w_s[j], gmlp_b_s[j], gmlp_w_out[j])
        else:
            h = gdn_layer(h, mix_norm[i], gdn_w_in[j], gdn_conv_w[j], gdn_a_log[j], gdn_dt_bias[j],
                          gdn_o_norm[j], gdn_w_out[j], batch, seq)
        if i % 2 == 0:
            h = dense_ffn(h, ffn_norm[i], dense_w_gate[i // 2], dense_w_up[i // 2], dense_w_down[i // 2])
        else:
            h = moe_ffn(h, ffn_norm[i], moe_router[i // 2], moe_w_gate, moe_w_up, moe_w_down, i // 2)
    return h.reshape(batch, seq, d)
```

```python
import functools

import jax
import jax.numpy as jnp
from jax import lax
from jax.experimental import pallas as pl
from jax.experimental.pallas import tpu as pltpu

F32 = jnp.float32
BF16 = jnp.bfloat16
EPS = 1e-6

D_MODEL = 1024
LANES = 128
SUBLANES = 8
HG_HEAD_DIM = 128
HG_HEADS = D_MODEL // HG_HEAD_DIM
HG_CHUNK = 32
SW_HEAD_DIM = 64
SW_HEADS = D_MODEL // SW_HEAD_DIM
SW_KV_HEADS = 4
SW_GROUP = SW_HEADS // SW_KV_HEADS
SW_WINDOW = 128
ROPE_THETA = 10000.0
GM_WIDTH = D_MODEL
GM_GROUPS = 8
GM_CHUNK = 128
GD_HEAD_DIM = 128
GD_QK_HEADS = D_MODEL // GD_HEAD_DIM
GD_V_HEADS = 2 * GD_QK_HEADS
GD_CONV = 4
GD_CHUNK = 64
GD_QKV = 2 * GD_QK_HEADS * GD_HEAD_DIM + GD_V_HEADS * GD_HEAD_DIM
GD_Z = GD_V_HEADS * GD_HEAD_DIM
GD_PAR_CHUNKS = 4
N_EXPERTS = 8
TOP_K = 2

V7X_VMEM_LIMIT_BYTES = 56 * 1024 * 1024

NT_DIMS = (((1,), (1,)), ((), ()))
TN_DIMS = (((0,), (0,)), ((), ()))


def _params(*sem):
    return pltpu.CompilerParams(dimension_semantics=sem, vmem_limit_bytes=V7X_VMEM_LIMIT_BYTES)


def _dot(a, b):
    return jnp.dot(a, b, preferred_element_type=F32)


def _dot_split(a_bf, x):
    hi = x.astype(BF16)
    lo = (x - hi.astype(F32)).astype(BF16)
    return _dot(a_bf, hi) + _dot(a_bf, lo)


def _rms(x, g):
    return x * lax.rsqrt(jnp.mean(x * x, axis=-1, keepdims=True) + EPS) * g


def _silu(x):
    return x * jax.nn.sigmoid(x)


def _row_tile(n, want):
    t = min(n, want)
    assert n % t == 0, (n, t)
    return t


def _norm_mm_body(x_ref, g_ref, w_ref, *rest, widths, chunk, use_bias, act):
    if use_bias:
        b_ref, out_refs = rest[0], rest[1:]
    else:
        b_ref, out_refs = None, rest
    h = _rms(x_ref[...], g_ref[...]).astype(BF16)
    off = 0
    for o_ref, width in zip(out_refs, widths):
        for c in range(0, width, chunk):
            cw = min(chunk, width - c)
            y = _dot(h, w_ref[:, off + c:off + c + cw])
            if use_bias:
                y = y + b_ref[:, off + c:off + c + cw]
            if act == "gelu":
                y = 0.5 * y * (1.0 + lax.erf(y * (0.5 ** 0.5)))
            o_ref[:, c:c + cw] = y.astype(o_ref.dtype)
        off += width


def norm_matmul(x, g, w, outs, bias=None, act=None, tm=512, chunk=512):
    n, d = x.shape
    m = w.shape[1]
    widths = tuple(o[0] for o in outs)
    assert sum(widths) == m
    tm = _row_tile(n, tm)
    in_specs = [pl.BlockSpec((tm, d), lambda i: (i, 0)),
                pl.BlockSpec((1, d), lambda i: (0, 0)),
                pl.BlockSpec((d, m), lambda i: (0, 0))]
    args = [x, g.reshape(1, d).astype(F32), w]
    if bias is not None:
        in_specs.append(pl.BlockSpec((1, m), lambda i: (0, 0)))
        args.append(bias.reshape(1, m).astype(F32))
    body = functools.partial(_norm_mm_body, widths=widths, chunk=chunk,
                             use_bias=bias is not None, act=act)
    return pl.pallas_call(
        body,
        name="norm_matmul",
        out_shape=[jax.ShapeDtypeStruct((n, wd), dt) for wd, dt in outs],
        grid=(n // tm,),
        in_specs=in_specs,
        out_specs=[pl.BlockSpec((tm, wd), lambda i: (i, 0)) for wd, _ in outs],
        compiler_params=_params("parallel"),
    )(*args)


def _mm_res_body(a_ref, w_ref, x_ref, o_ref):
    o_ref[...] = x_ref[...] + _dot(a_ref[...], w_ref[...])


def matmul_residual(a, w, x, tm=512):
    n, k = a.shape
    d = w.shape[1]
    tm = _row_tile(n, tm)
    return pl.pallas_call(
        _mm_res_body,
        name="matmul_residual",
        out_shape=jax.ShapeDtypeStruct((n, d), F32),
        grid=(n // tm,),
        in_specs=[pl.BlockSpec((tm, k), lambda i: (i, 0)),
                  pl.BlockSpec((k, d), lambda i: (0, 0)),
                  pl.BlockSpec((tm, d), lambda i: (i, 0))],
        out_specs=pl.BlockSpec((tm, d), lambda i: (i, 0)),
        compiler_params=_params("parallel"),
    )(a, w, x)


def _swiglu_up_body(x_ref, g_ref, wg_ref, wu_ref, o_ref, *, chunk):
    h = _rms(x_ref[...], g_ref[...]).astype(BF16)
    f = o_ref.shape[1]
    for c in range(0, f, chunk):
        a = _dot(h, wg_ref[:, c:c + chunk])
        b = _dot(h, wu_ref[:, c:c + chunk])
        o_ref[:, c:c + chunk] = (_silu(a) * b).astype(o_ref.dtype)


def swiglu_up(x, g, wg, wu, tm=1024, chunk=256):
    n, d = x.shape
    f = wg.shape[1]
    assert f % chunk == 0
    tm = _row_tile(n, tm)
    return pl.pallas_call(
        functools.partial(_swiglu_up_body, chunk=chunk),
        name="swiglu_up",
        out_shape=jax.ShapeDtypeStruct((n, f), BF16),
        grid=(n // tm,),
        in_specs=[pl.BlockSpec((tm, d), lambda i: (i, 0)),
                  pl.BlockSpec((1, d), lambda i: (0, 0)),
                  pl.BlockSpec((d, f), lambda i: (0, 0)),
                  pl.BlockSpec((d, f), lambda i: (0, 0))],
        out_specs=pl.BlockSpec((tm, f), lambda i: (i, 0)),
        compiler_params=_params("parallel"),
    )(x, g.reshape(1, d).astype(F32), wg, wu)


def _hgrn_body(q_ref, f_ref, i_ref, g_ref, lb_ref, on_ref, x_ref, wo_ref, o_ref, s_ref, m_ref, *, tb):
    c32 = HG_CHUNK
    hd = HG_HEAD_DIM

    @pl.when(pl.program_id(1) == 0)
    def _():
        s_ref[...] = jnp.zeros_like(s_ref)

    lb = lb_ref[...]
    on = on_ref[...]
    row = lax.broadcasted_iota(jnp.int32, (c32, c32), 0)
    col = lax.broadcasted_iota(jnp.int32, (c32, c32), 1)
    causal = row >= col
    tri_bf = causal.astype(BF16)

    def chunk(c, carry):
        r0 = pl.multiple_of(c * c32, c32)
        rows = pl.ds(r0, c32)
        forget = lb + (1.0 - lb) * jax.nn.sigmoid(f_ref[rows, :].astype(F32))
        b = _dot_split(tri_bf, jnp.log(forget))
        b_mid = b[c32 // 2 - 1:c32 // 2, :]
        b_last = b[c32 - 1:c32, :]
        qs = _silu(q_ref[rows, :].astype(F32))
        k = 1.0 - forget
        v_bf = i_ref[rows, :]
        qa = (qs * jnp.exp(b - b_mid)).astype(BF16)
        ka = (k * jnp.exp(b_mid - b)).astype(BF16)
        q_in = (qs * jnp.exp(b)).astype(BF16)
        k_end = (k * jnp.exp(b_last - b)).astype(BF16)
        d_end = jnp.exp(b_last)
        d_cols = jnp.concatenate([d_end[:, h * hd:(h + 1) * hd] for h in range(HG_HEADS)], axis=0).T
        gate = _silu(g_ref[rows, :].astype(F32))
        heads = range(HG_HEADS)
        sls = [slice(h * hd, (h + 1) * hd) for h in heads]
        s_old = [s_ref[h] for h in heads]
        sc = [lax.dot_general(qa[:, sl], ka[:, sl], NT_DIMS, preferred_element_type=F32) for sl in sls]
        inter = [_dot(q_in[:, sl], s.astype(BF16)) for sl, s in zip(sls, s_old)]
        upd = [lax.dot_general(k_end[:, sl], v_bf[:, sl], TN_DIMS, preferred_element_type=F32) for sl in sls]
        sc_bf = [jnp.where(causal, a, 0.0).astype(BF16) for a in sc]
        o = [_dot(a, v_bf[:, sl]) + b for a, sl, b in zip(sc_bf, sls, inter)]
        for h in heads:
            s_ref[h] = s_old[h] * d_cols[:, h:h + 1] + upd[h]
        outs = [_rms(a, on[:, sl]) for a, sl in zip(o, sls)]
        m_ref[rows, :] = (jnp.concatenate(outs, axis=-1) * gate).astype(m_ref.dtype)
        return carry

    lax.fori_loop(0, tb // c32, chunk, 0, unroll=16)
    o_ref[...] = x_ref[...] + _dot(m_ref[...], wo_ref[...])


def hgrn_recurrence(qfig, lb, o_norm, x, w_out, batch, seq, tb=512):
    n = batch * seq
    d = D_MODEL
    tb = _row_tile(seq, tb)
    nt = seq // tb
    col_spec = lambda j: pl.BlockSpec((tb, d), lambda b, t, j=j: (b * nt + t, j))
    return pl.pallas_call(
        functools.partial(_hgrn_body, tb=tb),
        name="hgrn_recurrence",
        out_shape=jax.ShapeDtypeStruct((n, d), F32),
        grid=(batch, nt),
        in_specs=[col_spec(0), col_spec(1), col_spec(2), col_spec(3),
                  pl.BlockSpec((1, d), lambda b, t: (0, 0)),
                  pl.BlockSpec((1, d), lambda b, t: (0, 0)),
                  col_spec(0),
                  pl.BlockSpec((d, d), lambda b, t: (0, 0))],
        out_specs=pl.BlockSpec((tb, d), lambda b, t: (b * nt + t, 0)),
        scratch_shapes=[pltpu.VMEM((HG_HEADS, HG_HEAD_DIM, HG_HEAD_DIM), F32), pltpu.VMEM((tb, d), BF16)],
        compiler_params=_params("parallel", "arbitrary"),
    )(qfig, qfig, qfig, qfig, lb.reshape(1, d).astype(F32), o_norm.reshape(1, d).astype(F32), x, w_out)


def _swa_body(sink_ref, q_ref, kv_ref, pos_ref, qn_ref, kn_ref, invf_ref, sgn_ref, x_ref, wo_ref, o_ref,
              kprev_ref, vprev_ref, m_ref, *, nblk):
    w = SW_WINDOW
    hd = SW_HEAD_DIM
    j = pl.program_id(1)

    @pl.when(j == 0)
    def _():
        kprev_ref[...] = jnp.zeros_like(kprev_ref)
        vprev_ref[...] = jnp.zeros_like(vprev_ref)

    lane = lax.broadcasted_iota(jnp.int32, (nblk * w, LANES), 1)
    first_half = (lane % hd) < (hd // 2)
    low_head = lane < hd
    ang = pos_ref[...] * invf_ref[...]
    cos = jnp.cos(ang)
    sin_signed = jnp.sin(ang) * sgn_ref[...]
    gi = lax.broadcasted_iota(jnp.int32, (LANES, LANES), 0) // hd
    gj = lax.broadcasted_iota(jnp.int32, (LANES, LANES), 1) // hd
    head_mean = jnp.where(gi == gj, 1.0 / hd, 0.0).astype(BF16)

    def norm_rope(x, gain):
        xx = x * x
        hi = xx.astype(BF16)
        lo = (xx - hi.astype(F32)).astype(BF16)
        ms = _dot(hi, head_mean) + _dot(lo, head_mean)
        x = x * lax.rsqrt(ms + EPS) * gain
        partner = jnp.where(first_half, pltpu.roll(x, LANES - hd // 2, 1), pltpu.roll(x, hd // 2, 1))
        return x * cos + partner * sin_signed

    qn = qn_ref[...]
    kn = kn_ref[...]
    kv = kv_ref[...].astype(F32)
    n_kv_tiles = SW_KV_HEADS * hd // LANES
    blocks = range(nblk)
    blk = [slice(b * w, (b + 1) * w) for b in blocks]
    k_cur = [norm_rope(kv[:, t * LANES:(t + 1) * LANES], kn) for t in range(n_kv_tiles)]
    v_cur = [kv[:, (n_kv_tiles + t) * LANES:(n_kv_tiles + t + 1) * LANES] for t in range(n_kv_tiles)]
    k_all = [[jnp.concatenate([kprev_ref[t] if b == 0 else k_cur[t][blk[b - 1]], k_cur[t][blk[b]]], axis=0)
              for t in range(n_kv_tiles)] for b in blocks]
    v_all = [[jnp.concatenate([vprev_ref[t] if b == 0 else v_cur[t][blk[b - 1]], v_cur[t][blk[b]]], axis=0)
              for t in range(n_kv_tiles)] for b in blocks]
    for t in range(n_kv_tiles):
        kprev_ref[t] = k_cur[t][blk[nblk - 1]]
        vprev_ref[t] = v_cur[t][blk[nblk - 1]]

    lane2 = lax.broadcasted_iota(jnp.int32, (2 * w, LANES), 1)
    low2 = lane2 < hd
    c_idx = lax.broadcasted_iota(jnp.int32, (w, 2 * w), 0)
    r_idx = lax.broadcasted_iota(jnp.int32, (w, 2 * w), 1)
    rel = c_idx + w - r_idx
    band = (rel >= 0) & (rel < w)
    valid = [band & ((j * nblk + b - 1) * w + r_idx >= 0) for b in blocks]
    scale = hd ** -0.5

    k_half, v_half = [], []
    for b in blocks:
        kb, vb = [], []
        for kvh in range(SW_KV_HEADS):
            kt, vt = k_all[b][kvh // 2], v_all[b][kvh // 2]
            kt_sw = pltpu.roll(kt, hd, 1)
            vt_sw = pltpu.roll(vt, hd, 1)
            lo_src, hi_src = ((kt, vt), (kt_sw, vt_sw)) if kvh % 2 == 0 else ((kt_sw, vt_sw), (kt, vt))
            kb.append((jnp.where(low2, lo_src[0], 0.0).astype(BF16), jnp.where(low2, 0.0, hi_src[0]).astype(BF16)))
            vb.append((jnp.where(low2, lo_src[1], 0.0).astype(BF16), jnp.where(low2, 0.0, hi_src[1]).astype(BF16)))
        k_half.append(kb)
        v_half.append(vb)
    n_pairs = SW_HEADS // 2
    qp = [(norm_rope(q_ref[:, t * LANES:(t + 1) * LANES].astype(F32), qn) * scale).astype(BF16)
          for t in range(n_pairs)]
    jobs = [(b, h) for b in blocks for h in range(SW_HEADS)]
    s = [jnp.where(valid[b], lax.dot_general(qp[h // 2][blk[b]], k_half[b][h // SW_GROUP][h % 2], NT_DIMS,
                                             preferred_element_type=F32), -jnp.inf) for b, h in jobs]
    m = [jnp.maximum(jnp.max(s[n], axis=-1, keepdims=True), sink_ref[h]) for n, (b, h) in enumerate(jobs)]
    p = [jnp.exp(s[n] - m[n]) for n in range(len(jobs))]
    denom = [jnp.sum(p[n], axis=-1, keepdims=True) + jnp.exp(sink_ref[h] - m[n]) for n, (b, h) in enumerate(jobs)]
    pv = [_dot(p[n].astype(BF16), v_half[b][h // SW_GROUP][h % 2]) / denom[n] for n, (b, h) in enumerate(jobs)]
    for b in blocks:
        for t in range(n_pairs):
            pair = pv[b * SW_HEADS + 2 * t] + pv[b * SW_HEADS + 2 * t + 1]
            m_ref[blk[b], t * LANES:(t + 1) * LANES] = pair.astype(m_ref.dtype)
    o_ref[...] = x_ref[...] + _dot(m_ref[...], wo_ref[...])


def swa_attention(q, kv, pos, q_norm, k_norm, sinks, x, w_out, batch, seq, nblk=4):
    n = batch * seq
    w = nblk * SW_WINDOW
    assert seq % w == 0
    nb = seq // w
    hd = SW_HEAD_DIM
    lane = jnp.arange(LANES)
    inv_freq = ROPE_THETA ** (-jnp.arange(0, hd, 2, dtype=F32) / hd)
    invf = inv_freq[lane % (hd // 2)].reshape(1, LANES)
    sgn = jnp.where((lane % hd) < hd // 2, -1.0, 1.0).astype(F32).reshape(1, LANES)
    tile2 = lambda g: jnp.tile(g.astype(F32), LANES // hd).reshape(1, LANES)
    n_kv_tiles = SW_KV_HEADS * hd // LANES
    row = lambda b, jj, s: (b * nb + jj, 0)
    const = lambda b, jj, s: (0, 0)
    grid_spec = pltpu.PrefetchScalarGridSpec(
        num_scalar_prefetch=1,
        grid=(batch, nb),
        in_specs=[pl.BlockSpec((w, D_MODEL), row),
                  pl.BlockSpec((w, 2 * SW_KV_HEADS * hd), row),
                  pl.BlockSpec((w, 1), row),
                  pl.BlockSpec((1, LANES), const),
                  pl.BlockSpec((1, LANES), const),
                  pl.BlockSpec((1, LANES), const),
                  pl.BlockSpec((1, LANES), const),
                  pl.BlockSpec((w, D_MODEL), row),
                  pl.BlockSpec((D_MODEL, D_MODEL), const)],
        out_specs=pl.BlockSpec((w, D_MODEL), row),
        scratch_shapes=[pltpu.VMEM((n_kv_tiles, SW_WINDOW, LANES), F32),
                        pltpu.VMEM((n_kv_tiles, SW_WINDOW, LANES), F32),
                        pltpu.VMEM((w, D_MODEL), BF16)],
    )
    return pl.pallas_call(
        functools.partial(_swa_body, nblk=nblk),
        name="swa_attention",
        out_shape=jax.ShapeDtypeStruct((n, D_MODEL), F32),
        grid_spec=grid_spec,
        compiler_params=_params("parallel", "arbitrary"),
    )(sinks.astype(F32), q, kv, pos, tile2(q_norm), tile2(k_norm), invf, sgn, x, w_out)


def _gmlp_body(u_ref, v_ref, lg_ref, lbias_ref, ws_ref, bs_ref, x_ref, wo_ref, o_ref, m_ref, *, tb):
    c = GM_CHUNK
    gw = GM_WIDTH // GM_GROUPS
    row = lax.broadcasted_iota(jnp.int32, (c, c), 0)
    col = lax.broadcasted_iota(jnp.int32, (c, c), 1)
    causal = row >= col
    wcs = [jnp.where(causal, ws_ref[g], 0.0).astype(BF16) for g in range(GM_GROUPS)]
    bs = bs_ref[...]
    for r in range(0, tb, c):
        v = v_ref[r:r + c, :].astype(F32)
        mu = jnp.mean(v, axis=-1, keepdims=True)
        vc = v - mu
        var = jnp.mean(vc * vc, axis=-1, keepdims=True)
        vn = (vc * lax.rsqrt(var + EPS) * lg_ref[...] + lbias_ref[...]).astype(BF16)
        for g in range(GM_GROUPS):
            sl = slice(g * gw, (g + 1) * gw)
            mixed = _dot(wcs[g], vn[:, sl]) + bs[:, g:g + 1]
            m_ref[r:r + c, sl] = (u_ref[r:r + c, sl].astype(F32) * mixed).astype(m_ref.dtype)
    o_ref[...] = x_ref[...] + _dot(m_ref[...], wo_ref[...])


def gmlp_spatial(z, ln_g, ln_b, w_s, b_s, x, w_out, tb=512):
    n = z.shape[0]
    d = GM_WIDTH
    tb = _row_tile(n, tb)
    const2 = lambda i: (0, 0)
    return pl.pallas_call(
        functools.partial(_gmlp_body, tb=tb),
        name="gmlp_spatial",
        out_shape=jax.ShapeDtypeStruct((n, d), F32),
        grid=(n // tb,),
        in_specs=[pl.BlockSpec((tb, d), lambda i: (i, 0)),
                  pl.BlockSpec((tb, d), lambda i: (i, 1)),
                  pl.BlockSpec((1, d), const2),
                  pl.BlockSpec((1, d), const2),
                  pl.BlockSpec((GM_GROUPS, GM_CHUNK, GM_CHUNK), lambda i: (0, 0, 0)),
                  pl.BlockSpec((GM_CHUNK, GM_GROUPS), const2),
                  pl.BlockSpec((tb, d), lambda i: (i, 0)),
                  pl.BlockSpec((d, d), const2)],
        out_specs=pl.BlockSpec((tb, d), lambda i: (i, 0)),
        scratch_shapes=[pltpu.VMEM((tb, d), BF16)],
        compiler_params=_params("parallel"),
    )(z, z, ln_g.reshape(1, d).astype(F32), ln_b.reshape(1, d).astype(F32),
      w_s.astype(F32), b_s.T.astype(F32), x, w_out)


def _gdn_body(y_ref, z_ref, ba_ref, alog_ref, dtb_ref, on_ref, x_ref, wo_ref, o_ref, s_ref, m_ref, *, tb):
    c = GD_CHUNK
    hd = GD_HEAD_DIM
    nq = GD_QK_HEADS * hd

    @pl.when(pl.program_id(1) == 0)
    def _():
        s_ref[...] = jnp.zeros_like(s_ref)

    row = lax.broadcasted_iota(jnp.int32, (c, c), 0)
    col = lax.broadcasted_iota(jnp.int32, (c, c), 1)
    tri = row >= col
    tri_strict = row > col
    tri_bf = tri.astype(BF16)
    eye = (row == col).astype(F32)
    lane = lax.broadcasted_iota(jnp.int32, (1, LANES), 1)
    neg_a = -jnp.exp(alog_ref[...])
    on = on_ref[...]

    heads = range(GD_V_HEADS)
    rep = GD_V_HEADS // GD_QK_HEADS
    col = lambda a, hv: a[:, hv:hv + 1]
    gcol = lambda a, hv: a[:, GD_V_HEADS + hv:GD_V_HEADS + hv + 1]
    n_par = GD_PAR_CHUNKS

    def group(gi, carry):
        rows = [pl.ds(pl.multiple_of((gi * n_par + j) * c, c), c) for j in range(n_par)]
        par = range(n_par)
        jobs = [(j, hv) for j in par for hv in heads]
        ba = [ba_ref[r, :] for r in rows]
        beta_all = [jax.nn.sigmoid(a) for a in ba]
        in_rate_lanes = (lane >= GD_V_HEADS) & (lane < 2 * GD_V_HEADS)
        g_all = [jnp.where(in_rate_lanes, neg_a * jax.nn.softplus(a + dtb_ref[...]), 0.0) for a in ba]
        gam_all = [_dot_split(tri_bf, a) for a in g_all]
        gam_rows = [a.T for a in gam_all]
        e_gam_all = [jnp.exp(a) for a in gam_all]
        gam_last_all = [a[c - 1:c, :] for a in gam_all]
        e_end_all = [jnp.exp(b - a) for a, b in zip(gam_all, gam_last_all)]
        d_end_all = [jnp.exp(a) for a in gam_last_all]
        q_bf = [[y_ref[r, hk * hd:(hk + 1) * hd] for hk in range(GD_QK_HEADS)] for r in rows]
        k_bf = [[y_ref[r, nq + hk * hd:nq + (hk + 1) * hd] for hk in range(GD_QK_HEADS)] for r in rows]
        q = [[a.astype(F32) for a in qj] for qj in q_bf]
        k = [[a.astype(F32) for a in kj] for kj in k_bf]
        kk = [[lax.dot_general(a, a, NT_DIMS, preferred_element_type=F32) for a in kj] for kj in k_bf]
        qk = [[lax.dot_general(a, b, NT_DIMS, preferred_element_type=F32)
               for a, b in zip(q_bf[j], k_bf[j])] for j in par]
        decay = [jnp.exp(jnp.where(
            tri, gcol(gam_all[j], hv) - gam_rows[j][GD_V_HEADS + hv:GD_V_HEADS + hv + 1, :], -jnp.inf))
            for j, hv in jobs]
        p = [-jnp.where(tri_strict, kk[j][hv // rep] * decay[n] * col(beta_all[j], hv), 0.0)
             for n, (j, hv) in enumerate(jobs)]
        rhs = [jnp.concatenate(
            [y_ref[rows[j], 2 * nq + hv * hd:2 * nq + (hv + 1) * hd].astype(F32) * col(beta_all[j], hv),
             k[j][hv // rep] * (col(beta_all[j], hv) * gcol(e_gam_all[j], hv))], axis=-1).astype(BF16)
            for j, hv in jobs]
        u = [eye + a for a in p]
        p_bf = [a.astype(BF16) for a in p]
        n_steps = c.bit_length() - 1
        for step in range(1, n_steps):
            p_bf = [_dot(pb, pb).astype(BF16) for pb in p_bf]
            u = [a + _dot(a.astype(BF16), pb) for a, pb in zip(u, p_bf)]
        sol = [_dot(a.astype(BF16), b) for a, b in zip(u, rhs)]
        q_dec = [(q[j][hv // rep] * gcol(e_gam_all[j], hv)).astype(BF16) for j, hv in jobs]
        qk_dec = [(qk[j][hv // rep] * decay[n]).astype(BF16) for n, (j, hv) in enumerate(jobs)]
        k_end = [(k[j][hv // rep] * gcol(e_end_all[j], hv)).astype(BF16) for j, hv in jobs]
        gate = [_silu(z_ref[rows[j], hv * hd:(hv + 1) * hd].astype(F32)) for j, hv in jobs]
        for j in par:
            at = lambda hv: j * GD_V_HEADS + hv
            s_old = [s_ref[hv] for hv in heads]
            s_bf = [a.astype(BF16) for a in s_old]
            nv_bf = [(sol[at(hv)][:, :hd] - _dot(sol[at(hv)][:, hd:].astype(BF16), s_bf[hv])).astype(BF16)
                     for hv in heads]
            o = [_dot(q_dec[at(hv)], s_bf[hv]) + _dot(qk_dec[at(hv)], nv_bf[hv]) for hv in heads]
            s_new = [s_old[hv] * gcol(d_end_all[j], hv) + lax.dot_general(
                k_end[at(hv)], nv_bf[hv], TN_DIMS, preferred_element_type=F32) for hv in heads]
            for hv in heads:
                s_ref[hv] = s_new[hv]
            outs = [_rms(o[hv], on) * gate[at(hv)] for hv in heads]
            m_ref[rows[j], :] = jnp.concatenate(outs, axis=-1).astype(m_ref.dtype)
        return carry

    lax.fori_loop(0, tb // (c * n_par), group, 0)
    o_ref[...] = x_ref[...] + _dot(m_ref[...], wo_ref[...])


def gdn_recurrence(y, z, ba, a_log, dt_bias, o_norm, x, w_out, batch, seq, tb=256):
    n = batch * seq
    tb = _row_tile(seq, tb)
    nt = seq // tb
    pad_heads = lambda a: jnp.zeros((1, LANES), F32).at[0, GD_V_HEADS:2 * GD_V_HEADS].set(a.astype(F32))
    row = lambda b, t: (b * nt + t, 0)
    const = lambda b, t: (0, 0)
    return pl.pallas_call(
        functools.partial(_gdn_body, tb=tb),
        name="gdn_recurrence",
        out_shape=jax.ShapeDtypeStruct((n, D_MODEL), F32),
        grid=(batch, nt),
        in_specs=[pl.BlockSpec((tb, GD_QKV), row),
                  pl.BlockSpec((tb, GD_Z), row),
                  pl.BlockSpec((tb, LANES), row),
                  pl.BlockSpec((1, LANES), const),
                  pl.BlockSpec((1, LANES), const),
                  pl.BlockSpec((1, GD_HEAD_DIM), const),
                  pl.BlockSpec((tb, D_MODEL), row),
                  pl.BlockSpec((GD_Z, D_MODEL), const)],
        out_specs=pl.BlockSpec((tb, D_MODEL), row),
        scratch_shapes=[pltpu.VMEM((GD_V_HEADS, GD_HEAD_DIM, GD_HEAD_DIM), F32), pltpu.VMEM((tb, GD_Z), BF16)],
        compiler_params=_params("parallel", "arbitrary"),
    )(y, z, ba, pad_heads(a_log), pad_heads(dt_bias), o_norm.reshape(1, GD_HEAD_DIM).astype(F32), x, w_out)


def _gdn_in_body(x_ref, g_ref, w_ref, cw_ref, y_ref, z_ref, ba_ref, tail_ref, *, tiles_per_seq, chunk):
    tm = x_ref.shape[0]
    hd = GD_HEAD_DIM
    nq = GD_QK_HEADS * hd
    halo = SUBLANES
    first = pl.program_id(0) % tiles_per_seq == 0
    h = _rms(x_ref[...], g_ref[...]).astype(BF16)
    pres = [_dot(h, w_ref[:, c:c + chunk]) for c in range(0, GD_QKV, chunk)]
    for c, pre in zip(range(0, GD_QKV, chunk), pres):
        cols = slice(c, c + chunk)
        tail = jnp.where(first, 0.0, tail_ref[:, cols])
        tail_ref[:, cols] = pre[tm - halo:tm, :]
        ext = jnp.concatenate([tail, pre], axis=0)
        acc = cw_ref[GD_CONV - 1:GD_CONV, cols] * pre
        for jj in range(GD_CONV - 1):
            back = GD_CONV - 1 - jj
            acc = acc + cw_ref[jj:jj + 1, cols] * ext[halo - back:halo - back + tm, :]
        y = _silu(acc)
        if c < 2 * nq:
            parts = []
            for hh in range(chunk // hd):
                a = y[:, hh * hd:(hh + 1) * hd]
                a = a * lax.rsqrt(jnp.sum(a * a, axis=-1, keepdims=True) + EPS)
                parts.append(a * (hd ** -0.5) if c < nq else a)
            y = jnp.concatenate(parts, axis=-1)
        y_ref[:, cols] = y.astype(y_ref.dtype)
    for c in range(0, GD_Z, chunk):
        z_ref[:, c:c + chunk] = _dot(h, w_ref[:, GD_QKV + c:GD_QKV + c + chunk]).astype(z_ref.dtype)
    ba_ref[...] = _dot(h, w_ref[:, GD_QKV + GD_Z:])


def gdn_in_proj(x, g, w_all, conv_w, seq, tm=256, chunk=256):
    n, d = x.shape
    m = w_all.shape[1]
    tm = _row_tile(seq, tm)
    assert nq_aligned(chunk) and m == GD_QKV + GD_Z + LANES
    const = lambda i: (0, 0)
    row = lambda i: (i, 0)
    return pl.pallas_call(
        functools.partial(_gdn_in_body, tiles_per_seq=seq // tm, chunk=chunk),
        name="gdn_in_proj",
        out_shape=[jax.ShapeDtypeStruct((n, GD_QKV), BF16), jax.ShapeDtypeStruct((n, GD_Z), BF16),
                   jax.ShapeDtypeStruct((n, LANES), F32)],
        grid=(n // tm,),
        in_specs=[pl.BlockSpec((tm, d), row), pl.BlockSpec((1, d), const), pl.BlockSpec((d, m), const),
                  pl.BlockSpec((GD_CONV, GD_QKV), const)],
        out_specs=[pl.BlockSpec((tm, GD_QKV), row), pl.BlockSpec((tm, GD_Z), row), pl.BlockSpec((tm, LANES), row)],
        scratch_shapes=[pltpu.VMEM((SUBLANES, GD_QKV), F32)],
        compiler_params=_params("arbitrary"),
    )(x, g.reshape(1, d).astype(F32), w_all, conv_w.astype(F32))


def nq_aligned(chunk):
    nq = GD_QK_HEADS * GD_HEAD_DIM
    return chunk % GD_HEAD_DIM == 0 and nq % chunk == 0 and GD_QKV % chunk == 0 and GD_Z % chunk == 0


def _router_body(x_ref, g_ref, wr_ref, o_ref):
    h = _rms(x_ref[...], g_ref[...])
    w = wr_ref[...]
    h_hi = h.astype(BF16)
    h_lo = (h - h_hi.astype(F32)).astype(BF16)
    w_hi = w.astype(BF16)
    w_lo = (w - w_hi.astype(F32)).astype(BF16)
    logits = _dot(h_hi, w_hi) + (_dot(h_lo, w_hi) + _dot(h_hi, w_lo))
    lane = lax.broadcasted_iota(jnp.int32, logits.shape, 1)
    logits = jnp.where(lane < N_EXPERTS, logits, -jnp.inf)
    m1 = jnp.max(logits, axis=-1, keepdims=True)
    i1 = jnp.min(jnp.where(logits == m1, lane, LANES), axis=-1, keepdims=True)
    rest = jnp.where(lane == i1, -jnp.inf, logits)
    m2 = jnp.max(rest, axis=-1, keepdims=True)
    i2 = jnp.min(jnp.where(rest == m2, lane, LANES), axis=-1, keepdims=True)
    e2 = jnp.exp(m2 - m1)
    w1 = 1.0 / (1.0 + e2)
    w2 = e2 / (1.0 + e2)
    out = jnp.where(lane == 0, i1.astype(F32),
                    jnp.where(lane == 1, i2.astype(F32),
                              jnp.where(lane == 2, w1, jnp.where(lane == 3, w2, 0.0))))
    o_ref[...] = out


def moe_router(x, g, router, tm=512):
    n, d = x.shape
    tm = _row_tile(n, tm)
    wr = jnp.zeros((d, LANES), F32).at[:, :N_EXPERTS].set(router.astype(F32))
    return pl.pallas_call(
        _router_body,
        name="moe_router",
        out_shape=jax.ShapeDtypeStruct((n, LANES), F32),
        grid=(n // tm,),
        in_specs=[pl.BlockSpec((tm, d), lambda i: (i, 0)),
                  pl.BlockSpec((1, d), lambda i: (0, 0)),
                  pl.BlockSpec((d, LANES), lambda i: (0, 0))],
        out_specs=pl.BlockSpec((tm, LANES), lambda i: (i, 0)),
        compiler_params=_params("parallel"),
    )(x, g.reshape(1, d).astype(F32), wr)


def _row_copy(src_ref, src_row, dst_ref, dst_row, sem):
    return pltpu.make_async_copy(src_ref.at[pl.ds(src_row, 1), :], dst_ref.at[pl.ds(dst_row, 1), :], sem)


def _dispatch_body(ends_ref, padded_ref, pos_ref, x_ref, xs_ref, zero_ref, sem, zero_sem, *, rt, tm):
    @pl.when(pl.program_id(0) == 0)
    def _():
        zero_ref[...] = jnp.zeros_like(zero_ref)

        def zero_tile(row0):
            row0 = row0 if isinstance(row0, int) else pl.multiple_of(row0, tm)
            cp = pltpu.make_async_copy(zero_ref, xs_ref.at[pl.ds(row0, tm), :], zero_sem)
            cp.start()
            cp.wait()

        for e in range(N_EXPERTS):
            pl.when(padded_ref[e] > 0)(functools.partial(zero_tile, ends_ref[e] - tm))
        n_rows = xs_ref.shape[0]
        for back in range(1, N_EXPERTS + 1):
            row0 = n_rows - back * tm
            pl.when(row0 >= ends_ref[N_EXPERTS - 1])(functools.partial(zero_tile, row0))

    def start(g, carry):
        r0 = pl.multiple_of(g * SUBLANES, SUBLANES)
        for u in range(SUBLANES):
            for k in range(TOP_K):
                _row_copy(x_ref, r0 + u, xs_ref, pos_ref[0, 0, TOP_K * (r0 + u) + k], sem).start()
        return carry

    lax.fori_loop(0, rt // SUBLANES, start, 0)
    all_rows = xs_ref.at[pl.ds(0, TOP_K * rt), :]
    pltpu.make_async_copy(all_rows, all_rows, sem).wait()


def moe_dispatch(x, pos, ends, padded, n_rows, tm, rt=512):
    n, d = x.shape
    rt = _row_tile(n, rt)
    grid_spec = pltpu.PrefetchScalarGridSpec(
        num_scalar_prefetch=2,
        grid=(n // rt,),
        in_specs=[pl.BlockSpec((1, 1, TOP_K * rt), lambda i, en, pa: (i, 0, 0), memory_space=pltpu.SMEM),
                  pl.BlockSpec((rt, d), lambda i, en, pa: (i, 0))],
        out_specs=pl.BlockSpec(memory_space=pl.ANY),
        scratch_shapes=[pltpu.VMEM((tm, d), F32), pltpu.SemaphoreType.DMA(()),
                        pltpu.SemaphoreType.DMA(())],
    )
    return pl.pallas_call(
        functools.partial(_dispatch_body, rt=rt, tm=tm),
        name="moe_dispatch",
        out_shape=jax.ShapeDtypeStruct((n_rows, d), F32),
        grid_spec=grid_spec,
        compiler_params=_params("arbitrary"),
    )(ends, padded, pos.reshape(n // rt, 1, TOP_K * rt), x)


def _expert_changed(te_ref, i):
    return (i == 0) | (te_ref[i] != te_ref[jnp.maximum(i - 1, 0)])


def _moe_up_body(te_ref, nt_ref, xs_ref, g_ref, wg_ref, wu_ref, o_ref, wg_bf, wu_bf, *, chunk):
    i = pl.program_id(1)
    f = o_ref.shape[1]

    @pl.when(_expert_changed(te_ref, i))
    def _():
        for c in range(0, f, chunk):
            wg_bf[:, c:c + chunk] = wg_ref[:, c:c + chunk].astype(BF16)
            wu_bf[:, c:c + chunk] = wu_ref[:, c:c + chunk].astype(BF16)

    @pl.when(i < nt_ref[0])
    def _():
        h = _rms(xs_ref[...], g_ref[...]).astype(BF16)
        for c in range(0, f, chunk):
            a = _dot(h, wg_bf[:, c:c + chunk])
            b = _dot(h, wu_bf[:, c:c + chunk])
            o_ref[:, c:c + chunk] = (_silu(a) * b).astype(o_ref.dtype)

    @pl.when(i >= nt_ref[0])
    def _():
        o_ref[...] = jnp.zeros_like(o_ref)


def moe_up(xs, g, wg, wu, layer, tile_expert, n_tiles_used, tm, fsplit=2, chunk=256):
    p, d = xs.shape
    f = wg.shape[3]
    fb = f // fsplit
    assert fb % chunk == 0 and p % tm == 0
    w_spec = pl.BlockSpec((None, None, d, fb), lambda j, i, te, nt: (layer, te[i], 0, j))
    grid_spec = pltpu.PrefetchScalarGridSpec(
        num_scalar_prefetch=2,
        grid=(fsplit, p // tm),
        in_specs=[pl.BlockSpec((tm, d), lambda j, i, te, nt: (jnp.minimum(i, nt[0] - 1), 0)),
                  pl.BlockSpec((1, d), lambda j, i, te, nt: (0, 0)),
                  w_spec, w_spec],
        out_specs=pl.BlockSpec((tm, fb), lambda j, i, te, nt: (i, j)),
        scratch_shapes=[pltpu.VMEM((d, fb), BF16), pltpu.VMEM((d, fb), BF16)],
    )
    return pl.pallas_call(
        functools.partial(_moe_up_body, chunk=chunk),
        name="moe_up",
        out_shape=jax.ShapeDtypeStruct((p, f), BF16),
        grid_spec=grid_spec,
        compiler_params=_params("arbitrary", "arbitrary"),
    )(tile_expert, n_tiles_used, xs, g.reshape(1, d).astype(F32), wg, wu)


def _moe_down_body(te_ref, nt_ref, a_ref, wd_ref, o_ref, wd_bf, *, chunk):
    i = pl.program_id(0)

    @pl.when(_expert_changed(te_ref, i))
    def _():
        for r in range(0, wd_bf.shape[0], chunk):
            wd_bf[r:r + chunk, :] = wd_ref[r:r + chunk, :].astype(BF16)

    @pl.when(i < nt_ref[0])
    def _():
        o_ref[...] = _dot(a_ref[...], wd_bf[...])

    @pl.when(i >= nt_ref[0])
    def _():
        o_ref[...] = jnp.zeros_like(o_ref)


def moe_down(act, wd, layer, tile_expert, n_tiles_used, tm, chunk=512):
    p, f = act.shape
    d = wd.shape[3]
    assert f % chunk == 0
    grid_spec = pltpu.PrefetchScalarGridSpec(
        num_scalar_prefetch=2,
        grid=(p // tm,),
        in_specs=[pl.BlockSpec((tm, f), lambda i, te, nt: (i, 0)),
                  pl.BlockSpec((None, None, f, d), lambda i, te, nt: (layer, te[i], 0, 0))],
        out_specs=pl.BlockSpec((tm, d), lambda i, te, nt: (i, 0)),
        scratch_shapes=[pltpu.VMEM((f, d), BF16)],
    )
    return pl.pallas_call(
        functools.partial(_moe_down_body, chunk=chunk),
        name="moe_down",
        out_shape=jax.ShapeDtypeStruct((p, d), F32),
        grid_spec=grid_spec,
        compiler_params=_params("arbitrary"),
    )(tile_expert, n_tiles_used, act, wd)


def _combine_body(pos_ref, pos_next_ref, x_ref, r_ref, ys_ref, o_ref, buf_ref, sem, *, rt):
    i = pl.program_id(0)
    slot = i % 2

    def gather(p_ref, s):
        def start(g, carry):
            r0 = pl.multiple_of(g * SUBLANES, SUBLANES)
            for u in range(SUBLANES):
                for k in range(TOP_K):
                    _row_copy(ys_ref, p_ref[0, 0, TOP_K * (r0 + u) + k], buf_ref.at[s, k], r0 + u,
                              sem.at[s]).start()
            return carry

        lax.fori_loop(0, rt // SUBLANES, start, 0)

    @pl.when(i == 0)
    def _():
        gather(pos_ref, 0)

    @pl.when(i + 1 < pl.num_programs(0))
    def _():
        gather(pos_next_ref, 1 - slot)

    for k in range(TOP_K):
        pltpu.make_async_copy(ys_ref.at[pl.ds(0, rt), :], buf_ref.at[slot, k], sem.at[slot]).wait()
    r = r_ref[...]
    o_ref[...] = x_ref[...] + r[:, 2:3] * buf_ref[slot, 0] + r[:, 3:4] * buf_ref[slot, 1]


def moe_combine(x, route, ys, pos, rt=256):
    n, d = x.shape
    rt = _row_tile(n, rt)
    steps = n // rt
    pos3 = pos.reshape(steps, 1, TOP_K * rt)
    grid_spec = pltpu.PrefetchScalarGridSpec(
        num_scalar_prefetch=0,
        grid=(steps,),
        in_specs=[pl.BlockSpec((1, 1, TOP_K * rt), lambda i: (i, 0, 0), memory_space=pltpu.SMEM),
                  pl.BlockSpec((1, 1, TOP_K * rt), lambda i: (jnp.minimum(i + 1, steps - 1), 0, 0),
                               memory_space=pltpu.SMEM),
                  pl.BlockSpec((rt, d), lambda i: (i, 0)),
                  pl.BlockSpec((rt, LANES), lambda i: (i, 0)),
                  pl.BlockSpec(memory_space=pl.ANY)],
        out_specs=pl.BlockSpec((rt, d), lambda i: (i, 0)),
        scratch_shapes=[pltpu.VMEM((2, TOP_K, rt, d), F32), pltpu.SemaphoreType.DMA((2,))],
    )
    return pl.pallas_call(
        functools.partial(_combine_body, rt=rt),
        name="moe_combine",
        out_shape=jax.ShapeDtypeStruct((n, d), F32),
        grid_spec=grid_spec,
        compiler_params=_params("arbitrary"),
    )(pos3, pos3, x, route, ys)


def moe_ffn(x, g, router, wg, wu, wd, layer, tm=512):
    n, d = x.shape
    route = moe_router(x, g, router)
    expert = route[:, :TOP_K].astype(jnp.int32).reshape(n * TOP_K)
    onehot = (expert[:, None] == jnp.arange(N_EXPERTS, dtype=jnp.int32)[None, :]).astype(jnp.int32)
    csum = jnp.cumsum(onehot, axis=0)
    rank = jnp.sum((csum - onehot) * onehot, axis=1)
    counts = csum[-1]
    padded = ((counts + tm - 1) // tm) * tm
    ends = jnp.cumsum(padded)
    starts = ends - padded
    pos = (jnp.sum(starts[None, :] * onehot, axis=1) + rank).astype(jnp.int32)
    n_rows = n * TOP_K + N_EXPERTS * tm
    n_tiles = n_rows // tm
    tile_start = jnp.arange(n_tiles, dtype=jnp.int32) * tm
    tile_expert = jnp.minimum(jnp.sum((tile_start[:, None] >= ends[None, :]).astype(jnp.int32), axis=1),
                              N_EXPERTS - 1).astype(jnp.int32)
    n_tiles_used = (ends[-1:] // tm).astype(jnp.int32)

    xs = moe_dispatch(x, pos, ends.astype(jnp.int32), padded.astype(jnp.int32), n_rows, tm)
    act = moe_up(xs, g, wg, wu, layer, tile_expert, n_tiles_used, tm)
    ys = moe_down(act, wd, layer, tile_expert, n_tiles_used, tm)
    return moe_combine(x, route, ys, pos)


def hgrn2_layer(x, g, w_in, o_norm, w_out, lower_bound, batch, seq):
    (qfig,) = norm_matmul(x, g, w_in.astype(BF16), [(4 * D_MODEL, BF16)])
    return hgrn_recurrence(qfig, lower_bound, o_norm, x, w_out.astype(BF16), batch, seq)


def swa_layer(x, g, positions, w_in, q_norm, k_norm, sinks, w_out, batch, seq):
    q, kv = norm_matmul(x, g, w_in.astype(BF16),
                        [(D_MODEL, BF16), (2 * SW_KV_HEADS * SW_HEAD_DIM, BF16)])
    pos = positions.astype(F32).reshape(batch * seq, 1)
    return swa_attention(q, kv, pos, q_norm, k_norm, sinks, x, w_out.astype(BF16), batch, seq)


def gmlp_layer(x, g, w_in, b_in, ln_g, ln_b, w_s, b_s, w_out):
    (z,) = norm_matmul(x, g, w_in.astype(BF16), [(2 * GM_WIDTH, BF16)], bias=b_in, act="gelu")
    return gmlp_spatial(z, ln_g, ln_b, w_s, b_s, x, w_out.astype(BF16))


def gdn_layer(x, g, w_in, conv_w, a_log, dt_bias, o_norm, w_out, batch, seq):
    d = x.shape[1]
    n_small = w_in.shape[1] - GD_QKV - GD_Z
    w_main = w_in[:, :GD_QKV + GD_Z].astype(BF16)
    w_small = jnp.zeros((d, LANES), F32).at[:, :n_small].set(w_in[:, GD_QKV + GD_Z:].astype(F32))
    w_all = jnp.concatenate([w_main, w_small.astype(BF16)], axis=1)
    y, z, ba = gdn_in_proj(x, g, w_all, conv_w, seq)
    return gdn_recurrence(y, z, ba, a_log, dt_bias, o_norm, x, w_out.astype(BF16), batch, seq)


def dense_ffn(x, g, w_gate, w_up, w_down):
    act = swiglu_up(x, g, w_gate.astype(BF16), w_up.astype(BF16))
    return matmul_residual(act, w_down.astype(BF16), x)


def kernel(x, positions, mix_norm, ffn_norm, hgrn_lb_logits, hgrn_w_in, hgrn_o_norm, hgrn_w_out,
           swa_w_in, swa_q_norm, swa_k_norm, swa_sinks, swa_w_out,
           gmlp_w_in, gmlp_b_in, gmlp_v_ln_g, gmlp_v_ln_b, gmlp_w_s, gmlp_b_s, gmlp_w_out,
           gdn_w_in, gdn_conv_w, gdn_a_log, gdn_dt_bias, gdn_o_norm, gdn_w_out,
           dense_w_gate, dense_w_up, dense_w_down,
           moe_router, moe_w_gate, moe_w_up, moe_w_down):
    batch, seq, d = x.shape
    depth = mix_norm.shape[0]
    lower_bounds = jnp.cumsum(jax.nn.softmax(hgrn_lb_logits.astype(F32), axis=0), axis=0)
    h = x.reshape(batch * seq, d)
    for i in range(depth):
        kind, j = i % 4, i // 4
        if kind == 0:
            h = hgrn2_layer(h, mix_norm[i], hgrn_w_in[j], hgrn_o_norm[j], hgrn_w_out[j],
                            lower_bounds[i], batch, seq)
        elif kind == 1:
            h = swa_layer(h, mix_norm[i], positions, swa_w_in[j], swa_q_norm[j], swa_k_norm[j],
                          swa_sinks[j], swa_w_out[j], batch, seq)
        elif kind == 2:
            h = gmlp_layer(h, mix_norm[i], gmlp_w_in[j], gmlp_b_in[j], gmlp_v_ln_g[j], gmlp_v_ln_b[j],
                           gmlp_w_s[j], gmlp_b_s[j], gmlp_w_out[j])
        else:
            h = gdn_layer(h, mix_norm[i], gdn_w_in[j], gdn_conv_w[j], gdn_a_log[j], gdn_dt_bias[j],
                          gdn_o_norm[j], gdn_w_out[j], batch, seq)
        if i % 2 == 0:
            h = dense_ffn(h, ffn_norm[i], dense_w_gate[i // 2], dense_w_up[i // 2], dense_w_down[i // 2])
        else:
            h = moe_ffn(h, ffn_norm[i], moe_router[i // 2], moe_w_gate, moe_w_up, moe_w_down, i // 2)
    return h.reshape(batch, seq, d)
```
